```python
import jax, jax.numpy as jnp
from jax import lax
import numpy as np

D_MODEL = 1024
BATCH = 2
SEQ = 8192
DEPTH = 4

GRID_W = 64
NA_HEADS = 8
NA_HEAD_DIM = 64
NA_WIDTH = NA_HEADS * NA_HEAD_DIM
NA_KH_MAX = 8
NA_KW = 16
FN_GROUPS = 8
FN_GROUP_DIM = 64
FN_WIDTH = FN_GROUPS * FN_GROUP_DIM
LRU_BLOCKS = 8
LRU_BLOCK_DIM = 64
LRU_WIDTH = LRU_BLOCKS * LRU_BLOCK_DIM
LRU_C = 8.0
CONV_W = 4
CONV_PAD_LEFT = 2
N_BRANCH = 3
BRANCH_WIDTH = 512
SPLITS = [NA_WIDTH, 2 * NA_WIDTH, 3 * NA_WIDTH, 3 * NA_WIDTH + FN_WIDTH,
          3 * NA_WIDTH + FN_WIDTH + LRU_WIDTH, 3 * NA_WIDTH + FN_WIDTH + 2 * LRU_WIDTH]
IN_COLS = 3 * NA_WIDTH + FN_WIDTH + 2 * LRU_WIDTH + N_BRANCH * D_MODEL
D_FF = 3 * D_MODEL
N_EXPERTS = 8
TOP_K = 2
D_FF_EXPERT = D_FF // TOP_K
N_DENSE = (DEPTH + 1) // 2
N_MOE = DEPTH // 2
RMS_EPS = 1e-6

kernel_name = 'hybrid_na_fnet_rglru_moe_encoder'


def rms_norm(x, g):
    xf = x.astype(jnp.float32)
    y = xf * lax.rsqrt(jnp.mean(xf * xf, axis=-1, keepdims=True) + RMS_EPS)
    return (y * g.astype(jnp.float32)).astype(x.dtype)


def neighbourhood_attention(q, k, v, rpb, rows):
    b, s, _ = q.shape
    kh = min(NA_KH_MAX, rows)

    def to_grid(t):
        return t.reshape(b, rows, GRID_W, NA_HEADS, NA_HEAD_DIM).transpose(0, 3, 1, 2, 4)

    qg, kg, vg = to_grid(q), to_grid(k), to_grid(v)
    cols = np.arange(GRID_W)
    col_start = np.clip(cols - NA_KW // 2, 0, GRID_W - NA_KW)
    col_idx = col_start[:, None] + np.arange(NA_KW)[None, :]
    col_off = col_idx - cols[:, None] + (NA_KW - 1)
    rpb_c = rpb.astype(jnp.float32)[:, :, col_off]
    scale = NA_HEAD_DIM ** -0.5

    def one_row(r):
        rs = jnp.clip(r - kh // 2, 0, rows - kh)
        k_band = lax.dynamic_slice_in_dim(kg, rs, kh, axis=2)
        v_band = lax.dynamic_slice_in_dim(vg, rs, kh, axis=2)
        q_row = lax.dynamic_index_in_dim(qg, r, axis=2, keepdims=False)
        k_win = k_band[:, :, :, col_idx, :]
        v_win = v_band[:, :, :, col_idx, :]
        row_off = rs + jnp.arange(kh) - r + (NA_KH_MAX - 1)
        bias = jnp.take(rpb_c, row_off, axis=1).transpose(0, 2, 1, 3)
        scores = jnp.einsum('bhcd,bhicjd->bhcij', q_row, k_win,
                            preferred_element_type=jnp.float32) * scale + bias[None]
        p = jax.nn.softmax(scores.reshape(b, NA_HEADS, GRID_W, kh * NA_KW), axis=-1)
        p = p.reshape(scores.shape).astype(v.dtype)
        return jnp.einsum('bhcij,bhicjd->bhcd', p, v_win)

    out = lax.map(one_row, jnp.arange(rows))
    return out.transpose(1, 0, 3, 2, 4).reshape(b, s, NA_WIDTH)


def fourier_mix(u):
    b, s, _ = u.shape
    ug = u.astype(jnp.float32).reshape(b, s, FN_GROUPS, FN_GROUP_DIM)
    f = jnp.fft.fft2(ug, axes=(1, 3), norm='ortho').real
    return f.reshape(b, s, FN_WIDTH).astype(u.dtype)


def centred_depthwise_conv(u, w, bias):
    s = u.shape[1]
    up = jnp.pad(u, ((0, 0), (CONV_PAD_LEFT, CONV_W - 1 - CONV_PAD_LEFT), (0, 0)))
    y = up[:, 0:s] * w[0]
    for t in range(1, CONV_W):
        y = y + up[:, t:t + s] * w[t]
    return y + bias


def block_diag(u, w, bias):
    b, s, _ = u.shape
    ub = u.reshape(b, s, LRU_BLOCKS, LRU_BLOCK_DIM)
    return jnp.einsum('bsnc,ncd->bsnd', ub, w).reshape(b, s, LRU_WIDTH) + bias


def rg_lru_direction(u, wa, ba, wx, bx, lam, reverse):
    s = u.shape[1]
    uf = u.astype(jnp.float32)
    r = jax.nn.sigmoid(block_diag(u, wa, ba).astype(jnp.float32))
    i = jax.nn.sigmoid(block_diag(u, wx, bx).astype(jnp.float32))
    log_a = -LRU_C * r * jax.nn.softplus(-lam.astype(jnp.float32))
    a = jnp.exp(log_a)
    mult = jnp.sqrt(-jnp.expm1(2.0 * log_a))
    first = s - 1 if reverse else 0
    mult = jnp.where((jnp.arange(s) == first)[None, :, None], jnp.ones_like(mult), mult)
    bterm = mult * i * uf

    def combine(c1, c2):
        a1, b1 = c1
        a2, b2 = c2
        return a1 * a2, a2 * b1 + b2

    _, h = lax.associative_scan(combine, (a, bterm), axis=1, reverse=reverse)
    return h


def recurrent_branch(u_x, u_gate, conv_w, conv_b, wa, ba, wx, bx, lam):
    c = centred_depthwise_conv(u_x, conv_w, conv_b)
    h = (rg_lru_direction(c, wa[0], ba[0], wx[0], bx[0], lam[0], False)
         + rg_lru_direction(c, wa[1], ba[1], wx[1], bx[1], lam[1], True))
    return (h * jax.nn.gelu(u_gate.astype(jnp.float32))).astype(u_x.dtype)


def hybrid_mixer(xn, rows, w_in, b_in, rpb, conv_w, conv_b, wa, ba, wx, bx, lam, w_branch, w_out):
    b, s, d = xn.shape
    proj = xn @ w_in + b_in
    q, k, v, u_f, u_x, u_g, gates = jnp.split(proj, SPLITS, axis=-1)
    y_a = neighbourhood_attention(q, k, v, rpb, rows)
    y_b = fourier_mix(u_f)
    y_c = recurrent_branch(u_x, u_g, conv_w, conv_b, wa, ba, wx, bx, lam)
    br = jnp.stack([y_a, y_b, y_c], axis=2)
    y_br = jnp.einsum('bskc,kcd->bskd', br, w_branch)
    g = jax.nn.sigmoid(gates.astype(jnp.float32)).reshape(b, s, N_BRANCH, d)
    merged = jnp.sum(g * y_br.astype(jnp.float32), axis=2).astype(xn.dtype)
    return merged @ w_out


def swiglu(x, w_gu, w_down):
    g, u = jnp.split(x @ w_gu, 2, axis=-1)
    return (jax.nn.silu(g) * u) @ w_down


def moe_swiglu(x, w_router, w_gu, w_down):
    b, s, d = x.shape
    xt = x.reshape(b * s, d)
    logits = (xt @ w_router).astype(jnp.float32)
    top_v, top_i = lax.top_k(logits, TOP_K)
    top_w = jax.nn.softmax(top_v, axis=-1)
    comb = jnp.sum(jax.nn.one_hot(top_i, N_EXPERTS, dtype=jnp.float32) * top_w[..., None], axis=1)
    y = jnp.zeros((b * s, d), jnp.float32)
    for e in range(N_EXPERTS):
        y = y + comb[:, e:e + 1] * swiglu(xt, w_gu[e], w_down[e]).astype(jnp.float32)
    return y.astype(x.dtype).reshape(b, s, d)


def setup_inputs(seed: int = 0) -> dict:
    key = jax.random.key(seed)
    ks = jax.random.split(key, 24)
    f32 = jnp.float32

    def nrm(kk, shape, scale):
        return jax.random.normal(kk, shape, f32) * scale

    a_c = jax.random.uniform(ks[11], (DEPTH, 2, LRU_WIDTH), f32, 0.9, 0.999)
    a_base = a_c ** (1.0 / LRU_C)
    lru_lambda = jnp.log(a_base) - jnp.log1p(-a_base)
    return {
        'x': nrm(ks[0], (BATCH, SEQ, D_MODEL), 1.0),
        'norm_mix_g': 1.0 + nrm(ks[1], (DEPTH, D_MODEL), 0.02),
        'w_in': nrm(ks[2], (DEPTH, D_MODEL, IN_COLS), D_MODEL ** -0.5),
        'b_in': nrm(ks[3], (DEPTH, IN_COLS), 0.02),
        'na_rpb': nrm(ks[4], (DEPTH, NA_HEADS, 2 * NA_KH_MAX - 1, 2 * NA_KW - 1), 0.02),
        'conv_w': nrm(ks[5], (DEPTH, CONV_W, LRU_WIDTH), CONV_W ** -0.5),
        'conv_b': nrm(ks[6], (DEPTH, LRU_WIDTH), 0.02),
        'lru_wa': nrm(ks[7], (DEPTH, 2, LRU_BLOCKS, LRU_BLOCK_DIM, LRU_BLOCK_DIM), LRU_BLOCK_DIM ** -0.5),
        'lru_ba': nrm(ks[8], (DEPTH, 2, LRU_WIDTH), 0.02),
        'lru_wx': nrm(ks[9], (DEPTH, 2, LRU_BLOCKS, LRU_BLOCK_DIM, LRU_BLOCK_DIM), LRU_BLOCK_DIM ** -0.5),
        'lru_bx': nrm(ks[10], (DEPTH, 2, LRU_WIDTH), 0.02),
        'lru_lambda': lru_lambda,
        'w_branch': nrm(ks[12], (DEPTH, N_BRANCH, BRANCH_WIDTH, D_MODEL), BRANCH_WIDTH ** -0.5),
        'w_out': nrm(ks[13], (DEPTH, D_MODEL, D_MODEL), D_MODEL ** -0.5),
        'norm_ffn_g': 1.0 + nrm(ks[14], (DEPTH, D_MODEL), 0.02),
        'ffn_w_gu': nrm(ks[15], (N_DENSE, D_MODEL, 2 * D_FF), D_MODEL ** -0.5),
        'ffn_w_down': nrm(ks[16], (N_DENSE, D_FF, D_MODEL), D_FF ** -0.5),
        'router_w': nrm(ks[17], (N_MOE, D_MODEL, N_EXPERTS), D_MODEL ** -0.5),
        'moe_w_gu': nrm(ks[18], (N_MOE, N_EXPERTS, D_MODEL, 2 * D_FF_EXPERT), D_MODEL ** -0.5),
        'moe_w_down': nrm(ks[19], (N_MOE, N_EXPERTS, D_FF_EXPERT, D_MODEL), D_FF_EXPERT ** -0.5),
        'final_g': 1.0 + nrm(ks[20], (D_MODEL,), 0.02),
    }


def reference(x, norm_mix_g, w_in, b_in, na_rpb, conv_w, conv_b, lru_wa, lru_ba, lru_wx, lru_bx,
              lru_lambda, w_branch, w_out, norm_ffn_g, ffn_w_gu, ffn_w_down, router_w, moe_w_gu,
              moe_w_down, final_g):
    rows = x.shape[1] // GRID_W
    h = x
    for l in range(DEPTH):
        xn = rms_norm(h, norm_mix_g[l])
        h = h + hybrid_mixer(xn, rows, w_in[l], b_in[l], na_rpb[l], conv_w[l], conv_b[l],
                             lru_wa[l], lru_ba[l], lru_wx[l], lru_bx[l], lru_lambda[l],
                             w_branch[l], w_out[l])
        xn = rms_norm(h, norm_ffn_g[l])
        if l % 2 == 0:
            h = h + swiglu(xn, ffn_w_gu[l // 2], ffn_w_down[l // 2])
        else:
            h = h + moe_swiglu(xn, router_w[l // 2], moe_w_gu[l // 2], moe_w_down[l // 2])
    return rms_norm(h, final_g)
```

```python
import functools

import numpy as np
import jax
import jax.numpy as jnp
from jax import lax
from jax.experimental import pallas as pl
from jax.experimental.pallas import tpu as pltpu

F32 = jnp.float32
BF16 = jnp.bfloat16

RMS_EPS = 1e-6
GRID_W = 64
NA_HEADS = 8
NA_HEAD_DIM = 64
NA_KH = 8
NA_KW = 16
NA_ROWS_PER_BLOCK = 8
LRU_C = 8.0
CONV_W = 4
CONV_PAD_LEFT = 2
N_EXPERTS = 8
MASK_VALUE = -1e30

VMEM_LIMIT_BYTES = 56 * 1024 * 1024


def _cparams(*sem):
    return pltpu.CompilerParams(dimension_semantics=sem, vmem_limit_bytes=VMEM_LIMIT_BYTES)


def _rms_norm_f32(x, g):
    ms = jnp.mean(x * x, axis=-1, keepdims=True)
    return x * lax.rsqrt(ms + RMS_EPS) * g


def _sigmoid(x):
    return 1.0 / (1.0 + jnp.exp(-x))


def _norm_proj_body(x_ref, g_ref, w_ref, b_ref, *out_refs):
    xn = _rms_norm_f32(x_ref[...], g_ref[...]).astype(BF16)
    r = jnp.dot(xn, w_ref[...].astype(BF16), preferred_element_type=F32) + b_ref[...]
    width = r.shape[1] // len(out_refs)
    for i, o in enumerate(out_refs):
        o[...] = r[:, i * width:(i + 1) * width].astype(o.dtype)


def _norm_proj(h, g_all, w_all, b_all, layer, col_block0, n_col_blocks, tn, n_out, tm=1024):
    n, d = h.shape
    depth = w_all.shape[0]
    g3 = g_all.reshape(depth, 1, d)
    b3 = b_all.reshape(depth, 1, -1)
    width = tn // n_out
    out_shape = [jax.ShapeDtypeStruct((n, n_col_blocks * width), BF16) for _ in range(n_out)]
    return pl.pallas_call(
        _norm_proj_body,
        out_shape=out_shape,
        grid=(n_col_blocks, n // tm),
        in_specs=[
            pl.BlockSpec((tm, d), lambda c, t: (t, 0)),
            pl.BlockSpec((None, 1, d), lambda c, t: (layer, 0, 0)),
            pl.BlockSpec((None, d, tn), lambda c, t: (layer, 0, col_block0 + c)),
            pl.BlockSpec((None, 1, tn), lambda c, t: (layer, 0, col_block0 + c)),
        ],
        out_specs=[pl.BlockSpec((tm, width), lambda c, t: (t, c)) for _ in range(n_out)],
        compiler_params=_cparams("arbitrary", "arbitrary"),
        name="norm_proj",
    )(h, g3, w_all, b3)


def _na_bias_tables(rpb_all):
    cols = np.arange(GRID_W)
    col_start = np.clip(cols - NA_KW // 2, 0, GRID_W - NA_KW)
    cc = np.arange(GRID_W)[None, :]
    in_win = (cc >= col_start[:, None]) & (cc < col_start[:, None] + NA_KW)
    col_off = cc - cols[:, None] + (NA_KW - 1)
    onehot = np.zeros((2 * NA_KW - 1, GRID_W, GRID_W), np.float32)
    cq, ck = np.nonzero(in_win)
    onehot[col_off[cq, ck], cq, ck] = 1.0
    t = jnp.einsum('lhro,ocd->lhrcd', rpb_all.astype(F32), jnp.asarray(onehot),
                   precision=lax.Precision.HIGHEST)
    t = jnp.where(jnp.asarray(in_win)[None, None, None], t, MASK_VALUE)
    tables = []
    for delta in range(NA_KH):
        tables.append(jnp.concatenate([t[:, :, i - delta + NA_KH - 1] for i in range(NA_KH)], axis=-1))
    return jnp.stack(tables, axis=1)


def _na_body(q_ref, kp_ref, kc_ref, kn_ref, vp_ref, vc_ref, vn_ref, tbl_ref, o_ref, kbuf, vbuf, *, n_blocks):
    j = pl.program_id(1)
    blk = NA_ROWS_PER_BLOCK * GRID_W
    kbuf[0:blk, :] = kp_ref[...]
    kbuf[blk:2 * blk, :] = kc_ref[...]
    kbuf[2 * blk:3 * blk, :] = kn_ref[...]
    vbuf[0:blk, :] = vp_ref[...]
    vbuf[blk:2 * blk, :] = vc_ref[...]
    vbuf[2 * blk:3 * blk, :] = vn_ref[...]
    band = NA_KH * GRID_W
    half = NA_KH // 2
    lane = lax.broadcasted_iota(jnp.int32, (GRID_W, 2 * NA_HEAD_DIM), 1)
    lo = lane < NA_HEAD_DIM
    qscale = NA_HEAD_DIM ** -0.5
    head_mask = (jnp.where(lo, qscale, 0.0).astype(BF16), jnp.where(lo, 0.0, qscale).astype(BF16))

    def row_body(rl, carry):
        start_first = NA_ROWS_PER_BLOCK + jnp.maximum(rl - half, 0)
        start_last = NA_ROWS_PER_BLOCK + jnp.minimum(rl - half, 0)
        start = jnp.where(j == 0, start_first, jnp.where(j == n_blocks - 1, start_last, rl + half))
        delta = jnp.where(j == 0, jnp.minimum(rl, half), jnp.where(j == n_blocks - 1, jnp.maximum(rl, half), half))
        krow = pl.multiple_of(start * GRID_W, GRID_W)
        qrow = pl.multiple_of(rl * GRID_W, GRID_W)
        for p in range(NA_HEADS // 2):
            cs = slice(p * 2 * NA_HEAD_DIM, (p + 1) * 2 * NA_HEAD_DIM)
            q2 = q_ref[pl.ds(qrow, GRID_W), cs]
            k2 = kbuf[pl.ds(krow, band), cs]
            v2 = vbuf[pl.ds(krow, band), cs]
            outs = []
            for hh in range(2):
                qm = q2 * head_mask[hh]
                s = lax.dot_general(qm, k2, (((1,), (1,)), ((), ())), preferred_element_type=F32)
                s = s + tbl_ref[delta, 2 * p + hh]
                m = jnp.max(s, axis=-1, keepdims=True)
                e = jnp.exp(s - m)
                l = jnp.sum(e, axis=-1, keepdims=True)
                o = jnp.dot(e.astype(BF16), v2, preferred_element_type=F32)
                outs.append(o / l)
            o_ref[pl.ds(qrow, GRID_W), cs] = jnp.where(lo, outs[0], outs[1]).astype(o_ref.dtype)
        return carry

    lax.fori_loop(0, NA_ROWS_PER_BLOCK, row_body, 0)


def _neighbourhood_attention(q, k, v, tables, layer, batch):
    n, width = q.shape
    blk = NA_ROWS_PER_BLOCK * GRID_W
    n_blocks = n // batch // blk
    assert n_blocks >= 2

    def tok(off):
        return lambda b, j: (b * n_blocks + jnp.clip(j + off, 0, n_blocks - 1), 0)

    tile = lambda off: pl.BlockSpec((blk, width), tok(off))
    return pl.pallas_call(
        functools.partial(_na_body, n_blocks=n_blocks),
        out_shape=jax.ShapeDtypeStruct((n, width), BF16),
        grid=(batch, n_blocks),
        in_specs=[tile(0), tile(-1), tile(0), tile(1), tile(-1), tile(0), tile(1),
                  pl.BlockSpec((None,) + tables.shape[1:], lambda b, j: (layer, 0, 0, 0, 0))],
        out_specs=tile(0),
        scratch_shapes=[pltpu.VMEM((3 * blk, width), BF16), pltpu.VMEM((3 * blk, width), BF16)],
        compiler_params=_cparams("arbitrary", "arbitrary"),
        name="neigh_attn",
    )(q, k, k, k, v, v, v, tables)


FN_N2 = 128
FN_GROUP_DIM = 64
FN_K1_PER_STEP = 4


def _dft_cos_sin(n):
    ang = 2.0 * np.pi * np.outer(np.arange(n), np.arange(n)) / n
    return np.cos(ang), np.sin(ang)


def _fourier_stage1_body(x_ref, f_ref, tc_ref, ts_ref, zr_ref, zi_ref):
    n1 = x_ref.shape[0]
    z = jnp.dot(f_ref[...], x_ref[...], preferred_element_type=F32)
    zr, zi = z[:n1], z[n1:]
    tc, ts = tc_ref[...], ts_ref[...]
    zr_ref[...] = (zr * tc + zi * ts).astype(zr_ref.dtype)
    zi_ref[...] = (zi * tc - zr * ts).astype(zi_ref.dtype)


def _fourier_stage2_body(zr_ref, zi_ref, f_ref, c_ref, o_ref, *, scale):
    n2 = zr_ref.shape[1]
    width = zr_ref.shape[2]
    for i in range(zr_ref.shape[0]):
        z = jnp.concatenate([zr_ref[i], zi_ref[i]], axis=0)
        y = jnp.dot(f_ref[...], z, preferred_element_type=F32)
        yc = jnp.concatenate([y[:n2], y[n2:]], axis=1).astype(BF16)
        out = jnp.dot(yc, c_ref[...], preferred_element_type=F32) * scale
        o_ref[:, i * width:(i + 1) * width] = out.astype(o_ref.dtype)


def _fourier_mix(u, batch, twiddles):
    n, width = u.shape
    s = n // batch
    n2 = FN_N2
    n1 = s // n2
    tc, ts = twiddles
    c1, s1 = _dft_cos_sin(n1)
    f1 = jnp.asarray(np.concatenate([c1, -s1], axis=0), BF16)
    c2, s2 = _dft_cos_sin(n2)
    f2 = jnp.asarray(np.block([[c2, s2], [-s2, c2]]), BF16)
    cg, sg = _dft_cos_sin(FN_GROUP_DIM)
    eye = np.eye(width // FN_GROUP_DIM)
    fc = jnp.asarray(np.concatenate([np.kron(eye, cg), np.kron(eye, sg)], axis=0), BF16)

    x2 = u.reshape(batch, n1, n2 * width)
    tn = 16 * width
    zr, zi = pl.pallas_call(
        _fourier_stage1_body,
        out_shape=[jax.ShapeDtypeStruct((batch, n1, n2 * width), BF16)] * 2,
        grid=(batch, n2 * width // tn),
        in_specs=[pl.BlockSpec((None, n1, tn), lambda b, j: (b, 0, j)),
                  pl.BlockSpec((2 * n1, n1), lambda b, j: (0, 0)),
                  pl.BlockSpec((n1, tn), lambda b, j: (0, j)),
                  pl.BlockSpec((n1, tn), lambda b, j: (0, j))],
        out_specs=[pl.BlockSpec((None, n1, tn), lambda b, j: (b, 0, j))] * 2,
        compiler_params=_cparams("arbitrary", "arbitrary"),
        name="fourier_stage1",
    )(x2, f1, tc, ts)

    zr4 = zr.reshape(batch, n1, n2, width)
    zi4 = zi.reshape(batch, n1, n2, width)
    nk = FN_K1_PER_STEP
    out = pl.pallas_call(
        functools.partial(_fourier_stage2_body, scale=float(1.0 / np.sqrt(s * FN_GROUP_DIM))),
        out_shape=jax.ShapeDtypeStruct((batch, n2, n1 * width), BF16),
        grid=(batch, n1 // nk),
        in_specs=[pl.BlockSpec((None, nk, n2, width), lambda b, j: (b, j, 0, 0)),
                  pl.BlockSpec((None, nk, n2, width), lambda b, j: (b, j, 0, 0)),
                  pl.BlockSpec((2 * n2, 2 * n2), lambda b, j: (0, 0)),
                  pl.BlockSpec((2 * width, width), lambda b, j: (0, 0))],
        out_specs=pl.BlockSpec((None, n2, nk * width), lambda b, j: (b, 0, j)),
        compiler_params=_cparams("arbitrary", "arbitrary"),
        name="fourier_stage2",
    )(zr4, zi4, f2, fc)
    return out.reshape(n, width)


def _fourier_twiddles(s, width):
    n2 = FN_N2
    n1 = s // n2
    ang = (2.0 * np.pi / s) * (jnp.arange(n1, dtype=F32)[:, None] * jnp.arange(n2, dtype=F32)[None, :])
    tc = jnp.broadcast_to(jnp.cos(ang)[:, :, None], (n1, n2, width)).reshape(n1, n2 * width)
    ts = jnp.broadcast_to(jnp.sin(ang)[:, :, None], (n1, n2, width)).reshape(n1, n2 * width)
    return tc, ts


LRU_LANES = 128
LRU_CHUNK = 512
LRU_HALO = 16
SUBLANES = 8


def _lru_gate_weights(wa_all, wx_all):
    depth, _, nb, db, _ = wa_all.shape
    ncol = nb // 2

    def blockdiag(w):
        w = w.reshape(depth, ncol, 2, db, db)
        z = jnp.zeros_like(w[:, :, 0])
        top = jnp.concatenate([w[:, :, 0], z], axis=-1)
        bot = jnp.concatenate([z, w[:, :, 1]], axis=-1)
        return jnp.concatenate([top, bot], axis=-2)

    parts = [blockdiag(wa_all[:, 0]), blockdiag(wx_all[:, 0]), blockdiag(wa_all[:, 1]), blockdiag(wx_all[:, 1])]
    return jnp.concatenate(parts, axis=-1).astype(BF16)


def _gelu_tanh(x):
    return 0.5 * x * (1.0 + jnp.tanh(np.sqrt(2.0 / np.pi) * (x + 0.044715 * (x * x * x))))


def _lru_body(ux_ref, ug_ref, cw_ref, cb_ref, w_ref, ba_ref, bx_ref, lam_ref, o_ref, hf_ref):
    seq = ux_ref.shape[0]
    tc = LRU_CHUNK
    n_chunks = seq // tc
    lanes = ux_ref.shape[1]
    row = lax.broadcasted_iota(jnp.int32, (tc, lanes), 0)
    sub = row % SUBLANES

    def conv_chunk(t0):
        main = ux_ref[pl.ds(t0, tc), :].astype(F32)
        lo_row = pl.multiple_of(jnp.maximum(t0 - LRU_HALO, 0), LRU_HALO)
        hi_row = pl.multiple_of(jnp.minimum(t0 + tc, seq - LRU_HALO), LRU_HALO)
        lo = ux_ref[pl.ds(lo_row, LRU_HALO), :].astype(F32)
        hi = ux_ref[pl.ds(hi_row, LRU_HALO), :].astype(F32)
        lo = jnp.where(t0 > 0, lo, 0.0)
        hi = jnp.where(t0 + tc < seq, hi, 0.0)
        ext = jnp.concatenate([lo, main, hi], axis=0)
        c = cb_ref[...] + jnp.zeros((tc, lanes), F32)
        for tap in range(CONV_W):
            off = LRU_HALO - CONV_PAD_LEFT + tap
            c = c + ext[off:off + tc] * cw_ref[tap:tap + 1, :]
        return c

    def gates(c, d, t0, first_pos):
        pre = jnp.dot(c.astype(BF16), w_ref[:, 2 * d * lanes:(2 * d + 2) * lanes], preferred_element_type=F32)
        r = _sigmoid(pre[:, :lanes] + ba_ref[d:d + 1, :])
        i = _sigmoid(pre[:, lanes:] + bx_ref[d:d + 1, :])
        lam = lam_ref[d:d + 1, :]
        softplus = jnp.maximum(-lam, 0.0) + jnp.log(1.0 + jnp.exp(-jnp.abs(lam)))
        a = jnp.exp(-LRU_C * r * softplus)
        mult = jnp.sqrt(1.0 - a * a)
        mult = jnp.where(row + t0 == first_pos, 1.0, mult)
        return a, mult * i * c

    def scan_chunk(a, b, carry, reverse):
        for sh in (1, 2, 4):
            if reverse:
                a_sh = pltpu.roll(a, tc - sh, axis=0)
                b_sh = pltpu.roll(b, tc - sh, axis=0)
                valid = sub < SUBLANES - sh
            else:
                a_sh = pltpu.roll(a, sh, axis=0)
                b_sh = pltpu.roll(b, sh, axis=0)
                valid = sub >= sh
            b = jnp.where(valid, a * b_sh + b, b)
            a = jnp.where(valid, a * a_sh, a)
        n_tiles = tc // SUBLANES
        tiles = [None] * n_tiles
        order = range(n_tiles - 1, -1, -1) if reverse else range(n_tiles)
        edge = 0 if reverse else SUBLANES - 1
        for jt in order:
            hj = b[jt * SUBLANES:(jt + 1) * SUBLANES] + a[jt * SUBLANES:(jt + 1) * SUBLANES] * carry
            carry = hj[edge:edge + 1]
            tiles[jt] = hj
        return jnp.concatenate(tiles, axis=0), carry

    def fwd_body(ci, carry):
        t0 = pl.multiple_of(ci * tc, tc)
        c = conv_chunk(t0)
        a, b = gates(c, 0, t0, 0)
        h, carry = scan_chunk(a, b, carry, False)
        hf_ref[pl.ds(t0, tc), :] = h
        return carry

    lax.fori_loop(0, n_chunks, fwd_body, jnp.zeros((1, lanes), F32))

    def bwd_body(ci, carry):
        t0 = pl.multiple_of((n_chunks - 1 - ci) * tc, tc)
        c = conv_chunk(t0)
        a, b = gates(c, 1, t0, seq - 1)
        h, carry = scan_chunk(a, b, carry, True)
        g = _gelu_tanh(ug_ref[pl.ds(t0, tc), :].astype(F32))
        o_ref[pl.ds(t0, tc), :] = ((hf_ref[pl.ds(t0, tc), :] + h) * g).astype(o_ref.dtype)
        return carry

    lax.fori_loop(0, n_chunks, bwd_body, jnp.zeros((1, lanes), F32))


def _recurrent_branch(u_x, u_g, conv_w, conv_b, w_gate, ba, bx, lam, layer, batch):
    n, width = u_x.shape
    s = n // batch
    depth = conv_w.shape[0]
    ncol = width // LRU_LANES
    ux3 = u_x.reshape(batch, s, width)
    ug3 = u_g.reshape(batch, s, width)
    cb3 = conv_b.reshape(depth, 1, width)
    seq_spec = pl.BlockSpec((None, s, LRU_LANES), lambda b, c: (b, 0, c))
    par = lambda rows: pl.BlockSpec((None, rows, LRU_LANES), lambda b, c: (layer, 0, c))
    out = pl.pallas_call(
        _lru_body,
        out_shape=jax.ShapeDtypeStruct((batch, s, width), BF16),
        grid=(batch, ncol),
        in_specs=[seq_spec, seq_spec, par(CONV_W), par(1),
                  pl.BlockSpec((None, None, LRU_LANES, 4 * LRU_LANES), lambda b, c: (layer, c, 0, 0)),
                  par(2), par(2), par(2)],
        out_specs=seq_spec,
        scratch_shapes=[pltpu.VMEM((s, LRU_LANES), F32)],
        compiler_params=_cparams("arbitrary", "arbitrary"),
        name="rg_lru",
    )(ux3, ug3, conv_w, cb3, w_gate, ba, bx, lam)
    return out.reshape(n, width)


def _merge_body(ya_ref, yb_ref, yc_ref, gt_ref, h_ref, wb_ref, wo_ref, o_ref):
    d = h_ref.shape[1]
    merged = None
    for kbr, y_ref in enumerate((ya_ref, yb_ref, yc_ref)):
        ybr = jnp.dot(y_ref[...], wb_ref[kbr], preferred_element_type=F32)
        term = _sigmoid(gt_ref[:, kbr * d:(kbr + 1) * d].astype(F32)) * ybr
        merged = term if merged is None else merged + term
    o_ref[...] = h_ref[...] + jnp.dot(merged.astype(BF16), wo_ref[...], preferred_element_type=F32)


def _merge(ya, yb, yc, gates, h, wb_all, wo_all, layer, tm=1024):
    n, d = h.shape
    bw = ya.shape[1]
    ytile = pl.BlockSpec((tm, bw), lambda t: (t, 0))
    return pl.pallas_call(
        _merge_body,
        out_shape=jax.ShapeDtypeStruct((n, d), F32),
        grid=(n // tm,),
        in_specs=[ytile, ytile, ytile,
                  pl.BlockSpec((tm, 3 * d), lambda t: (t, 0)),
                  pl.BlockSpec((tm, d), lambda t: (t, 0)),
                  pl.BlockSpec((None, 3, bw, d), lambda t: (layer, 0, 0, 0)),
                  pl.BlockSpec((None, d, d), lambda t: (layer, 0, 0))],
        out_specs=pl.BlockSpec((tm, d), lambda t: (t, 0)),
        compiler_params=_cparams("arbitrary"),
        name="branch_merge",
    )(ya, yb, yc, gates, h, wb_all, wo_all)


def _ffn_body(h_ref, g_ref, wg_ref, wu_ref, wd_ref, *rest, ff_blocks_per_expert, use_comb):
    if use_comb:
        comb_ref, o_ref, xn_ref, acc_ref = rest
    else:
        o_ref, xn_ref, acc_ref = rest
    j = pl.program_id(1)

    @pl.when(j == 0)
    def _():
        h = h_ref[...]
        xn_ref[...] = _rms_norm_f32(h, g_ref[...]).astype(BF16)
        acc_ref[...] = h

    xn = xn_ref[...]
    gate = jnp.dot(xn, wg_ref[...].astype(BF16), preferred_element_type=F32)
    up = jnp.dot(xn, wu_ref[...].astype(BF16), preferred_element_type=F32)
    act = gate * _sigmoid(gate) * up
    if use_comb:
        comb = comb_ref[...]
        e = j // ff_blocks_per_expert
        lane = lax.broadcasted_iota(jnp.int32, comb.shape, 1)
        act = act * jnp.sum(jnp.where(lane == e, comb, 0.0), axis=-1, keepdims=True)
    acc_ref[...] += jnp.dot(act.astype(BF16), wd_ref[...].astype(BF16), preferred_element_type=F32)

    @pl.when(j == pl.num_programs(1) - 1)
    def _():
        o_ref[...] = acc_ref[...]


def _ffn(h, g_all, layer, w_gu_all, w_down_all, widx, comb=None, tm=1024, tf=512):
    n, d = h.shape
    depth = g_all.shape[0]
    g3 = g_all.reshape(depth, 1, d)
    experts = comb is not None
    f = w_down_all.shape[-2]
    nf = f // tf
    if experts:
        ne = w_gu_all.shape[1]
        steps = ne * nf
        wg_spec = pl.BlockSpec((None, None, d, tf), lambda t, j: (widx, j // nf, 0, j % nf))
        wu_spec = pl.BlockSpec((None, None, d, tf), lambda t, j: (widx, j // nf, 0, j % nf + nf))
        wd_spec = pl.BlockSpec((None, None, tf, d), lambda t, j: (widx, j // nf, j % nf, 0))
    else:
        steps = nf
        wg_spec = pl.BlockSpec((None, d, tf), lambda t, j: (widx, 0, j))
        wu_spec = pl.BlockSpec((None, d, tf), lambda t, j: (widx, 0, j + nf))
        wd_spec = pl.BlockSpec((None, tf, d), lambda t, j: (widx, j, 0))
    in_specs = [pl.BlockSpec((tm, d), lambda t, j: (t, 0)),
                pl.BlockSpec((None, 1, d), lambda t, j: (layer, 0, 0)),
                wg_spec, wu_spec, wd_spec]
    args = [h, g3, w_gu_all, w_gu_all, w_down_all]
    if experts:
        in_specs.append(pl.BlockSpec((tm, comb.shape[1]), lambda t, j: (t, 0)))
        args.append(comb)
    return pl.pallas_call(
        functools.partial(_ffn_body, ff_blocks_per_expert=nf, use_comb=experts),
        out_shape=jax.ShapeDtypeStruct((n, d), F32),
        grid=(n // tm, steps),
        in_specs=in_specs,
        out_specs=pl.BlockSpec((tm, d), lambda t, j: (t, 0)),
        scratch_shapes=[pltpu.VMEM((tm, d), BF16), pltpu.VMEM((tm, d), F32)],
        compiler_params=_cparams("arbitrary", "arbitrary"),
        name="moe_ffn" if experts else "dense_ffn",
    )(*args)


def _router_body(h_ref, g_ref, wr_ref, comb_ref):
    xn = _rms_norm_f32(h_ref[...], g_ref[...])
    logits = jnp.dot(xn, wr_ref[...], preferred_element_type=F32, precision=lax.Precision.HIGHEST)
    ne = logits.shape[1]
    lane = lax.broadcasted_iota(jnp.int32, logits.shape, 1)
    m1 = jnp.max(logits, axis=-1, keepdims=True)
    i1 = jnp.min(jnp.where(logits == m1, lane, ne), axis=-1, keepdims=True)
    rest = jnp.where(lane == i1, -jnp.inf, logits)
    m2 = jnp.max(rest, axis=-1, keepdims=True)
    i2 = jnp.min(jnp.where(rest == m2, lane, ne), axis=-1, keepdims=True)
    e = jnp.exp(m2 - m1)
    w1 = 1.0 / (1.0 + e)
    w2 = e / (1.0 + e)
    comb_ref[...] = jnp.where(lane == i1, w1, 0.0) + jnp.where(lane == i2, w2, 0.0)


def _router(h, g_all, layer, wr_all, widx, tm=1024):
    n, d = h.shape
    depth = g_all.shape[0]
    ne = wr_all.shape[-1]
    return pl.pallas_call(
        _router_body,
        out_shape=jax.ShapeDtypeStruct((n, ne), F32),
        grid=(n // tm,),
        in_specs=[pl.BlockSpec((tm, d), lambda t: (t, 0)),
                  pl.BlockSpec((None, 1, d), lambda t: (layer, 0, 0)),
                  pl.BlockSpec((None, d, ne), lambda t: (widx, 0, 0))],
        out_specs=pl.BlockSpec((tm, ne), lambda t: (t, 0)),
        compiler_params=_cparams("arbitrary"),
        name="router",
    )(h, g_all.reshape(depth, 1, d), wr_all)


def _moe(h, g_all, layer, wr_all, w_gu_all, w_down_all, widx):
    comb = _router(h, g_all, layer, wr_all, widx)
    return _ffn(h, g_all, layer, w_gu_all, w_down_all, widx, comb=comb)


def _final_norm_body(h_ref, g_ref, o_ref):
    o_ref[...] = _rms_norm_f32(h_ref[...], g_ref[...])


def _final_norm(h, g, tm=1024):
    n, d = h.shape
    return pl.pallas_call(
        _final_norm_body,
        out_shape=jax.ShapeDtypeStruct((n, d), F32),
        grid=(n // tm,),
        in_specs=[pl.BlockSpec((tm, d), lambda t: (t, 0)), pl.BlockSpec((1, d), lambda t: (0, 0))],
        out_specs=pl.BlockSpec((tm, d), lambda t: (t, 0)),
        compiler_params=_cparams("arbitrary"),
        name="final_norm",
    )(h, g.reshape(1, d))


def kernel(x, norm_mix_g, w_in, b_in, na_rpb, conv_w, conv_b, lru_wa, lru_ba, lru_wx, lru_bx, lru_lambda, w_branch, w_out, norm_ffn_g, ffn_w_gu, ffn_w_down, router_w, moe_w_gu, moe_w_down, final_g):
    batch, seq, d = x.shape
    depth = w_in.shape[0]
    n = batch * seq
    bw = w_branch.shape[2]
    h = x.reshape(n, d)

    tables = _na_bias_tables(na_rpb)
    twiddles = _fourier_twiddles(seq, bw)
    w_gate = _lru_gate_weights(lru_wa, lru_wx)
    wb16 = w_branch.astype(BF16)
    wo16 = w_out.astype(BF16)
    tn = 3 * bw

    for l in range(depth):
        q, k, v = _norm_proj(h, norm_mix_g, w_in, b_in, l, 0, 1, tn, 3)
        u_f, u_x, u_g = _norm_proj(h, norm_mix_g, w_in, b_in, l, 1, 1, tn, 3)
        (gates,) = _norm_proj(h, norm_mix_g, w_in, b_in, l, 2, (3 * d) // tn, tn, 1)
        y_a = _neighbourhood_attention(q, k, v, tables, l, batch)
        y_b = _fourier_mix(u_f, batch, twiddles)
        y_c = _recurrent_branch(u_x, u_g, conv_w, conv_b, w_gate, lru_ba, lru_bx, lru_lambda, l, batch)
        h = _merge(y_a, y_b, y_c, gates, h, wb16, wo16, l)
        if l % 2 == 0:
            h = _ffn(h, norm_ffn_g, l, ffn_w_gu, ffn_w_down, l // 2)
        else:
            h = _moe(h, norm_ffn_g, l, router_w, moe_w_gu, moe_w_down, l // 2)
    return _final_norm(h, final_g).reshape(batch, seq, d)
```

```python
import functools

import numpy as np
import jax
import jax.numpy as jnp
from jax import lax
from jax.experimental import pallas as pl
from jax.experimental.pallas import tpu as pltpu

F32 = jnp.float32
BF16 = jnp.bfloat16

RMS_EPS = 1e-6
GRID_W = 64
NA_HEADS = 8
NA_HEAD_DIM = 64
NA_KH = 8
NA_KW = 16
NA_ROWS_PER_BLOCK = 8
NA_ROWS_IN_FLIGHT = 4
LRU_C = 8.0
CONV_W = 4
CONV_PAD_LEFT = 2
N_EXPERTS = 8
MASK_VALUE = -1e30

VMEM_LIMIT_BYTES = 56 * 1024 * 1024


def _cparams(*sem):
    return pltpu.CompilerParams(dimension_semantics=sem, vmem_limit_bytes=VMEM_LIMIT_BYTES)


def _rms_norm_f32(x, g):
    ms = jnp.mean(x * x, axis=-1, keepdims=True)
    return x * lax.rsqrt(ms + RMS_EPS) * g


def _sigmoid(x):
    return 1.0 / (1.0 + jnp.exp(-x))


def _norm_proj_body(x_ref, g_ref, w_ref, b_ref, *out_refs):
    xn = _rms_norm_f32(x_ref[...], g_ref[...]).astype(BF16)
    r = jnp.dot(xn, w_ref[...].astype(BF16), preferred_element_type=F32) + b_ref[...]
    width = r.shape[1] // len(out_refs)
    for i, o in enumerate(out_refs):
        o[...] = r[:, i * width:(i + 1) * width].astype(o.dtype)


def _norm_proj(h, g_all, w_all, b_all, layer, col_block0, n_col_blocks, tn, n_out, tm=1024):
    n, d = h.shape
    depth = w_all.shape[0]
    g3 = g_all.reshape(depth, 1, d)
    b3 = b_all.reshape(depth, 1, -1)
    width = tn // n_out
    out_shape = [jax.ShapeDtypeStruct((n, n_col_blocks * width), BF16) for _ in range(n_out)]
    return pl.pallas_call(
        _norm_proj_body,
        out_shape=out_shape,
        grid=(n_col_blocks, n // tm),
        in_specs=[
            pl.BlockSpec((tm, d), lambda c, t: (t, 0)),
            pl.BlockSpec((None, 1, d), lambda c, t: (layer, 0, 0)),
            pl.BlockSpec((None, d, tn), lambda c, t: (layer, 0, col_block0 + c)),
            pl.BlockSpec((None, 1, tn), lambda c, t: (layer, 0, col_block0 + c)),
        ],
        out_specs=[pl.BlockSpec((tm, width), lambda c, t: (t, c)) for _ in range(n_out)],
        compiler_params=_cparams("arbitrary", "arbitrary"),
        name="norm_proj",
    )(h, g3, w_all, b3)


def _na_bias_tables(rpb_all):
    cols = np.arange(GRID_W)
    col_start = np.clip(cols - NA_KW // 2, 0, GRID_W - NA_KW)
    cc = np.arange(GRID_W)[None, :]
    in_win = (cc >= col_start[:, None]) & (cc < col_start[:, None] + NA_KW)
    col_off = cc - cols[:, None] + (NA_KW - 1)
    onehot = np.zeros((2 * NA_KW - 1, GRID_W, GRID_W), np.float32)
    cq, ck = np.nonzero(in_win)
    onehot[col_off[cq, ck], cq, ck] = 1.0
    t = jnp.einsum('lhro,ocd->lhrcd', rpb_all.astype(F32), jnp.asarray(onehot),
                   precision=lax.Precision.HIGHEST)
    t = jnp.where(jnp.asarray(in_win)[None, None, None], t, MASK_VALUE)
    tables = []
    for delta in range(NA_KH):
        tables.append(jnp.concatenate([t[:, :, i - delta + NA_KH - 1] for i in range(NA_KH)], axis=-1))
    return jnp.stack(tables, axis=1)


def _na_body(q_ref, kp_ref, kc_ref, kn_ref, vp_ref, vc_ref, vn_ref, tbl_ref, o_ref, kbuf, vbuf, s_scr, e_scr, *, n_blocks):
    j = pl.program_id(1)
    blk = NA_ROWS_PER_BLOCK * GRID_W
    kbuf[0:blk, :] = kp_ref[...]
    kbuf[blk:2 * blk, :] = kc_ref[...]
    kbuf[2 * blk:3 * blk, :] = kn_ref[...]
    vbuf[0:blk, :] = vp_ref[...]
    vbuf[blk:2 * blk, :] = vc_ref[...]
    vbuf[2 * blk:3 * blk, :] = vn_ref[...]
    band = NA_KH * GRID_W
    half = NA_KH // 2
    lane = lax.broadcasted_iota(jnp.int32, (GRID_W, 2 * NA_HEAD_DIM), 1)
    lo = lane < NA_HEAD_DIM
    qscale = NA_HEAD_DIM ** -0.5
    head_mask = (jnp.where(lo, qscale, 0.0).astype(BF16), jnp.where(lo, 0.0, qscale).astype(BF16))

    n_pairs = NA_HEADS // 2
    cols = [slice(p * 2 * NA_HEAD_DIM, (p + 1) * 2 * NA_HEAD_DIM) for p in range(n_pairs)]

    def row_offsets(rl):
        start_first = NA_ROWS_PER_BLOCK + jnp.maximum(rl - half, 0)
        start_last = NA_ROWS_PER_BLOCK + jnp.minimum(rl - half, 0)
        start = jnp.where(j == 0, start_first, jnp.where(j == n_blocks - 1, start_last, rl + half))
        delta = jnp.where(j == 0, jnp.minimum(rl, half), jnp.where(j == n_blocks - 1, jnp.maximum(rl, half), half))
        return pl.multiple_of(rl * GRID_W, GRID_W), pl.multiple_of(start * GRID_W, GRID_W), delta

    def scores(rl, slot):
        qrow, krow, delta = row_offsets(rl)
        for p in range(n_pairs):
            q2 = q_ref[pl.ds(qrow, GRID_W), cols[p]]
            qs = jnp.concatenate([q2 * head_mask[0], q2 * head_mask[1]], axis=0)
            k2 = kbuf[pl.ds(krow, band), cols[p]]
            s = lax.dot_general(qs, k2, (((1,), (1,)), ((), ())), preferred_element_type=F32)
            s_scr[slot, p] = s + tbl_ref[delta, p]

    def softmax(slot):
        inv_l = []
        for p in range(n_pairs):
            s = s_scr[slot, p]
            m = jnp.max(s, axis=-1, keepdims=True)
            e = jnp.exp(s - m)
            inv_l.append(1.0 / jnp.sum(e, axis=-1, keepdims=True))
            e_scr[slot, p] = e.astype(BF16)
        return inv_l

    def weighted_values(rl, slot, inv_l):
        qrow, krow, _ = row_offsets(rl)
        for p in range(n_pairs):
            v2 = vbuf[pl.ds(krow, band), cols[p]]
            o = jnp.dot(e_scr[slot, p], v2, preferred_element_type=F32) * inv_l[p]
            o_ref[pl.ds(qrow, GRID_W), cols[p]] = jnp.where(lo, o[:GRID_W], o[GRID_W:]).astype(o_ref.dtype)

    def rows_body(it, carry):
        rows = [it * NA_ROWS_IN_FLIGHT + r for r in range(NA_ROWS_IN_FLIGHT)]
        for slot, rl in enumerate(rows):
            scores(rl, slot)
        inv = [softmax(slot) for slot in range(NA_ROWS_IN_FLIGHT)]
        for slot, rl in enumerate(rows):
            weighted_values(rl, slot, inv[slot])
        return carry

    lax.fori_loop(0, NA_ROWS_PER_BLOCK // NA_ROWS_IN_FLIGHT, rows_body, 0)


def _neighbourhood_attention(q, k, v, tables, layer, batch):
    n, width = q.shape
    n_pairs = NA_HEADS // 2
    depth = tables.shape[0]
    tables = tables.reshape(depth, NA_KH, n_pairs, 2 * GRID_W, NA_KH * GRID_W)
    blk = NA_ROWS_PER_BLOCK * GRID_W
    n_blocks = n // batch // blk
    assert n_blocks >= 2

    def tok(off):
        return lambda b, j: (b * n_blocks + jnp.clip(j + off, 0, n_blocks - 1), 0)

    tile = lambda off: pl.BlockSpec((blk, width), tok(off))
    return pl.pallas_call(
        functools.partial(_na_body, n_blocks=n_blocks),
        out_shape=jax.ShapeDtypeStruct((n, width), BF16),
        grid=(batch, n_blocks),
        in_specs=[tile(0), tile(-1), tile(0), tile(1), tile(-1), tile(0), tile(1),
                  pl.BlockSpec((None,) + tables.shape[1:], lambda b, j: (layer, 0, 0, 0, 0))],
        out_specs=tile(0),
        scratch_shapes=[pltpu.VMEM((3 * blk, width), BF16), pltpu.VMEM((3 * blk, width), BF16),
                        pltpu.VMEM((NA_ROWS_IN_FLIGHT, n_pairs, 2 * GRID_W, NA_KH * GRID_W), F32),
                        pltpu.VMEM((NA_ROWS_IN_FLIGHT, n_pairs, 2 * GRID_W, NA_KH * GRID_W), BF16)],
        compiler_params=_cparams("arbitrary", "arbitrary"),
        name="neigh_attn",
    )(q, k, k, k, v, v, v, tables)


FN_N2 = 128
FN_GROUP_DIM = 64
FN_K1_PER_STEP = 4


def _dft_cos_sin(n):
    ang = 2.0 * np.pi * (np.outer(np.arange(n), np.arange(n)) % n) / n
    return np.cos(ang), np.sin(ang)


def _fourier_stage1_body(x_ref, f_ref, tc_ref, ts_ref, zr_ref, zi_ref):
    n1 = x_ref.shape[0]
    z = jnp.dot(f_ref[...], x_ref[...], preferred_element_type=F32)
    zr, zi = z[:n1], z[n1:]
    tc, ts = tc_ref[...], ts_ref[...]
    zr_ref[...] = (zr * tc + zi * ts).astype(zr_ref.dtype)
    zi_ref[...] = (zi * tc - zr * ts).astype(zi_ref.dtype)


def _fourier_stage2_body(zr_ref, zi_ref, f_ref, c_ref, o_ref, *, scale):
    n2 = zr_ref.shape[1]
    width = zr_ref.shape[2]
    for i in range(zr_ref.shape[0]):
        z = jnp.concatenate([zr_ref[i], zi_ref[i]], axis=0)
        y = jnp.dot(f_ref[...], z, preferred_element_type=F32)
        yc = jnp.concatenate([y[:n2], y[n2:]], axis=1).astype(BF16)
        out = jnp.dot(yc, c_ref[...], preferred_element_type=F32) * scale
        o_ref[:, i * width:(i + 1) * width] = out.astype(o_ref.dtype)


def _fourier_mix(u, batch, twiddles):
    n, width = u.shape
    s = n // batch
    n2 = FN_N2
    n1 = s // n2
    tc, ts = twiddles
    c1, s1 = _dft_cos_sin(n1)
    f1 = jnp.asarray(np.concatenate([c1, -s1], axis=0), F32).astype(BF16)
    c2, s2 = _dft_cos_sin(n2)
    f2 = jnp.asarray(np.block([[c2, s2], [-s2, c2]]), F32).astype(BF16)
    cg, sg = _dft_cos_sin(FN_GROUP_DIM)
    eye = np.eye(width // FN_GROUP_DIM)
    fc = jnp.asarray(np.concatenate([np.kron(eye, cg), np.kron(eye, sg)], axis=0), F32).astype(BF16)

    x2 = u.reshape(batch, n1, n2 * width)
    tn = 16 * width
    zr, zi = pl.pallas_call(
        _fourier_stage1_body,
        out_shape=[jax.ShapeDtypeStruct((batch, n1, n2 * width), BF16)] * 2,
        grid=(batch, n2 * width // tn),
        in_specs=[pl.BlockSpec((None, n1, tn), lambda b, j: (b, 0, j)),
                  pl.BlockSpec((2 * n1, n1), lambda b, j: (0, 0)),
                  pl.BlockSpec((n1, tn), lambda b, j: (0, j)),
                  pl.BlockSpec((n1, tn), lambda b, j: (0, j))],
        out_specs=[pl.BlockSpec((None, n1, tn), lambda b, j: (b, 0, j))] * 2,
        compiler_params=_cparams("arbitrary", "arbitrary"),
        name="fourier_stage1",
    )(x2, f1, tc, ts)

    zr4 = zr.reshape(batch, n1, n2, width)
    zi4 = zi.reshape(batch, n1, n2, width)
    nk = FN_K1_PER_STEP
    out = pl.pallas_call(
        functools.partial(_fourier_stage2_body, scale=float(1.0 / np.sqrt(s * FN_GROUP_DIM))),
        out_shape=jax.ShapeDtypeStruct((batch, n2, n1 * width), BF16),
        grid=(batch, n1 // nk),
        in_specs=[pl.BlockSpec((None, nk, n2, width), lambda b, j: (b, j, 0, 0)),
                  pl.BlockSpec((None, nk, n2, width), lambda b, j: (b, j, 0, 0)),
                  pl.BlockSpec((2 * n2, 2 * n2), lambda b, j: (0, 0)),
                  pl.BlockSpec((2 * width, width), lambda b, j: (0, 0))],
        out_specs=pl.BlockSpec((None, n2, nk * width), lambda b, j: (b, 0, j)),
        compiler_params=_cparams("arbitrary", "arbitrary"),
        name="fourier_stage2",
    )(zr4, zi4, f2, fc)
    return out.reshape(n, width)


def _fourier_twiddles(s, width):
    n2 = FN_N2
    n1 = s // n2
    ang = (2.0 * np.pi / s) * (jnp.arange(n1, dtype=F32)[:, None] * jnp.arange(n2, dtype=F32)[None, :])
    tc = jnp.broadcast_to(jnp.cos(ang)[:, :, None], (n1, n2, width)).reshape(n1, n2 * width)
    ts = jnp.broadcast_to(jnp.sin(ang)[:, :, None], (n1, n2, width)).reshape(n1, n2 * width)
    return tc, ts


LRU_LANES = 128
LRU_CHUNK = 512
LRU_HALO = 16
SUBLANES = 8


def _lru_gate_weights(wa_all, wx_all):
    depth, _, nb, db, _ = wa_all.shape
    ncol = nb // 2

    def blockdiag(w):
        w = w.reshape(depth, ncol, 2, db, db)
        z = jnp.zeros_like(w[:, :, 0])
        top = jnp.concatenate([w[:, :, 0], z], axis=-1)
        bot = jnp.concatenate([z, w[:, :, 1]], axis=-1)
        return jnp.concatenate([top, bot], axis=-2)

    parts = [blockdiag(wa_all[:, 0]), blockdiag(wx_all[:, 0]), blockdiag(wa_all[:, 1]), blockdiag(wx_all[:, 1])]
    return jnp.concatenate(parts, axis=-1).astype(BF16)


def _gelu_tanh(x):
    return 0.5 * x * (1.0 + jnp.tanh(np.sqrt(2.0 / np.pi) * (x + 0.044715 * (x * x * x))))


def _lru_body(ux_ref, ug_ref, cw_ref, cb_ref, w_ref, ba_ref, bx_ref, lam_ref, o_ref, hf_ref):
    seq = ux_ref.shape[0]
    tc = LRU_CHUNK
    n_chunks = seq // tc
    lanes = ux_ref.shape[1]
    row = lax.broadcasted_iota(jnp.int32, (tc, lanes), 0)
    sub = row % SUBLANES

    def conv_chunk(t0):
        main = ux_ref[pl.ds(t0, tc), :].astype(F32)
        lo_row = pl.multiple_of(jnp.maximum(t0 - LRU_HALO, 0), LRU_HALO)
        hi_row = pl.multiple_of(jnp.minimum(t0 + tc, seq - LRU_HALO), LRU_HALO)
        lo = ux_ref[pl.ds(lo_row, LRU_HALO), :].astype(F32)
        hi = ux_ref[pl.ds(hi_row, LRU_HALO), :].astype(F32)
        lo = jnp.where(t0 > 0, lo, 0.0)
        hi = jnp.where(t0 + tc < seq, hi, 0.0)
        ext = jnp.concatenate([lo, main, hi], axis=0)
        c = cb_ref[...] + jnp.zeros((tc, lanes), F32)
        for tap in range(CONV_W):
            off = LRU_HALO - CONV_PAD_LEFT + tap
            c = c + ext[off:off + tc] * cw_ref[tap:tap + 1, :]
        return c

    def gates(c, d, t0, first_pos):
        pre = jnp.dot(c.astype(BF16), w_ref[:, 2 * d * lanes:(2 * d + 2) * lanes], preferred_element_type=F32)
        r = _sigmoid(pre[:, :lanes] + ba_ref[d:d + 1, :])
        i = _sigmoid(pre[:, lanes:] + bx_ref[d:d + 1, :])
        lam = lam_ref[d:d + 1, :]
        softplus = jnp.maximum(-lam, 0.0) + jnp.log(1.0 + jnp.exp(-jnp.abs(lam)))
        a = jnp.exp(-LRU_C * r * softplus)
        om = 1.0 - a * a
        mult = jnp.where(om > 0.0, om * lax.rsqrt(om), 0.0)
        mult = jnp.where(row + t0 == first_pos, 1.0, mult)
        return a, mult * i * c

    def scan_chunk(a, b, carry, reverse):
        for sh in (1, 2, 4):
            if reverse:
                a_sh = pltpu.roll(a, tc - sh, axis=0)
                b_sh = pltpu.roll(b, tc - sh, axis=0)
                valid = sub < SUBLANES - sh
            else:
                a_sh = pltpu.roll(a, sh, axis=0)
                b_sh = pltpu.roll(b, sh, axis=0)
                valid = sub >= sh
            b = jnp.where(valid, a * b_sh + b, b)
            a = jnp.where(valid, a * a_sh, a)
        n_tiles = tc // SUBLANES
        tiles = [None] * n_tiles
        order = range(n_tiles - 1, -1, -1) if reverse else range(n_tiles)
        edge = 0 if reverse else SUBLANES - 1
        for jt in order:
            hj = b[jt * SUBLANES:(jt + 1) * SUBLANES] + a[jt * SUBLANES:(jt + 1) * SUBLANES] * carry
            carry = hj[edge:edge + 1]
            tiles[jt] = hj
        return jnp.concatenate(tiles, axis=0), carry

    def fwd_body(ci, carry):
        t0 = pl.multiple_of(ci * tc, tc)
        c = conv_chunk(t0)
        a, b = gates(c, 0, t0, 0)
        h, carry = scan_chunk(a, b, carry, False)
        hf_ref[pl.ds(t0, tc), :] = h
        return carry

    lax.fori_loop(0, n_chunks, fwd_body, jnp.zeros((1, lanes), F32))

    def bwd_body(ci, carry):
        t0 = pl.multiple_of((n_chunks - 1 - ci) * tc, tc)
        c = conv_chunk(t0)
        a, b = gates(c, 1, t0, seq - 1)
        h, carry = scan_chunk(a, b, carry, True)
        g = _gelu_tanh(ug_ref[pl.ds(t0, tc), :].astype(F32))
        o_ref[pl.ds(t0, tc), :] = ((hf_ref[pl.ds(t0, tc), :] + h) * g).astype(o_ref.dtype)
        return carry

    lax.fori_loop(0, n_chunks, bwd_body, jnp.zeros((1, lanes), F32))


def _recurrent_branch(u_x, u_g, conv_w, conv_b, w_gate, ba, bx, lam, layer, batch):
    n, width = u_x.shape
    s = n // batch
    depth = conv_w.shape[0]
    ncol = width // LRU_LANES
    ux3 = u_x.reshape(batch, s, width)
    ug3 = u_g.reshape(batch, s, width)
    cb3 = conv_b.reshape(depth, 1, width)
    seq_spec = pl.BlockSpec((None, s, LRU_LANES), lambda b, c: (b, 0, c))
    par = lambda rows: pl.BlockSpec((None, rows, LRU_LANES), lambda b, c: (layer, 0, c))
    out = pl.pallas_call(
        _lru_body,
        out_shape=jax.ShapeDtypeStruct((batch, s, width), BF16),
        grid=(batch, ncol),
        in_specs=[seq_spec, seq_spec, par(CONV_W), par(1),
                  pl.BlockSpec((None, None, LRU_LANES, 4 * LRU_LANES), lambda b, c: (layer, c, 0, 0)),
                  par(2), par(2), par(2)],
        out_specs=seq_spec,
        scratch_shapes=[pltpu.VMEM((s, LRU_LANES), F32)],
        compiler_params=_cparams("arbitrary", "arbitrary"),
        name="rg_lru",
    )(ux3, ug3, conv_w, cb3, w_gate, ba, bx, lam)
    return out.reshape(n, width)


def _merge_body(ya_ref, yb_ref, yc_ref, gt_ref, h_ref, wb_ref, wo_ref, o_ref):
    d = h_ref.shape[1]
    merged = None
    for kbr, y_ref in enumerate((ya_ref, yb_ref, yc_ref)):
        ybr = jnp.dot(y_ref[...], wb_ref[kbr], preferred_element_type=F32)
        term = _sigmoid(gt_ref[:, kbr * d:(kbr + 1) * d].astype(F32)) * ybr
        merged = term if merged is None else merged + term
    o_ref[...] = h_ref[...] + jnp.dot(merged.astype(BF16), wo_ref[...], preferred_element_type=F32)


def _merge(ya, yb, yc, gates, h, wb_all, wo_all, layer, tm=1024):
    n, d = h.shape
    bw = ya.shape[1]
    ytile = pl.BlockSpec((tm, bw), lambda t: (t, 0))
    return pl.pallas_call(
        _merge_body,
        out_shape=jax.ShapeDtypeStruct((n, d), F32),
        grid=(n // tm,),
        in_specs=[ytile, ytile, ytile,
                  pl.BlockSpec((tm, 3 * d), lambda t: (t, 0)),
                  pl.BlockSpec((tm, d), lambda t: (t, 0)),
                  pl.BlockSpec((None, 3, bw, d), lambda t: (layer, 0, 0, 0)),
                  pl.BlockSpec((None, d, d), lambda t: (layer, 0, 0))],
        out_specs=pl.BlockSpec((tm, d), lambda t: (t, 0)),
        compiler_params=_cparams("arbitrary"),
        name="branch_merge",
    )(ya, yb, yc, gates, h, wb_all, wo_all)


def _ffn_body(h_ref, g_ref, wg_ref, wu_ref, wd_ref, o_ref, xn_ref, acc_ref):
    j = pl.program_id(1)

    @pl.when(j == 0)
    def _():
        h = h_ref[...]
        xn_ref[...] = _rms_norm_f32(h, g_ref[...]).astype(BF16)
        acc_ref[...] = h

    xn = xn_ref[...]
    gate = jnp.dot(xn, wg_ref[...].astype(BF16), preferred_element_type=F32)
    up = jnp.dot(xn, wu_ref[...].astype(BF16), preferred_element_type=F32)
    act = gate * _sigmoid(gate) * up
    acc_ref[...] += jnp.dot(act.astype(BF16), wd_ref[...].astype(BF16), preferred_element_type=F32)

    @pl.when(j == pl.num_programs(1) - 1)
    def _():
        o_ref[...] = acc_ref[...]


def _ffn(h, g_all, layer, w_gu_all, w_down_all, widx, tm=1024, tf=512):
    n, d = h.shape
    depth = g_all.shape[0]
    nf = w_down_all.shape[-2] // tf
    return pl.pallas_call(
        _ffn_body,
        out_shape=jax.ShapeDtypeStruct((n, d), F32),
        grid=(n // tm, nf),
        in_specs=[pl.BlockSpec((tm, d), lambda t, j: (t, 0)),
                  pl.BlockSpec((None, 1, d), lambda t, j: (layer, 0, 0)),
                  pl.BlockSpec((None, d, tf), lambda t, j: (widx, 0, j)),
                  pl.BlockSpec((None, d, tf), lambda t, j: (widx, 0, j + nf)),
                  pl.BlockSpec((None, tf, d), lambda t, j: (widx, j, 0))],
        out_specs=pl.BlockSpec((tm, d), lambda t, j: (t, 0)),
        scratch_shapes=[pltpu.VMEM((tm, d), BF16), pltpu.VMEM((tm, d), F32)],
        compiler_params=_cparams("arbitrary", "arbitrary"),
        name="dense_ffn",
    )(h, g_all.reshape(depth, 1, d), w_gu_all, w_gu_all, w_down_all)


MOE_TOKEN_TILE = 512
MOE_ROW_TILE = 512
MOE_SEG_ALIGN = 16
MOE_TOP_K = 2
MOE_CHUNK_SIZES = tuple(MOE_SEG_ALIGN << b for b in range(5, -1, -1))
assert MOE_CHUNK_SIZES[0] == MOE_TOKEN_TILE


def _moe_compact_rows(ne):
    rows = MOE_TOKEN_TILE * MOE_TOP_K + ne * (MOE_SEG_ALIGN - 1)
    return -(-rows // MOE_SEG_ALIGN) * MOE_SEG_ALIGN


def _moe_sorted_rows(n, ne):
    rows = n * MOE_TOP_K + (n // MOE_TOKEN_TILE) * ne * (MOE_SEG_ALIGN - 1) + ne * (MOE_ROW_TILE - MOE_SEG_ALIGN)
    return -(-rows // MOE_ROW_TILE) * MOE_ROW_TILE


def _router_body(h_ref, g_ref, wrt_ref, xn_ref, posr_ref, wrow_ref, posc_ref, cnt_ref):
    t_tokens = h_ref.shape[0]
    xn = _rms_norm_f32(h_ref[...], g_ref[...])
    xn_ref[...] = xn.astype(BF16)
    nt_dims = (((1,), (1,)), ((), ()))
    logits = lax.dot_general(wrt_ref[...], xn, nt_dims, preferred_element_type=F32,
                             precision=lax.Precision.HIGHEST)
    ne = logits.shape[0]
    sub = lax.broadcasted_iota(jnp.int32, logits.shape, 0)
    m1 = jnp.max(logits, axis=0, keepdims=True)
    i1 = jnp.min(jnp.where(logits == m1, sub, ne), axis=0, keepdims=True)
    rest = jnp.where(sub == i1, -jnp.inf, logits)
    m2 = jnp.max(rest, axis=0, keepdims=True)
    i2 = jnp.min(jnp.where(rest == m2, sub, ne), axis=0, keepdims=True)
    e = jnp.exp(m2 - m1)
    wrow_ref[...] = jnp.concatenate([1.0 / (1.0 + e), e / (1.0 + e)], axis=0)

    sel1, sel2 = sub == i1, sub == i2
    memb = jnp.where(sel1, 1.0, jnp.where(sel2, 1.0, 0.0))
    before = lax.broadcasted_iota(jnp.int32, (t_tokens, t_tokens), 0) < lax.broadcasted_iota(jnp.int32, (t_tokens, t_tokens), 1)
    rank = jnp.dot(memb.astype(BF16), jnp.where(before, 1.0, 0.0).astype(BF16), preferred_element_type=F32)
    sub_c = lax.broadcasted_iota(jnp.int32, cnt_ref.shape, 0)
    cnt_out = jnp.zeros(cnt_ref.shape, F32)
    base = rank
    running = jnp.zeros((1, 1), F32)
    for ex in range(ne):
        c = jnp.sum(memb[ex:ex + 1, :], axis=1, keepdims=True)
        cnt_out = jnp.where(sub_c == ex, c, cnt_out)
        base = jnp.where(sub == ex, base + running, base)
        running = running + jnp.floor((c + (MOE_SEG_ALIGN - 1)) * (1.0 / MOE_SEG_ALIGN)) * MOE_SEG_ALIGN
    cnt_ref[...] = cnt_out.astype(jnp.int32)
    pos1 = jnp.sum(jnp.where(sel1, base, 0.0), axis=0, keepdims=True)
    pos2 = jnp.sum(jnp.where(sel2, base, 0.0), axis=0, keepdims=True)
    posr_ref[...] = jnp.concatenate([pos1, pos2], axis=0).astype(jnp.int32)
    pos8 = jnp.concatenate([pos1, pos2, jnp.zeros((posc_ref.shape[1] - MOE_TOP_K, t_tokens), F32)], axis=0)
    eye = lax.broadcasted_iota(jnp.int32, (t_tokens, t_tokens), 0) == lax.broadcasted_iota(jnp.int32, (t_tokens, t_tokens), 1)
    posc = lax.dot_general(jnp.where(eye, 1.0, 0.0), pos8, nt_dims, preferred_element_type=F32,
                           precision=lax.Precision.HIGHEST)
    posc_ref[...] = posc.astype(jnp.int32)


def _router(h, g_all, layer, wr_all, widx):
    n, d = h.shape
    depth = g_all.shape[0]
    ne = wr_all.shape[-1]
    tm = MOE_TOKEN_TILE
    nt = n // tm
    wrt = jnp.swapaxes(wr_all, 1, 2)
    return pl.pallas_call(
        _router_body,
        out_shape=[jax.ShapeDtypeStruct((n, d), BF16),
                   jax.ShapeDtypeStruct((MOE_TOP_K, n), jnp.int32),
                   jax.ShapeDtypeStruct((MOE_TOP_K, n), F32),
                   jax.ShapeDtypeStruct((n, SUBLANES), jnp.int32),
                   jax.ShapeDtypeStruct((nt, ne, 128), jnp.int32)],
        grid=(nt,),
        in_specs=[pl.BlockSpec((tm, d), lambda t: (t, 0)),
                  pl.BlockSpec((None, 1, d), lambda t: (layer, 0, 0)),
                  pl.BlockSpec((None, ne, d), lambda t: (widx, 0, 0))],
        out_specs=[pl.BlockSpec((tm, d), lambda t: (t, 0)),
                   pl.BlockSpec((MOE_TOP_K, tm), lambda t: (0, t)),
                   pl.BlockSpec((MOE_TOP_K, tm), lambda t: (0, t)),
                   pl.BlockSpec((tm, SUBLANES), lambda t: (t, 0)),
                   pl.BlockSpec((None, ne, 128), lambda t: (t, 0, 0))],
        compiler_params=_cparams("arbitrary"),
        name="router",
    )(h, g_all.reshape(depth, 1, d), wrt)


def _moe_tables(cnt, n_sorted_rows):
    cnt = cnt[:, :, 0]
    nt, ne = cnt.shape
    seg = (cnt + (MOE_SEG_ALIGN - 1)) // MOE_SEG_ALIGN * MOE_SEG_ALIGN
    seg_off = jnp.cumsum(seg, axis=1) - seg
    e_rows = jnp.sum(seg, axis=0)
    e_tiles = (e_rows + (MOE_ROW_TILE - 1)) // MOE_ROW_TILE
    e_cum = jnp.cumsum(e_tiles)
    e_base = (e_cum - e_tiles) * MOE_ROW_TILE
    dst = e_base[None, :] + jnp.cumsum(seg, axis=0) - seg
    n_used = e_cum[-1]
    tile_ids = jnp.minimum(jnp.arange(n_sorted_rows // MOE_ROW_TILE, dtype=jnp.int32), n_used - 1)
    tile_expert = jnp.sum(tile_ids[:, None] >= e_cum[None, :], axis=1)
    last_tile_row = e_base + (e_tiles - 1) * MOE_ROW_TILE
    i32 = lambda a: a.astype(jnp.int32)
    return dict(seg_off=i32(seg_off.reshape(-1)), seg_len=i32(seg.reshape(-1)), dst=i32(dst.reshape(-1)),
                n_used=i32(n_used.reshape(1)), tile_expert=i32(tile_expert),
                last_tile_row=i32(last_tile_row), has_rows=i32(e_tiles > 0))


def _moe_chunk_copies(t, ne, seg_off_ref, seg_len_ref, dst_ref, make_copy, act):
    for ex in range(ne):
        idx = t * ne + ex
        off = seg_off_ref[idx]
        ln = seg_len_ref[idx]
        row = dst_ref[idx]
        for size in MOE_CHUNK_SIZES:
            take = (ln & size) != 0

            @pl.when(take)
            def _(off=off, row=row, size=size):
                for cp in make_copy(pl.multiple_of(off, MOE_SEG_ALIGN), pl.multiple_of(row, MOE_SEG_ALIGN), size):
                    act(cp)

            step = jnp.where(take, size, 0)
            off = off + step
            row = row + step


def _dispatch_body(seg_off_ref, seg_len_ref, dst_ref, last_row_ref, has_rows_ref, n_used_ref,
                   xn_ref, posr_ref, wrow_ref, xs_ref, ws_ref, cbuf, wbuf, zx, zw, sem):
    t = pl.program_id(0)
    ne = last_row_ref.shape[0]
    n_rows, t_tokens = cbuf.shape[0], xn_ref.shape[0]

    def zero_copies(row):
        row = pl.multiple_of(row, MOE_ROW_TILE)
        return (pltpu.make_async_copy(zx, xs_ref.at[pl.ds(row, MOE_ROW_TILE)], sem),
                pltpu.make_async_copy(zw, ws_ref.at[pl.ds(row, MOE_ROW_TILE)], sem))

    @pl.when(t == 0)
    def _():
        zx[...] = jnp.zeros_like(zx)
        zw[...] = jnp.zeros_like(zw)
        for act in (lambda cp: cp.start(), lambda cp: cp.wait()):
            for ex in range(ne):
                @pl.when(has_rows_ref[ex] != 0)
                def _(ex=ex):
                    for cp in zero_copies(last_row_ref[ex]):
                        act(cp)

        def tail_body(i, carry):
            for cp in zero_copies(i * MOE_ROW_TILE):
                cp.start()
                cp.wait()
            return carry

        lax.fori_loop(n_used_ref[0], xs_ref.shape[0] // MOE_ROW_TILE, tail_body, 0)

    r_iota = lax.broadcasted_iota(jnp.int32, (n_rows, t_tokens), 0)
    hit1 = r_iota == posr_ref[0:1, :]
    hit2 = r_iota == posr_ref[1:2, :]
    onehot = jnp.where(hit1, 1.0, jnp.where(hit2, 1.0, 0.0)).astype(BF16)
    cbuf[...] = jnp.dot(onehot, xn_ref[...], preferred_element_type=F32).astype(BF16)
    wsel = jnp.where(hit1, wrow_ref[0:1, :], jnp.where(hit2, wrow_ref[1:2, :], 0.0))
    wbuf[...] = jnp.broadcast_to(jnp.sum(wsel, axis=1, keepdims=True), wbuf.shape)

    def make_copy(off, row, size):
        return (pltpu.make_async_copy(cbuf.at[pl.ds(off, size)], xs_ref.at[pl.ds(row, size)], sem),
                pltpu.make_async_copy(wbuf.at[pl.ds(off, size)], ws_ref.at[pl.ds(row, size)], sem))

    _moe_chunk_copies(t, ne, seg_off_ref, seg_len_ref, dst_ref, make_copy, lambda cp: cp.start())
    _moe_chunk_copies(t, ne, seg_off_ref, seg_len_ref, dst_ref, make_copy, lambda cp: cp.wait())


def _dispatch(xn, posr, wrow, tables, n_sorted_rows, ne):
    n, d = xn.shape
    tm = MOE_TOKEN_TILE
    n_rows = _moe_compact_rows(ne)
    grid_spec = pltpu.PrefetchScalarGridSpec(
        num_scalar_prefetch=6,
        grid=(n // tm,),
        in_specs=[pl.BlockSpec((tm, d), lambda t, *_: (t, 0)),
                  pl.BlockSpec((MOE_TOP_K, tm), lambda t, *_: (0, t)),
                  pl.BlockSpec((MOE_TOP_K, tm), lambda t, *_: (0, t))],
        out_specs=[pl.BlockSpec(memory_space=pl.ANY), pl.BlockSpec(memory_space=pl.ANY)],
        scratch_shapes=[pltpu.VMEM((n_rows, d), BF16), pltpu.VMEM((n_rows, 128), F32),
                        pltpu.VMEM((MOE_ROW_TILE, d), BF16), pltpu.VMEM((MOE_ROW_TILE, 128), F32),
                        pltpu.SemaphoreType.DMA],
    )
    return pl.pallas_call(
        _dispatch_body,
        out_shape=[jax.ShapeDtypeStruct((n_sorted_rows, d), BF16), jax.ShapeDtypeStruct((n_sorted_rows, 128), F32)],
        grid_spec=grid_spec,
        compiler_params=_cparams("arbitrary"),
        name="moe_dispatch",
    )(tables['seg_off'], tables['seg_len'], tables['dst'], tables['last_tile_row'], tables['has_rows'], tables['n_used'],
      xn, posr, wrow)


def _experts_body(tile_expert_ref, n_used_ref, xs_ref, ws_ref, wgu_ref, wd_ref, ys_ref, wgu16, wd16):
    i = pl.program_id(0)
    used = i < n_used_ref[0]
    prev = tile_expert_ref[jnp.maximum(i - 1, 0)]
    new_expert = jnp.logical_or(i == 0, tile_expert_ref[i] != prev)
    f = wd_ref.shape[0]
    fc = 512

    @pl.when(jnp.logical_and(used, new_expert))
    def _():
        for c in range(2 * f // fc):
            wgu16[:, c * fc:(c + 1) * fc] = wgu_ref[:, c * fc:(c + 1) * fc].astype(BF16)
        for c in range(f // fc):
            wd16[c * fc:(c + 1) * fc, :] = wd_ref[c * fc:(c + 1) * fc, :].astype(BF16)

    @pl.when(used)
    def _():
        x = xs_ref[...]
        w = ws_ref[...]
        wrep = jnp.concatenate([w] * (fc // w.shape[1]), axis=1)
        acc = None
        for c in range(f // fc):
            gate = jnp.dot(x, wgu16[:, c * fc:(c + 1) * fc], preferred_element_type=F32)
            up = jnp.dot(x, wgu16[:, f + c * fc:f + (c + 1) * fc], preferred_element_type=F32)
            act = (gate * _sigmoid(gate) * up * wrep).astype(BF16)
            part = jnp.dot(act, wd16[c * fc:(c + 1) * fc, :], preferred_element_type=F32)
            acc = part if acc is None else acc + part
        ys_ref[...] = acc.astype(ys_ref.dtype)

    @pl.when(jnp.logical_not(used))
    def _():
        ys_ref[...] = jnp.zeros_like(ys_ref)


def _experts(xs, ws, tables, w_gu_all, w_down_all, widx):
    rows, d = xs.shape
    f = w_down_all.shape[-2]
    tile = lambda w: pl.BlockSpec((MOE_ROW_TILE, w), lambda i, te, nu: (jnp.minimum(i, nu[0] - 1), 0))
    grid_spec = pltpu.PrefetchScalarGridSpec(
        num_scalar_prefetch=2,
        grid=(rows // MOE_ROW_TILE,),
        in_specs=[tile(d), tile(ws.shape[1]),
                  pl.BlockSpec((None, None, d, 2 * f), lambda i, te, nu: (widx, te[i], 0, 0), pipeline_mode=pl.Buffered(1)),
                  pl.BlockSpec((None, None, f, d), lambda i, te, nu: (widx, te[i], 0, 0), pipeline_mode=pl.Buffered(1))],
        out_specs=pl.BlockSpec((MOE_ROW_TILE, d), lambda i, te, nu: (i, 0)),
        scratch_shapes=[pltpu.VMEM((d, 2 * f), BF16), pltpu.VMEM((f, d), BF16)],
    )
    return pl.pallas_call(
        _experts_body,
        out_shape=jax.ShapeDtypeStruct((rows, d), BF16),
        grid_spec=grid_spec,
        compiler_params=_cparams("arbitrary"),
        name="moe_experts",
    )(tables['tile_expert'], tables['n_used'], xs, ws, w_gu_all, w_down_all)


def _combine_body(seg_off_ref, seg_len_ref, dst_ref, ys_ref, posc_ref, h_ref, o_ref, ybuf, sem, *, ne):
    t = pl.program_id(0)
    t_tokens, n_rows = h_ref.shape[0], ybuf.shape[0]

    @pl.when(t == 0)
    def _():
        ybuf[...] = jnp.zeros_like(ybuf)

    def make_copy(off, row, size):
        return (pltpu.make_async_copy(ys_ref.at[pl.ds(row, size)], ybuf.at[pl.ds(off, size)], sem),)

    _moe_chunk_copies(t, ne, seg_off_ref, seg_len_ref, dst_ref, make_copy, lambda cp: cp.start())
    _moe_chunk_copies(t, ne, seg_off_ref, seg_len_ref, dst_ref, make_copy, lambda cp: cp.wait())

    pos = posc_ref[...]
    lane_r = lax.broadcasted_iota(jnp.int32, (t_tokens, n_rows), 1)
    onehot = jnp.where(lane_r == pos[:, 0:1], 1.0, jnp.where(lane_r == pos[:, 1:2], 1.0, 0.0)).astype(BF16)
    o_ref[...] = h_ref[...] + jnp.dot(onehot, ybuf[...], preferred_element_type=F32)


def _combine(ys, posc, h, tables, ne):
    n, d = h.shape
    tm = MOE_TOKEN_TILE
    grid_spec = pltpu.PrefetchScalarGridSpec(
        num_scalar_prefetch=3,
        grid=(n // tm,),
        in_specs=[pl.BlockSpec(memory_space=pl.ANY),
                  pl.BlockSpec((tm, posc.shape[1]), lambda t, *_: (t, 0)),
                  pl.BlockSpec((tm, d), lambda t, *_: (t, 0))],
        out_specs=pl.BlockSpec((tm, d), lambda t, *_: (t, 0)),
        scratch_shapes=[pltpu.VMEM((_moe_compact_rows(ne), d), BF16), pltpu.SemaphoreType.DMA],
    )
    return pl.pallas_call(
        functools.partial(_combine_body, ne=ne),
        out_shape=jax.ShapeDtypeStruct((n, d), F32),
        grid_spec=grid_spec,
        compiler_params=_cparams("arbitrary"),
        name="moe_combine",
    )(tables['seg_off'], tables['seg_len'], tables['dst'], ys, posc, h)


def _moe(h, g_all, layer, wr_all, w_gu_all, w_down_all, widx):
    n = h.shape[0]
    ne = wr_all.shape[-1]
    xn, posr, wrow, posc, cnt = _router(h, g_all, layer, wr_all, widx)
    n_sorted_rows = _moe_sorted_rows(n, ne)
    tables = _moe_tables(cnt, n_sorted_rows)
    xs, ws = _dispatch(xn, posr, wrow, tables, n_sorted_rows, ne)
    ys = _experts(xs, ws, tables, w_gu_all, w_down_all, widx)
    return _combine(ys, posc, h, tables, ne)


def _final_norm_body(h_ref, g_ref, o_ref):
    o_ref[...] = _rms_norm_f32(h_ref[...], g_ref[...])


def _final_norm(h, g, tm=1024):
    n, d = h.shape
    return pl.pallas_call(
        _final_norm_body,
        out_shape=jax.ShapeDtypeStruct((n, d), F32),
        grid=(n // tm,),
        in_specs=[pl.BlockSpec((tm, d), lambda t: (t, 0)), pl.BlockSpec((1, d), lambda t: (0, 0))],
        out_specs=pl.BlockSpec((tm, d), lambda t: (t, 0)),
        compiler_params=_cparams("arbitrary"),
        name="final_norm",
    )(h, g.reshape(1, d))


def kernel(x, norm_mix_g, w_in, b_in, na_rpb, conv_w, conv_b, lru_wa, lru_ba, lru_wx, lru_bx, lru_lambda, w_branch, w_out, norm_ffn_g, ffn_w_gu, ffn_w_down, router_w, moe_w_gu, moe_w_down, final_g):
    batch, seq, d = x.shape
    depth = w_in.shape[0]
    n = batch * seq
    bw = w_branch.shape[2]
    h = x.reshape(n, d)

    tables = _na_bias_tables(na_rpb)
    twiddles = _fourier_twiddles(seq, bw)
    w_gate = _lru_gate_weights(lru_wa, lru_wx)
    wb16 = w_branch.astype(BF16)
    wo16 = w_out.astype(BF16)
    tn = 3 * bw

    for l in range(depth):
        q, k, v = _norm_proj(h, norm_mix_g, w_in, b_in, l, 0, 1, tn, 3)
        u_f, u_x, u_g = _norm_proj(h, norm_mix_g, w_in, b_in, l, 1, 1, tn, 3)
        (gates,) = _norm_proj(h, norm_mix_g, w_in, b_in, l, 2, (3 * d) // tn, tn, 1)
        y_a = _neighbourhood_attention(q, k, v, tables, l, batch)
        y_b = _fourier_mix(u_f, batch, twiddles)
        y_c = _recurrent_branch(u_x, u_g, conv_w, conv_b, w_gate, lru_ba, lru_bx, lru_lambda, l, batch)
        h = _merge(y_a, y_b, y_c, gates, h, wb16, wo16, l)
        if l % 2 == 0:
            h = _ffn(h, norm_ffn_g, l, ffn_w_gu, ffn_w_down, l // 2)
        else:
            h = _moe(h, norm_ffn_g, l, router_w, moe_w_gu, moe_w_down, l // 2)
    return _final_norm(h, final_g).reshape(batch, seq, d)
```

```python
import functools

import numpy as np
import jax
import jax.numpy as jnp
from jax import lax
from jax.experimental import pallas as pl
from jax.experimental.pallas import tpu as pltpu

F32 = jnp.float32
BF16 = jnp.bfloat16

RMS_EPS = 1e-6
GRID_W = 64
NA_HEADS = 8
NA_HEAD_DIM = 64
NA_KH = 8
NA_KW = 16
NA_ROWS_PER_BLOCK = 8
NA_ROWS_IN_FLIGHT = 4
LRU_C = 8.0
CONV_W = 4
CONV_PAD_LEFT = 2
N_EXPERTS = 8
MASK_VALUE = -1e30

VMEM_LIMIT_BYTES = 56 * 1024 * 1024


def _cparams(*sem):
    return pltpu.CompilerParams(dimension_semantics=sem, vmem_limit_bytes=VMEM_LIMIT_BYTES)


def _rms_norm_f32(x, g):
    ms = jnp.mean(x * x, axis=-1, keepdims=True)
    return x * lax.rsqrt(ms + RMS_EPS) * g


def _sigmoid(x):
    return 1.0 / (1.0 + jnp.exp(-x))


def _norm_proj_body(x_ref, g_ref, w_ref, b_ref, *out_refs):
    xn = _rms_norm_f32(x_ref[...], g_ref[...]).astype(BF16)
    r = jnp.dot(xn, w_ref[...].astype(BF16), preferred_element_type=F32) + b_ref[...]
    width = r.shape[1] // len(out_refs)
    for i, o in enumerate(out_refs):
        o[...] = r[:, i * width:(i + 1) * width].astype(o.dtype)


def _norm_proj(h, g_all, w_all, b_all, layer, col_block0, n_col_blocks, tn, n_out, tm=1024):
    n, d = h.shape
    depth = w_all.shape[0]
    g3 = g_all.reshape(depth, 1, d)
    b3 = b_all.reshape(depth, 1, -1)
    width = tn // n_out
    out_shape = [jax.ShapeDtypeStruct((n, n_col_blocks * width), BF16) for _ in range(n_out)]
    return pl.pallas_call(
        _norm_proj_body,
        out_shape=out_shape,
        grid=(n_col_blocks, n // tm),
        in_specs=[
            pl.BlockSpec((tm, d), lambda c, t: (t, 0)),
            pl.BlockSpec((None, 1, d), lambda c, t: (layer, 0, 0)),
            pl.BlockSpec((None, d, tn), lambda c, t: (layer, 0, col_block0 + c)),
            pl.BlockSpec((None, 1, tn), lambda c, t: (layer, 0, col_block0 + c)),
        ],
        out_specs=[pl.BlockSpec((tm, width), lambda c, t: (t, c)) for _ in range(n_out)],
        compiler_params=_cparams("arbitrary", "arbitrary"),
        name="norm_proj",
    )(h, g3, w_all, b3)


def _na_bias_tables(rpb_all):
    cols = np.arange(GRID_W)
    col_start = np.clip(cols - NA_KW // 2, 0, GRID_W - NA_KW)
    cc = np.arange(GRID_W)[None, :]
    in_win = (cc >= col_start[:, None]) & (cc < col_start[:, None] + NA_KW)
    col_off = cc - cols[:, None] + (NA_KW - 1)
    onehot = np.zeros((2 * NA_KW - 1, GRID_W, GRID_W), np.float32)
    cq, ck = np.nonzero(in_win)
    onehot[col_off[cq, ck], cq, ck] = 1.0
    t = jnp.einsum('lhro,ocd->lhrcd', rpb_all.astype(F32), jnp.asarray(onehot),
                   precision=lax.Precision.HIGHEST)
    t = jnp.where(jnp.asarray(in_win)[None, None, None], t, MASK_VALUE)
    depth, heads, n_off = t.shape[:3]
    t = t.reshape(depth, heads // 2, 2, n_off, GRID_W, GRID_W)
    t = jnp.transpose(t, (0, 1, 3, 2, 4, 5)).reshape(depth, heads // 2, n_off, 2 * GRID_W, GRID_W)
    return jnp.concatenate([t[:, :, :-1], t[:, :, 1:]], axis=-1)


def _na_body(q_ref, kp_ref, kc_ref, kn_ref, vp_ref, vc_ref, vn_ref, tbl_ref, o_ref, kbuf, vbuf, s_scr, e_scr, *, n_blocks):
    j = pl.program_id(1)
    blk = NA_ROWS_PER_BLOCK * GRID_W
    kbuf[0:blk, :] = kp_ref[...]
    kbuf[blk:2 * blk, :] = kc_ref[...]
    kbuf[2 * blk:3 * blk, :] = kn_ref[...]
    vbuf[0:blk, :] = vp_ref[...]
    vbuf[blk:2 * blk, :] = vc_ref[...]
    vbuf[2 * blk:3 * blk, :] = vn_ref[...]
    band = NA_KH * GRID_W
    half = NA_KH // 2
    lane = lax.broadcasted_iota(jnp.int32, (GRID_W, 2 * NA_HEAD_DIM), 1)
    lo = lane < NA_HEAD_DIM
    qscale = NA_HEAD_DIM ** -0.5
    head_mask = (jnp.where(lo, qscale, 0.0).astype(BF16), jnp.where(lo, 0.0, qscale).astype(BF16))

    n_pairs = NA_HEADS // 2
    cols = [slice(p * 2 * NA_HEAD_DIM, (p + 1) * 2 * NA_HEAD_DIM) for p in range(n_pairs)]

    def row_offsets(rl):
        start_first = NA_ROWS_PER_BLOCK + jnp.maximum(rl - half, 0)
        start_last = NA_ROWS_PER_BLOCK + jnp.minimum(rl - half, 0)
        start = jnp.where(j == 0, start_first, jnp.where(j == n_blocks - 1, start_last, rl + half))
        delta = jnp.where(j == 0, jnp.minimum(rl, half), jnp.where(j == n_blocks - 1, jnp.maximum(rl, half), half))
        return pl.multiple_of(rl * GRID_W, GRID_W), pl.multiple_of(start * GRID_W, GRID_W), delta

    def scores(rl, slot):
        qrow, krow, delta = row_offsets(rl)
        for p in range(n_pairs):
            q2 = q_ref[pl.ds(qrow, GRID_W), cols[p]]
            qs = jnp.concatenate([q2 * head_mask[0], q2 * head_mask[1]], axis=0)
            k2 = kbuf[pl.ds(krow, band), cols[p]]
            s = lax.dot_general(qs, k2, (((1,), (1,)), ((), ())), preferred_element_type=F32)
            bias = jnp.concatenate([tbl_ref[p, 2 * m - delta + (NA_KH - 1)] for m in range(NA_KH // 2)], axis=1)
            s_scr[slot, p] = s + bias

    def softmax(slot):
        inv_l = []
        for p in range(n_pairs):
            s = s_scr[slot, p]
            m = jnp.max(s, axis=-1, keepdims=True)
            e = jnp.exp(s - m)
            inv_l.append(1.0 / jnp.sum(e, axis=-1, keepdims=True))
            e_scr[slot, p] = e.astype(BF16)
        return inv_l

    def weighted_values(rl, slot, inv_l):
        qrow, krow, _ = row_offsets(rl)
        for p in range(n_pairs):
            v2 = vbuf[pl.ds(krow, band), cols[p]]
            o = jnp.dot(e_scr[slot, p], v2, preferred_element_type=F32) * inv_l[p]
            o_ref[pl.ds(qrow, GRID_W), cols[p]] = jnp.where(lo, o[:GRID_W], o[GRID_W:]).astype(o_ref.dtype)

    def rows_body(it, carry):
        rows = [it * NA_ROWS_IN_FLIGHT + r for r in range(NA_ROWS_IN_FLIGHT)]
        for slot, rl in enumerate(rows):
            scores(rl, slot)
        inv = [softmax(slot) for slot in range(NA_ROWS_IN_FLIGHT)]
        for slot, rl in enumerate(rows):
            weighted_values(rl, slot, inv[slot])
        return carry

    lax.fori_loop(0, NA_ROWS_PER_BLOCK // NA_ROWS_IN_FLIGHT, rows_body, 0)


def _neighbourhood_attention(q, k, v, tables, layer, batch):
    n, width = q.shape
    n_pairs = NA_HEADS // 2
    blk = NA_ROWS_PER_BLOCK * GRID_W
    n_blocks = n // batch // blk
    assert n_blocks >= 2

    def tok(off):
        return lambda b, j: (b * n_blocks + jnp.clip(j + off, 0, n_blocks - 1), 0)

    tile = lambda off: pl.BlockSpec((blk, width), tok(off))
    return pl.pallas_call(
        functools.partial(_na_body, n_blocks=n_blocks),
        out_shape=jax.ShapeDtypeStruct((n, width), BF16),
        grid=(batch, n_blocks),
        in_specs=[tile(0), tile(-1), tile(0), tile(1), tile(-1), tile(0), tile(1),
                  pl.BlockSpec((None,) + tables.shape[1:], lambda b, j: (layer, 0, 0, 0, 0))],
        out_specs=tile(0),
        scratch_shapes=[pltpu.VMEM((3 * blk, width), BF16), pltpu.VMEM((3 * blk, width), BF16),
                        pltpu.VMEM((NA_ROWS_IN_FLIGHT, n_pairs, 2 * GRID_W, NA_KH * GRID_W), F32),
                        pltpu.VMEM((NA_ROWS_IN_FLIGHT, n_pairs, 2 * GRID_W, NA_KH * GRID_W), BF16)],
        compiler_params=_cparams("arbitrary", "arbitrary"),
        name="neigh_attn",
    )(q, k, k, k, v, v, v, tables)


FN_N2 = 128
FN_GROUP_DIM = 64
FN_K1_PER_STEP = 4


def _dft_cos_sin(n):
    ang = 2.0 * np.pi * (np.outer(np.arange(n), np.arange(n)) % n) / n
    return np.cos(ang), np.sin(ang)


def _fourier_stage1_body(x_ref, f_ref, tc_ref, ts_ref, zr_ref, zi_ref):
    n1 = x_ref.shape[0]
    z = jnp.dot(f_ref[...], x_ref[...], preferred_element_type=F32)
    zr, zi = z[:n1], z[n1:]
    tc, ts = tc_ref[...], ts_ref[...]
    zr_ref[...] = (zr * tc + zi * ts).astype(zr_ref.dtype)
    zi_ref[...] = (zi * tc - zr * ts).astype(zi_ref.dtype)


def _fourier_stage2_body(zr_ref, zi_ref, f_ref, c_ref, o_ref, *, scale):
    n2 = zr_ref.shape[1]
    width = zr_ref.shape[2]
    for i in range(zr_ref.shape[0]):
        z = jnp.concatenate([zr_ref[i], zi_ref[i]], axis=0)
        y = jnp.dot(f_ref[...], z, preferred_element_type=F32)
        yc = jnp.concatenate([y[:n2], y[n2:]], axis=1).astype(BF16)
        out = jnp.dot(yc, c_ref[...], preferred_element_type=F32) * scale
        o_ref[:, i * width:(i + 1) * width] = out.astype(o_ref.dtype)


def _fourier_mix(u, batch, twiddles):
    n, width = u.shape
    s = n // batch
    n2 = FN_N2
    n1 = s // n2
    tc, ts = twiddles
    c1, s1 = _dft_cos_sin(n1)
    f1 = jnp.asarray(np.concatenate([c1, -s1], axis=0), F32).astype(BF16)
    c2, s2 = _dft_cos_sin(n2)
    f2 = jnp.asarray(np.block([[c2, s2], [-s2, c2]]), F32).astype(BF16)
    cg, sg = _dft_cos_sin(FN_GROUP_DIM)
    eye = np.eye(width // FN_GROUP_DIM)
    fc = jnp.asarray(np.concatenate([np.kron(eye, cg), np.kron(eye, sg)], axis=0), F32).astype(BF16)

    x2 = u.reshape(batch, n1, n2 * width)
    tn = 16 * width
    zr, zi = pl.pallas_call(
        _fourier_stage1_body,
        out_shape=[jax.ShapeDtypeStruct((batch, n1, n2 * width), BF16)] * 2,
        grid=(batch, n2 * width // tn),
        in_specs=[pl.BlockSpec((None, n1, tn), lambda b, j: (b, 0, j)),
                  pl.BlockSpec((2 * n1, n1), lambda b, j: (0, 0)),
                  pl.BlockSpec((n1, tn), lambda b, j: (0, j)),
                  pl.BlockSpec((n1, tn), lambda b, j: (0, j))],
        out_specs=[pl.BlockSpec((None, n1, tn), lambda b, j: (b, 0, j))] * 2,
        compiler_params=_cparams("arbitrary", "arbitrary"),
        name="fourier_stage1",
    )(x2, f1, tc, ts)

    zr4 = zr.reshape(batch, n1, n2, width)
    zi4 = zi.reshape(batch, n1, n2, width)
    nk = FN_K1_PER_STEP
    out = pl.pallas_call(
        functools.partial(_fourier_stage2_body, scale=float(1.0 / np.sqrt(s * FN_GROUP_DIM))),
        out_shape=jax.ShapeDtypeStruct((batch, n2, n1 * width), BF16),
        grid=(batch, n1 // nk),
        in_specs=[pl.BlockSpec((None, nk, n2, width), lambda b, j: (b, j, 0, 0)),
                  pl.BlockSpec((None, nk, n2, width), lambda b, j: (b, j, 0, 0)),
                  pl.BlockSpec((2 * n2, 2 * n2), lambda b, j: (0, 0)),
                  pl.BlockSpec((2 * width, width), lambda b, j: (0, 0))],
        out_specs=pl.BlockSpec((None, n2, nk * width), lambda b, j: (b, 0, j)),
        compiler_params=_cparams("arbitrary", "arbitrary"),
        name="fourier_stage2",
    )(zr4, zi4, f2, fc)
    return out.reshape(n, width)


def _fourier_twiddles(s, width):
    n2 = FN_N2
    n1 = s // n2
    ang = (2.0 * np.pi / s) * (jnp.arange(n1, dtype=F32)[:, None] * jnp.arange(n2, dtype=F32)[None, :])
    tc = jnp.broadcast_to(jnp.cos(ang)[:, :, None], (n1, n2, width)).reshape(n1, n2 * width)
    ts = jnp.broadcast_to(jnp.sin(ang)[:, :, None], (n1, n2, width)).reshape(n1, n2 * width)
    return tc, ts


LRU_LANES = 128
LRU_SEGMENTS = 16
LRU_JCHUNK = 32
SUBLANES = 8


def _lru_gate_weights(wa_all, wx_all):
    depth, _, nb, db, _ = wa_all.shape
    ncol = nb // 2

    def blockdiag(w):
        w = w.reshape(depth, ncol, 2, db, db)
        z = jnp.zeros_like(w[:, :, 0])
        top = jnp.concatenate([w[:, :, 0], z], axis=-1)
        bot = jnp.concatenate([z, w[:, :, 1]], axis=-1)
        return jnp.concatenate([top, bot], axis=-2)

    parts = [blockdiag(wa_all[:, 0]), blockdiag(wx_all[:, 0]), blockdiag(wa_all[:, 1]), blockdiag(wx_all[:, 1])]
    return jnp.concatenate(parts, axis=-1).astype(BF16)


def _gelu_tanh(x):
    return 0.5 * x * (1.0 + jnp.tanh(np.sqrt(2.0 / np.pi) * (x + 0.044715 * (x * x * x))))


def _lru_body(ux_ref, ug_ref, cw_ref, cb_ref, w_ref, ba_ref, bx_ref, lam_ref, o_ref, h_scr, p_scr):
    n_j, n_g, lanes = ux_ref.shape
    jc = LRU_JCHUNK
    n_chunks = n_j // jc
    seg = lax.broadcasted_iota(jnp.int32, (n_g, lanes), 0)
    seg3 = lax.broadcasted_iota(jnp.int32, (jc, n_g, lanes), 1)
    row3 = lax.broadcasted_iota(jnp.int32, (jc, n_g, lanes), 0)

    def from_prev_segment(x):
        return jnp.where(seg >= 1, pltpu.roll(x, 1, axis=0), 0.0)

    def from_next_segment(x):
        return jnp.where(seg < n_g - 1, pltpu.roll(x, n_g - 1, axis=0), 0.0)

    def conv_chunk(j0):
        main = ux_ref[pl.ds(j0, jc)].astype(F32)
        lo_in = ux_ref[pl.ds(jnp.maximum(j0 - CONV_PAD_LEFT, 0), CONV_PAD_LEFT)].astype(F32)
        tail = ux_ref[n_j - CONV_PAD_LEFT:n_j].astype(F32)
        lo_wrap = jnp.stack([from_prev_segment(tail[r]) for r in range(CONV_PAD_LEFT)], axis=0)
        lo = jnp.where(j0 > 0, lo_in, lo_wrap)
        n_hi = CONV_W - 1 - CONV_PAD_LEFT
        hi_in = ux_ref[pl.ds(jnp.minimum(j0 + jc, n_j - n_hi), n_hi)].astype(F32)
        head = ux_ref[0:n_hi].astype(F32)
        hi_wrap = jnp.stack([from_next_segment(head[r]) for r in range(n_hi)], axis=0)
        hi = jnp.where(j0 + jc < n_j, hi_in, hi_wrap)
        ext = jnp.concatenate([lo, main, hi], axis=0)
        c = ext[0:jc] * cw_ref[0:1, :] + cb_ref[...]
        for tap in range(1, CONV_W):
            c = c + ext[tap:tap + jc] * cw_ref[tap:tap + 1, :]
        return c

    def gates(c, pre, d, j0):
        r = _sigmoid(pre[:, 2 * d * lanes:(2 * d + 1) * lanes] + ba_ref[d:d + 1, :])
        i = _sigmoid(pre[:, (2 * d + 1) * lanes:(2 * d + 2) * lanes] + bx_ref[d:d + 1, :])
        lam = lam_ref[d:d + 1, :]
        softplus = jnp.maximum(-lam, 0.0) + jnp.log(1.0 + jnp.exp(-jnp.abs(lam)))
        a = jnp.exp(-LRU_C * r * softplus)
        om = 1.0 - a * a
        mult = jnp.where(om > 0.0, om * lax.rsqrt(om), 0.0)
        a = a.reshape(jc, n_g, lanes)
        gain = (mult * i).reshape(jc, n_g, lanes)
        first = (seg3 == (n_g - 1) * d) & (row3 + j0 == (n_j - 1) * d)
        return a, jnp.where(first, i.reshape(jc, n_g, lanes), gain) * c

    def local_scan(a, b, carry, d):
        h, p = carry
        hs, ps = [None] * jc, [None] * jc
        for jj in (range(jc) if d == 0 else range(jc - 1, -1, -1)):
            h = a[jj] * h + b[jj]
            p = a[jj] * p
            hs[jj], ps[jj] = h, p
        return jnp.stack(hs, axis=0), jnp.stack(ps, axis=0), (h, p)

    scan_init = (jnp.zeros((n_g, lanes), F32), jnp.ones((n_g, lanes), F32))

    def gate_and_forward_body(ci, carry):
        j0 = ci * jc
        rows = pl.ds(j0, jc)
        c = conv_chunk(j0)
        pre = jnp.dot(c.reshape(jc * n_g, lanes).astype(BF16), w_ref[...], preferred_element_type=F32)
        a_b, b_b = gates(c, pre, 1, j0)
        p_scr[1, rows] = a_b
        h_scr[1, rows] = b_b
        a_f, b_f = gates(c, pre, 0, j0)
        h_scr[0, rows], p_scr[0, rows], carry = local_scan(a_f, b_f, carry, 0)
        return carry

    lax.fori_loop(0, n_chunks, gate_and_forward_body, scan_init)

    def backward_body(ci, carry):
        rows = pl.ds((n_chunks - 1 - ci) * jc, jc)
        h_scr[1, rows], p_scr[1, rows], carry = local_scan(p_scr[1, rows], h_scr[1, rows], carry, 1)
        return carry

    lax.fori_loop(0, n_chunks, backward_body, scan_init)

    def carry_in(d):
        edge = n_j - 1 if d == 0 else 0
        h_end, p_end = h_scr[d, edge], p_scr[d, edge]
        state = jnp.zeros((1, lanes), F32)
        out = jnp.zeros((n_g, lanes), F32)
        for g in (range(n_g) if d == 0 else range(n_g - 1, -1, -1)):
            out = jnp.where(seg == g, state, out)
            state = h_end[g:g + 1] + p_end[g:g + 1] * state
        return out

    e_fwd, e_bwd = carry_in(0), carry_in(1)

    def out_body(ci, carry):
        j0 = ci * jc
        rows = pl.ds(j0, jc)
        h = h_scr[0, rows] + p_scr[0, rows] * e_fwd + h_scr[1, rows] + p_scr[1, rows] * e_bwd
        o_ref[rows] = (h * _gelu_tanh(ug_ref[rows].astype(F32))).astype(o_ref.dtype)
        return carry

    lax.fori_loop(0, n_chunks, out_body, 0)


def _recurrent_branch(u_x, u_g, conv_w, conv_b, w_gate, ba, bx, lam, layer, batch):
    n, width = u_x.shape
    s = n // batch
    depth = conv_w.shape[0]
    ncol = width // LRU_LANES
    n_g = LRU_SEGMENTS
    n_j = s // n_g

    def to_segments(u):
        return jnp.transpose(u.reshape(batch, n_g, n_j, width), (0, 2, 1, 3))

    cb3 = conv_b.reshape(depth, 1, width)
    seq_spec = pl.BlockSpec((None, n_j, n_g, LRU_LANES), lambda b, c: (b, 0, 0, c))
    par = lambda rows: pl.BlockSpec((None, rows, LRU_LANES), lambda b, c: (layer, 0, c))
    out = pl.pallas_call(
        _lru_body,
        out_shape=jax.ShapeDtypeStruct((batch, n_j, n_g, width), BF16),
        grid=(batch, ncol),
        in_specs=[seq_spec, seq_spec, par(CONV_W), par(1),
                  pl.BlockSpec((None, None, LRU_LANES, 4 * LRU_LANES), lambda b, c: (layer, c, 0, 0)),
                  par(2), par(2), par(2)],
        out_specs=seq_spec,
        scratch_shapes=[pltpu.VMEM((2, n_j, n_g, LRU_LANES), F32), pltpu.VMEM((2, n_j, n_g, LRU_LANES), F32)],
        compiler_params=_cparams("arbitrary", "arbitrary"),
        name="rg_lru",
    )(to_segments(u_x), to_segments(u_g), conv_w, cb3, w_gate, ba, bx, lam)
    return jnp.transpose(out, (0, 2, 1, 3)).reshape(n, width)


def _merge_body(ya_ref, yb_ref, yc_ref, gt_ref, h_ref, wb_ref, wo_ref, o_ref):
    d = h_ref.shape[1]
    merged = None
    for kbr, y_ref in enumerate((ya_ref, yb_ref, yc_ref)):
        ybr = jnp.dot(y_ref[...], wb_ref[kbr], preferred_element_type=F32)
        term = _sigmoid(gt_ref[:, kbr * d:(kbr + 1) * d].astype(F32)) * ybr
        merged = term if merged is None else merged + term
    o_ref[...] = h_ref[...] + jnp.dot(merged.astype(BF16), wo_ref[...], preferred_element_type=F32)


def _merge(ya, yb, yc, gates, h, wb_all, wo_all, layer, tm=1024):
    n, d = h.shape
    bw = ya.shape[1]
    ytile = pl.BlockSpec((tm, bw), lambda t: (t, 0))
    return pl.pallas_call(
        _merge_body,
        out_shape=jax.ShapeDtypeStruct((n, d), F32),
        grid=(n // tm,),
        in_specs=[ytile, ytile, ytile,
                  pl.BlockSpec((tm, 3 * d), lambda t: (t, 0)),
                  pl.BlockSpec((tm, d), lambda t: (t, 0)),
                  pl.BlockSpec((None, 3, bw, d), lambda t: (layer, 0, 0, 0)),
                  pl.BlockSpec((None, d, d), lambda t: (layer, 0, 0))],
        out_specs=pl.BlockSpec((tm, d), lambda t: (t, 0)),
        compiler_params=_cparams("arbitrary"),
        name="branch_merge",
    )(ya, yb, yc, gates, h, wb_all, wo_all)


def _ffn_body(h_ref, g_ref, wg_ref, wu_ref, wd_ref, o_ref, xn_ref, acc_ref):
    j = pl.program_id(1)

    @pl.when(j == 0)
    def _():
        h = h_ref[...]
        xn_ref[...] = _rms_norm_f32(h, g_ref[...]).astype(BF16)
        acc_ref[...] = h

    xn = xn_ref[...]
    gate = jnp.dot(xn, wg_ref[...].astype(BF16), preferred_element_type=F32)
    up = jnp.dot(xn, wu_ref[...].astype(BF16), preferred_element_type=F32)
    act = gate * _sigmoid(gate) * up
    acc_ref[...] += jnp.dot(act.astype(BF16), wd_ref[...].astype(BF16), preferred_element_type=F32)

    @pl.when(j == pl.num_programs(1) - 1)
    def _():
        o_ref[...] = acc_ref[...]


def _ffn(h, g_all, layer, w_gu_all, w_down_all, widx, tm=1024, tf=512):
    n, d = h.shape
    depth = g_all.shape[0]
    nf = w_down_all.shape[-2] // tf
    return pl.pallas_call(
        _ffn_body,
        out_shape=jax.ShapeDtypeStruct((n, d), F32),
        grid=(n // tm, nf),
        in_specs=[pl.BlockSpec((tm, d), lambda t, j: (t, 0)),
                  pl.BlockSpec((None, 1, d), lambda t, j: (layer, 0, 0)),
                  pl.BlockSpec((None, d, tf), lambda t, j: (widx, 0, j)),
                  pl.BlockSpec((None, d, tf), lambda t, j: (widx, 0, j + nf)),
                  pl.BlockSpec((None, tf, d), lambda t, j: (widx, j, 0))],
        out_specs=pl.BlockSpec((tm, d), lambda t, j: (t, 0)),
        scratch_shapes=[pltpu.VMEM((tm, d), BF16), pltpu.VMEM((tm, d), F32)],
        compiler_params=_cparams("arbitrary", "arbitrary"),
        name="dense_ffn",
    )(h, g_all.reshape(depth, 1, d), w_gu_all, w_gu_all, w_down_all)


MOE_TOKEN_TILE = 512
MOE_ROW_TILE = 512
MOE_SEG_ALIGN = 16
MOE_TOP_K = 2
MOE_CHUNK_SIZES = tuple(MOE_SEG_ALIGN << b for b in range(5, -1, -1))
assert MOE_CHUNK_SIZES[0] == MOE_TOKEN_TILE


def _moe_compact_rows(ne):
    rows = MOE_TOKEN_TILE * MOE_TOP_K + ne * (MOE_SEG_ALIGN - 1)
    return -(-rows // MOE_SEG_ALIGN) * MOE_SEG_ALIGN


def _moe_sorted_rows(n, ne):
    rows = n * MOE_TOP_K + (n // MOE_TOKEN_TILE) * ne * (MOE_SEG_ALIGN - 1) + ne * (MOE_ROW_TILE - MOE_SEG_ALIGN)
    return -(-rows // MOE_ROW_TILE) * MOE_ROW_TILE


def _router_body(h_ref, g_ref, wrt_ref, xn_ref, posr_ref, wrow_ref, posc_ref, cnt_ref, before_scr, eye_scr):
    t_tokens = h_ref.shape[0]

    @pl.when(pl.program_id(0) == 0)
    def _():
        r_i = lax.broadcasted_iota(jnp.int32, (t_tokens, t_tokens), 0)
        c_i = lax.broadcasted_iota(jnp.int32, (t_tokens, t_tokens), 1)
        before_scr[...] = jnp.where(r_i < c_i, 1.0, 0.0).astype(BF16)
        eye_scr[...] = jnp.where(r_i == c_i, 1.0, 0.0).astype(BF16)

    xn = _rms_norm_f32(h_ref[...], g_ref[...])
    xn_hi = xn.astype(BF16)
    xn_ref[...] = xn_hi
    nt_dims = (((1,), (1,)), ((), ()))
    xn_lo = (xn - xn_hi.astype(F32)).astype(BF16)
    w = wrt_ref[...]
    ne = w.shape[0]
    w_hi = w.astype(BF16)
    w_lo = (w - w_hi.astype(F32)).astype(BF16)
    by_hi = lax.dot_general(jnp.concatenate([w_hi, w_lo], axis=0), xn_hi, nt_dims, preferred_element_type=F32)
    logits = by_hi[:ne] + by_hi[ne:] + lax.dot_general(w_hi, xn_lo, nt_dims, preferred_element_type=F32)
    sub = lax.broadcasted_iota(jnp.int32, logits.shape, 0)
    m1 = jnp.max(logits, axis=0, keepdims=True)
    i1 = jnp.min(jnp.where(logits == m1, sub, ne), axis=0, keepdims=True)
    rest = jnp.where(sub == i1, -jnp.inf, logits)
    m2 = jnp.max(rest, axis=0, keepdims=True)
    i2 = jnp.min(jnp.where(rest == m2, sub, ne), axis=0, keepdims=True)
    e = jnp.exp(m2 - m1)
    wrow_ref[...] = jnp.concatenate([1.0 / (1.0 + e), e / (1.0 + e)], axis=0)

    sel1, sel2 = sub == i1, sub == i2
    memb = jnp.where(sel1, 1.0, jnp.where(sel2, 1.0, 0.0))
    rank = jnp.dot(memb.astype(BF16), before_scr[...], preferred_element_type=F32)
    sub_c = lax.broadcasted_iota(jnp.int32, cnt_ref.shape, 0)
    cnt_out = jnp.zeros(cnt_ref.shape, F32)
    base = rank
    running = jnp.zeros((1, 1), F32)
    for ex in range(ne):
        c = jnp.sum(memb[ex:ex + 1, :], axis=1, keepdims=True)
        cnt_out = jnp.where(sub_c == ex, c, cnt_out)
        base = jnp.where(sub == ex, base + running, base)
        running = running + jnp.floor((c + (MOE_SEG_ALIGN - 1)) * (1.0 / MOE_SEG_ALIGN)) * MOE_SEG_ALIGN
    cnt_ref[...] = cnt_out.astype(jnp.int32)
    pos1 = jnp.sum(jnp.where(sel1, base, 0.0), axis=0, keepdims=True)
    pos2 = jnp.sum(jnp.where(sel2, base, 0.0), axis=0, keepdims=True)
    posr_ref[...] = jnp.concatenate([pos1, pos2], axis=0).astype(jnp.int32)
    digits = []
    for pos in (pos1, pos2):
        hi = jnp.floor(pos * (1.0 / 128.0))
        digits += [hi, pos - 128.0 * hi]
    pad = jnp.zeros((2 * SUBLANES - len(digits), t_tokens), F32)
    dig_t = lax.dot_general(eye_scr[...], jnp.concatenate(digits + [pad], axis=0).astype(BF16), nt_dims,
                            preferred_element_type=F32)
    d_i = lax.broadcasted_iota(jnp.int32, (2 * SUBLANES, posc_ref.shape[1]), 0)
    c_i = lax.broadcasted_iota(jnp.int32, (2 * SUBLANES, posc_ref.shape[1]), 1)
    recombine = jnp.where(d_i == 2 * c_i, 128.0, jnp.where(d_i == 2 * c_i + 1, 1.0, 0.0)).astype(BF16)
    posc = jnp.dot(dig_t.astype(BF16), recombine, preferred_element_type=F32)
    posc_ref[...] = posc.astype(jnp.int32)


def _router(h, g_all, layer, wr_all, widx):
    n, d = h.shape
    depth = g_all.shape[0]
    ne = wr_all.shape[-1]
    tm = MOE_TOKEN_TILE
    nt = n // tm
    wrt = jnp.swapaxes(wr_all, 1, 2)
    return pl.pallas_call(
        _router_body,
        out_shape=[jax.ShapeDtypeStruct((n, d), BF16),
                   jax.ShapeDtypeStruct((MOE_TOP_K, n), jnp.int32),
                   jax.ShapeDtypeStruct((MOE_TOP_K, n), F32),
                   jax.ShapeDtypeStruct((n, SUBLANES), jnp.int32),
                   jax.ShapeDtypeStruct((nt, ne, 128), jnp.int32)],
        grid=(nt,),
        in_specs=[pl.BlockSpec((tm, d), lambda t: (t, 0)),
                  pl.BlockSpec((None, 1, d), lambda t: (layer, 0, 0)),
                  pl.BlockSpec((None, ne, d), lambda t: (widx, 0, 0))],
        out_specs=[pl.BlockSpec((tm, d), lambda t: (t, 0)),
                   pl.BlockSpec((MOE_TOP_K, tm), lambda t: (0, t)),
                   pl.BlockSpec((MOE_TOP_K, tm), lambda t: (0, t)),
                   pl.BlockSpec((tm, SUBLANES), lambda t: (t, 0)),
                   pl.BlockSpec((None, ne, 128), lambda t: (t, 0, 0))],
        scratch_shapes=[pltpu.VMEM((tm, tm), BF16), pltpu.VMEM((tm, tm), BF16)],
        compiler_params=_cparams("arbitrary"),
        name="router",
    )(h, g_all.reshape(depth, 1, d), wrt)


def _moe_tables(cnt, n_sorted_rows):
    cnt = cnt[:, :, 0]
    nt, ne = cnt.shape
    seg = (cnt + (MOE_SEG_ALIGN - 1)) // MOE_SEG_ALIGN * MOE_SEG_ALIGN
    seg_off = jnp.cumsum(seg, axis=1) - seg
    e_rows = jnp.sum(seg, axis=0)
    e_tiles = (e_rows + (MOE_ROW_TILE - 1)) // MOE_ROW_TILE
    e_cum = jnp.cumsum(e_tiles)
    e_base = (e_cum - e_tiles) * MOE_ROW_TILE
    dst = e_base[None, :] + jnp.cumsum(seg, axis=0) - seg
    n_used = e_cum[-1]
    tile_ids = jnp.minimum(jnp.arange(n_sorted_rows // MOE_ROW_TILE, dtype=jnp.int32), n_used - 1)
    tile_expert = jnp.sum(tile_ids[:, None] >= e_cum[None, :], axis=1)
    last_tile_row = e_base + (e_tiles - 1) * MOE_ROW_TILE
    i32 = lambda a: a.astype(jnp.int32)
    return dict(seg_off=i32(seg_off.reshape(-1)), seg_len=i32(seg.reshape(-1)), dst=i32(dst.reshape(-1)),
                n_used=i32(n_used.reshape(1)), tile_expert=i32(tile_expert),
                last_tile_row=i32(last_tile_row), has_rows=i32(e_tiles > 0))


def _moe_chunk_copies(t, ne, seg_off_ref, seg_len_ref, dst_ref, make_copy, act):
    for ex in range(ne):
        idx = t * ne + ex
        off = seg_off_ref[idx]
        ln = seg_len_ref[idx]
        row = dst_ref[idx]
        for size in MOE_CHUNK_SIZES:
            take = (ln & size) != 0

            @pl.when(take)
            def _(off=off, row=row, size=size):
                for cp in make_copy(pl.multiple_of(off, MOE_SEG_ALIGN), pl.multiple_of(row, MOE_SEG_ALIGN), size):
                    act(cp)

            step = jnp.where(take, size, 0)
            off = off + step
            row = row + step


def _dispatch_body(seg_off_ref, seg_len_ref, dst_ref, last_row_ref, has_rows_ref, n_used_ref,
                   xn_ref, posr_ref, wrow_ref, xs_ref, ws_ref, cbuf, wbuf, zx, zw, sems, zsem):
    t = pl.program_id(0)
    n_tiles = pl.num_programs(0)
    ne = last_row_ref.shape[0]
    n_rows, t_tokens = cbuf.shape[1], xn_ref.shape[0]

    def zero_copies(row):
        row = pl.multiple_of(row, MOE_ROW_TILE)
        return (pltpu.make_async_copy(zx, xs_ref.at[pl.ds(row, MOE_ROW_TILE)], zsem),
                pltpu.make_async_copy(zw, ws_ref.at[pl.ds(row, MOE_ROW_TILE)], zsem))

    @pl.when(t == 0)
    def _():
        zx[...] = jnp.zeros_like(zx)
        zw[...] = jnp.zeros_like(zw)
        for act in (lambda cp: cp.start(), lambda cp: cp.wait()):
            for ex in range(ne):
                @pl.when(has_rows_ref[ex] != 0)
                def _(ex=ex):
                    for cp in zero_copies(last_row_ref[ex]):
                        act(cp)

        def tail_body(i, carry):
            for cp in zero_copies(i * MOE_ROW_TILE):
                cp.start()
                cp.wait()
            return carry

        lax.fori_loop(n_used_ref[0], xs_ref.shape[0] // MOE_ROW_TILE, tail_body, 0)

    r_iota = lax.broadcasted_iota(jnp.int32, (n_rows, t_tokens), 0)
    hit1 = r_iota == posr_ref[0:1, :]
    hit2 = r_iota == posr_ref[1:2, :]
    onehot = jnp.where(hit1, 1.0, jnp.where(hit2, 1.0, 0.0)).astype(BF16)
    slot = t % 2
    cbuf[slot] = jnp.dot(onehot, xn_ref[...], preferred_element_type=F32).astype(BF16)
    wsel = jnp.where(hit1, wrow_ref[0:1, :], jnp.where(hit2, wrow_ref[1:2, :], 0.0))
    wbuf[slot] = jnp.broadcast_to(jnp.sum(wsel, axis=1, keepdims=True), wbuf.shape[1:])

    def drain(tile, act):
        s = tile % 2

        def make_copy(off, row, size):
            return (pltpu.make_async_copy(cbuf.at[s, pl.ds(off, size)], xs_ref.at[pl.ds(row, size)], sems.at[s]),
                    pltpu.make_async_copy(wbuf.at[s, pl.ds(off, size)], ws_ref.at[pl.ds(row, size)], sems.at[s]))

        _moe_chunk_copies(tile, ne, seg_off_ref, seg_len_ref, dst_ref, make_copy, act)

    drain(t, lambda cp: cp.start())

    @pl.when(t > 0)
    def _():
        drain(t - 1, lambda cp: cp.wait())

    @pl.when(t == n_tiles - 1)
    def _():
        drain(t, lambda cp: cp.wait())


def _dispatch(xn, posr, wrow, tables, n_sorted_rows, ne):
    n, d = xn.shape
    tm = MOE_TOKEN_TILE
    n_rows = _moe_compact_rows(ne)
    grid_spec = pltpu.PrefetchScalarGridSpec(
        num_scalar_prefetch=6,
        grid=(n // tm,),
        in_specs=[pl.BlockSpec((tm, d), lambda t, *_: (t, 0)),
                  pl.BlockSpec((MOE_TOP_K, tm), lambda t, *_: (0, t)),
                  pl.BlockSpec((MOE_TOP_K, tm), lambda t, *_: (0, t))],
        out_specs=[pl.BlockSpec(memory_space=pl.ANY), pl.BlockSpec(memory_space=pl.ANY)],
        scratch_shapes=[pltpu.VMEM((2, n_rows, d), BF16), pltpu.VMEM((2, n_rows, 128), F32),
                        pltpu.VMEM((MOE_ROW_TILE, d), BF16), pltpu.VMEM((MOE_ROW_TILE, 128), F32),
                        pltpu.SemaphoreType.DMA((2,)), pltpu.SemaphoreType.DMA],
    )
    return pl.pallas_call(
        _dispatch_body,
        out_shape=[jax.ShapeDtypeStruct((n_sorted_rows, d), BF16), jax.ShapeDtypeStruct((n_sorted_rows, 128), F32)],
        grid_spec=grid_spec,
        compiler_params=_cparams("arbitrary"),
        name="moe_dispatch",
    )(tables['seg_off'], tables['seg_len'], tables['dst'], tables['last_tile_row'], tables['has_rows'], tables['n_used'],
      xn, posr, wrow)


def _experts_body(tile_expert_ref, n_used_ref, xs_ref, ws_ref, wgu_ref, wd_ref, ys_ref, wgu16, wd16):
    i = pl.program_id(0)
    used = i < n_used_ref[0]
    prev = tile_expert_ref[jnp.maximum(i - 1, 0)]
    new_expert = jnp.logical_or(i == 0, tile_expert_ref[i] != prev)
    f = wd_ref.shape[0]
    fc = 512

    @pl.when(jnp.logical_and(used, new_expert))
    def _():
        for c in range(2 * f // fc):
            wgu16[:, c * fc:(c + 1) * fc] = wgu_ref[:, c * fc:(c + 1) * fc].astype(BF16)
        for c in range(f // fc):
            wd16[c * fc:(c + 1) * fc, :] = wd_ref[c * fc:(c + 1) * fc, :].astype(BF16)

    @pl.when(used)
    def _():
        x = xs_ref[...]
        w = ws_ref[...]
        wrep = jnp.concatenate([w] * (fc // w.shape[1]), axis=1)
        acc = None
        for c in range(f // fc):
            gate = jnp.dot(x, wgu16[:, c * fc:(c + 1) * fc], preferred_element_type=F32)
            up = jnp.dot(x, wgu16[:, f + c * fc:f + (c + 1) * fc], preferred_element_type=F32)
            act = (gate * _sigmoid(gate) * up * wrep).astype(BF16)
            part = jnp.dot(act, wd16[c * fc:(c + 1) * fc, :], preferred_element_type=F32)
            acc = part if acc is None else acc + part
        ys_ref[...] = acc.astype(ys_ref.dtype)

    @pl.when(jnp.logical_not(used))
    def _():
        ys_ref[...] = jnp.zeros_like(ys_ref)


def _experts(xs, ws, tables, w_gu_all, w_down_all, widx):
    rows, d = xs.shape
    f = w_down_all.shape[-2]
    tile = lambda w: pl.BlockSpec((MOE_ROW_TILE, w), lambda i, te, nu: (jnp.maximum(jnp.minimum(i, nu[0] - 1), 0), 0))
    grid_spec = pltpu.PrefetchScalarGridSpec(
        num_scalar_prefetch=2,
        grid=(rows // MOE_ROW_TILE,),
        in_specs=[tile(d), tile(ws.shape[1]),
                  pl.BlockSpec((None, None, d, 2 * f), lambda i, te, nu: (widx, te[i], 0, 0), pipeline_mode=pl.Buffered(1)),
                  pl.BlockSpec((None, None, f, d), lambda i, te, nu: (widx, te[i], 0, 0), pipeline_mode=pl.Buffered(1))],
        out_specs=pl.BlockSpec((MOE_ROW_TILE, d), lambda i, te, nu: (i, 0)),
        scratch_shapes=[pltpu.VMEM((d, 2 * f), BF16), pltpu.VMEM((f, d), BF16)],
    )
    return pl.pallas_call(
        _experts_body,
        out_shape=jax.ShapeDtypeStruct((rows, d), BF16),
        grid_spec=grid_spec,
        compiler_params=_cparams("arbitrary"),
        name="moe_experts",
    )(tables['tile_expert'], tables['n_used'], xs, ws, w_gu_all, w_down_all)


def _combine_body(seg_off_ref, seg_len_ref, dst_ref, ys_ref, posc_ref, h_ref, fg_ref, o_ref, ybuf, sems, *, ne, final_norm):
    t = pl.program_id(0)
    n_tiles = pl.num_programs(0)
    t_tokens, n_rows = h_ref.shape[0], ybuf.shape[1]

    def fetch(tile, act):
        slot = tile % 2

        def make_copy(off, row, size):
            return (pltpu.make_async_copy(ys_ref.at[pl.ds(row, size)], ybuf.at[slot, pl.ds(off, size)], sems.at[slot]),)

        _moe_chunk_copies(tile, ne, seg_off_ref, seg_len_ref, dst_ref, make_copy, act)

    @pl.when(t == 0)
    def _():
        ybuf[...] = jnp.zeros_like(ybuf)
        fetch(t, lambda cp: cp.start())

    @pl.when(t + 1 < n_tiles)
    def _():
        fetch(t + 1, lambda cp: cp.start())

    fetch(t, lambda cp: cp.wait())

    pos = posc_ref[...]
    lane_r = lax.broadcasted_iota(jnp.int32, (t_tokens, n_rows), 1)
    onehot = jnp.where(lane_r == pos[:, 0:1], 1.0, jnp.where(lane_r == pos[:, 1:2], 1.0, 0.0)).astype(BF16)
    out = h_ref[...] + jnp.dot(onehot, ybuf[t % 2], preferred_element_type=F32)
    o_ref[...] = _rms_norm_f32(out, fg_ref[...]) if final_norm else out


def _combine(ys, posc, h, tables, ne, final_g=None):
    n, d = h.shape
    tm = MOE_TOKEN_TILE
    final_norm = final_g is not None
    fg = final_g.reshape(1, d) if final_norm else jnp.ones((1, d), F32)
    grid_spec = pltpu.PrefetchScalarGridSpec(
        num_scalar_prefetch=3,
        grid=(n // tm,),
        in_specs=[pl.BlockSpec(memory_space=pl.ANY),
                  pl.BlockSpec((tm, posc.shape[1]), lambda t, *_: (t, 0)),
                  pl.BlockSpec((tm, d), lambda t, *_: (t, 0)),
                  pl.BlockSpec((1, d), lambda t, *_: (0, 0))],
        out_specs=pl.BlockSpec((tm, d), lambda t, *_: (t, 0)),
        scratch_shapes=[pltpu.VMEM((2, _moe_compact_rows(ne), d), BF16), pltpu.SemaphoreType.DMA((2,))],
    )
    return pl.pallas_call(
        functools.partial(_combine_body, ne=ne, final_norm=final_norm),
        out_shape=jax.ShapeDtypeStruct((n, d), F32),
        grid_spec=grid_spec,
        compiler_params=_cparams("arbitrary"),
        name="moe_combine",
    )(tables['seg_off'], tables['seg_len'], tables['dst'], ys, posc, h, fg)


def _moe(h, g_all, layer, wr_all, w_gu_all, w_down_all, widx, final_g=None):
    n = h.shape[0]
    ne = wr_all.shape[-1]
    xn, posr, wrow, posc, cnt = _router(h, g_all, layer, wr_all, widx)
    n_sorted_rows = _moe_sorted_rows(n, ne)
    tables = _moe_tables(cnt, n_sorted_rows)
    xs, ws = _dispatch(xn, posr, wrow, tables, n_sorted_rows, ne)
    ys = _experts(xs, ws, tables, w_gu_all, w_down_all, widx)
    return _combine(ys, posc, h, tables, ne, final_g)


def _final_norm_body(h_ref, g_ref, o_ref):
    o_ref[...] = _rms_norm_f32(h_ref[...], g_ref[...])


def _final_norm(h, g, tm=1024):
    n, d = h.shape
    return pl.pallas_call(
        _final_norm_body,
        out_shape=jax.ShapeDtypeStruct((n, d), F32),
        grid=(n // tm,),
        in_specs=[pl.BlockSpec((tm, d), lambda t: (t, 0)), pl.BlockSpec((1, d), lambda t: (0, 0))],
        out_specs=pl.BlockSpec((tm, d), lambda t: (t, 0)),
        compiler_params=_cparams("arbitrary"),
        name="final_norm",
    )(h, g.reshape(1, d))


def kernel(x, norm_mix_g, w_in, b_in, na_rpb, conv_w, conv_b, lru_wa, lru_ba, lru_wx, lru_bx, lru_lambda, w_branch, w_out, norm_ffn_g, ffn_w_gu, ffn_w_down, router_w, moe_w_gu, moe_w_down, final_g):
    batch, seq, d = x.shape
    depth = w_in.shape[0]
    n = batch * seq
    bw = w_branch.shape[2]
    h = x.reshape(n, d)

    tables = _na_bias_tables(na_rpb)
    twiddles = _fourier_twiddles(seq, bw)
    w_gate = _lru_gate_weights(lru_wa, lru_wx)
    wb16 = w_branch.astype(BF16)
    wo16 = w_out.astype(BF16)
    tn = 3 * bw

    for l in range(depth):
        q, k, v = _norm_proj(h, norm_mix_g, w_in, b_in, l, 0, 1, tn, 3)
        u_f, u_x, u_g = _norm_proj(h, norm_mix_g, w_in, b_in, l, 1, 1, tn, 3)
        (gates,) = _norm_proj(h, norm_mix_g, w_in, b_in, l, 2, (3 * d) // tn, tn, 1)
        y_a = _neighbourhood_attention(q, k, v, tables, l, batch)
        y_b = _fourier_mix(u_f, batch, twiddles)
        y_c = _recurrent_branch(u_x, u_g, conv_w, conv_b, w_gate, lru_ba, lru_bx, lru_lambda, l, batch)
        h = _merge(y_a, y_b, y_c, gates, h, wb16, wo16, l)
        if l % 2 == 0:
            h = _ffn(h, norm_ffn_g, l, ffn_w_gu, ffn_w_down, l // 2)
        else:
            h = _moe(h, norm_ffn_g, l, router_w, moe_w_gu, moe_w_down, l // 2,
                     final_g=final_g if l == depth - 1 else None)
    if depth % 2 == 1:
        h = _final_norm(h, final_g)
    return h.reshape(batch, seq, d)
```

```python
import functools

import numpy as np
import jax
import jax.numpy as jnp
from jax import lax
from jax.experimental import pallas as pl
from jax.experimental.pallas import tpu as pltpu

F32 = jnp.float32
BF16 = jnp.bfloat16

RMS_EPS = 1e-6
GRID_W = 64
NA_HEADS = 8
NA_HEAD_DIM = 64
NA_KH = 8
NA_KW = 16
NA_ROWS_PER_BLOCK = 8
NA_ROWS_IN_FLIGHT = 4
LRU_C = 8.0
CONV_W = 4
CONV_PAD_LEFT = 2
N_EXPERTS = 8
MASK_VALUE = -1e30

VMEM_LIMIT_BYTES = 56 * 1024 * 1024


def _cparams(*sem):
    return pltpu.CompilerParams(dimension_semantics=sem, vmem_limit_bytes=VMEM_LIMIT_BYTES)


def _rms_norm_f32(x, g):
    ms = jnp.mean(x * x, axis=-1, keepdims=True)
    return x * lax.rsqrt(ms + RMS_EPS) * g


def _sigmoid(x):
    return 1.0 / (1.0 + jnp.exp(-x))


def _norm_proj_body(x_ref, g_ref, w_ref, b_ref, *out_refs):
    xn = _rms_norm_f32(x_ref[...], g_ref[...]).astype(BF16)
    r = jnp.dot(xn, w_ref[...].astype(BF16), preferred_element_type=F32) + b_ref[...]
    width = r.shape[1] // len(out_refs)
    for i, o in enumerate(out_refs):
        o[...] = r[:, i * width:(i + 1) * width].astype(o.dtype)


def _norm_proj(h, g_all, w_all, b_all, layer, col_block0, n_col_blocks, tn, n_out, tm=1024):
    n, d = h.shape
    depth = w_all.shape[0]
    g3 = g_all.reshape(depth, 1, d)
    b3 = b_all.reshape(depth, 1, -1)
    width = tn // n_out
    out_shape = [jax.ShapeDtypeStruct((n, n_col_blocks * width), BF16) for _ in range(n_out)]
    return pl.pallas_call(
        _norm_proj_body,
        out_shape=out_shape,
        grid=(n_col_blocks, n // tm),
        in_specs=[
            pl.BlockSpec((tm, d), lambda c, t: (t, 0)),
            pl.BlockSpec((None, 1, d), lambda c, t: (layer, 0, 0)),
            pl.BlockSpec((None, d, tn), lambda c, t: (layer, 0, col_block0 + c)),
            pl.BlockSpec((None, 1, tn), lambda c, t: (layer, 0, col_block0 + c)),
        ],
        out_specs=[pl.BlockSpec((tm, width), lambda c, t: (t, c)) for _ in range(n_out)],
        compiler_params=_cparams("arbitrary", "arbitrary"),
        name="norm_proj",
    )(h, g3, w_all, b3)


def _na_bias_tables(rpb_all):
    cols = np.arange(GRID_W)
    col_start = np.clip(cols - NA_KW // 2, 0, GRID_W - NA_KW)
    cc = np.arange(GRID_W)[None, :]
    in_win = (cc >= col_start[:, None]) & (cc < col_start[:, None] + NA_KW)
    col_off = cc - cols[:, None] + (NA_KW - 1)
    onehot = np.zeros((2 * NA_KW - 1, GRID_W, GRID_W), np.float32)
    cq, ck = np.nonzero(in_win)
    onehot[col_off[cq, ck], cq, ck] = 1.0
    t = jnp.einsum('lhro,ocd->lhrcd', rpb_all.astype(F32), jnp.asarray(onehot),
                   precision=lax.Precision.HIGHEST)
    t = jnp.where(jnp.asarray(in_win)[None, None, None], t, MASK_VALUE)
    depth, heads, n_off = t.shape[:3]
    t = t.reshape(depth, heads // 2, 2, n_off, GRID_W, GRID_W)
    t = jnp.transpose(t, (0, 1, 3, 2, 4, 5)).reshape(depth, heads // 2, n_off, 2 * GRID_W, GRID_W)
    return jnp.concatenate([t[:, :, :-1], t[:, :, 1:]], axis=-1)


def _na_body(q_ref, kp_ref, kc_ref, kn_ref, vp_ref, vc_ref, vn_ref, tbl_ref, o_ref, kbuf, vbuf, s_scr, e_scr, *, n_blocks):
    j = pl.program_id(1)
    blk = NA_ROWS_PER_BLOCK * GRID_W
    kbuf[0:blk, :] = kp_ref[...]
    kbuf[blk:2 * blk, :] = kc_ref[...]
    kbuf[2 * blk:3 * blk, :] = kn_ref[...]
    vbuf[0:blk, :] = vp_ref[...]
    vbuf[blk:2 * blk, :] = vc_ref[...]
    vbuf[2 * blk:3 * blk, :] = vn_ref[...]
    band = NA_KH * GRID_W
    half = NA_KH // 2
    lane = lax.broadcasted_iota(jnp.int32, (GRID_W, 2 * NA_HEAD_DIM), 1)
    lo = lane < NA_HEAD_DIM
    qscale = NA_HEAD_DIM ** -0.5
    head_mask = (jnp.where(lo, qscale, 0.0).astype(BF16), jnp.where(lo, 0.0, qscale).astype(BF16))

    n_pairs = NA_HEADS // 2
    cols = [slice(p * 2 * NA_HEAD_DIM, (p + 1) * 2 * NA_HEAD_DIM) for p in range(n_pairs)]

    def row_offsets(rl):
        start_first = NA_ROWS_PER_BLOCK + jnp.maximum(rl - half, 0)
        start_last = NA_ROWS_PER_BLOCK + jnp.minimum(rl - half, 0)
        start = jnp.where(j == 0, start_first, jnp.where(j == n_blocks - 1, start_last, rl + half))
        delta = jnp.where(j == 0, jnp.minimum(rl, half), jnp.where(j == n_blocks - 1, jnp.maximum(rl, half), half))
        return pl.multiple_of(rl * GRID_W, GRID_W), pl.multiple_of(start * GRID_W, GRID_W), delta

    def scores(rl, slot):
        qrow, krow, delta = row_offsets(rl)
        for p in range(n_pairs):
            q2 = q_ref[pl.ds(qrow, GRID_W), cols[p]]
            qs = jnp.concatenate([q2 * head_mask[0], q2 * head_mask[1]], axis=0)
            k2 = kbuf[pl.ds(krow, band), cols[p]]
            s = lax.dot_general(qs, k2, (((1,), (1,)), ((), ())), preferred_element_type=F32)
            bias = jnp.concatenate([tbl_ref[p, 2 * m - delta + (NA_KH - 1)] for m in range(NA_KH // 2)], axis=1)
            s_scr[slot, p] = s + bias

    def softmax(slot):
        inv_l = []
        for p in range(n_pairs):
            s = s_scr[slot, p]
            m = jnp.max(s, axis=-1, keepdims=True)
            e = jnp.exp(s - m)
            inv_l.append(1.0 / jnp.sum(e, axis=-1, keepdims=True))
            e_scr[slot, p] = e.astype(BF16)
        return inv_l

    def weighted_values(rl, slot, inv_l):
        qrow, krow, _ = row_offsets(rl)
        for p in range(n_pairs):
            v2 = vbuf[pl.ds(krow, band), cols[p]]
            o = jnp.dot(e_scr[slot, p], v2, preferred_element_type=F32) * inv_l[p]
            o_ref[pl.ds(qrow, GRID_W), cols[p]] = jnp.where(lo, o[:GRID_W], o[GRID_W:]).astype(o_ref.dtype)

    def rows_body(it, carry):
        rows = [it * NA_ROWS_IN_FLIGHT + r for r in range(NA_ROWS_IN_FLIGHT)]
        for slot, rl in enumerate(rows):
            scores(rl, slot)
        inv = [softmax(slot) for slot in range(NA_ROWS_IN_FLIGHT)]
        for slot, rl in enumerate(rows):
            weighted_values(rl, slot, inv[slot])
        return carry

    lax.fori_loop(0, NA_ROWS_PER_BLOCK // NA_ROWS_IN_FLIGHT, rows_body, 0)


def _neighbourhood_attention(q, k, v, tables, layer, batch):
    n, width = q.shape
    n_pairs = NA_HEADS // 2
    blk = NA_ROWS_PER_BLOCK * GRID_W
    n_blocks = n // batch // blk
    assert n_blocks >= 2

    def tok(off):
        return lambda b, j: (b * n_blocks + jnp.clip(j + off, 0, n_blocks - 1), 0)

    tile = lambda off: pl.BlockSpec((blk, width), tok(off))
    return pl.pallas_call(
        functools.partial(_na_body, n_blocks=n_blocks),
        out_shape=jax.ShapeDtypeStruct((n, width), BF16),
        grid=(batch, n_blocks),
        in_specs=[tile(0), tile(-1), tile(0), tile(1), tile(-1), tile(0), tile(1),
                  pl.BlockSpec((None,) + tables.shape[1:], lambda b, j: (layer, 0, 0, 0, 0))],
        out_specs=tile(0),
        scratch_shapes=[pltpu.VMEM((3 * blk, width), BF16), pltpu.VMEM((3 * blk, width), BF16),
                        pltpu.VMEM((NA_ROWS_IN_FLIGHT, n_pairs, 2 * GRID_W, NA_KH * GRID_W), F32),
                        pltpu.VMEM((NA_ROWS_IN_FLIGHT, n_pairs, 2 * GRID_W, NA_KH * GRID_W), BF16)],
        compiler_params=_cparams("arbitrary", "arbitrary"),
        name="neigh_attn",
    )(q, k, k, k, v, v, v, tables)


FN_N2 = 128
FN_GROUP_DIM = 64
FN_K1_PER_STEP = 4


def _dft_cos_sin(n):
    ang = 2.0 * np.pi * (np.outer(np.arange(n), np.arange(n)) % n) / n
    return np.cos(ang), np.sin(ang)


def _fourier_stage1_body(x_ref, f_ref, tc_ref, ts_ref, zr_ref, zi_ref):
    n1 = x_ref.shape[0]
    z = jnp.dot(f_ref[...], x_ref[...], preferred_element_type=F32)
    zr, zi = z[:n1], z[n1:]
    tc, ts = tc_ref[...], ts_ref[...]
    zr_ref[...] = (zr * tc + zi * ts).astype(zr_ref.dtype)
    zi_ref[...] = (zi * tc - zr * ts).astype(zi_ref.dtype)


def _fourier_stage2_body(zr_ref, zi_ref, f_ref, c_ref, o_ref, *, scale):
    n2 = zr_ref.shape[1]
    width = zr_ref.shape[2]
    for i in range(zr_ref.shape[0]):
        z = jnp.concatenate([zr_ref[i], zi_ref[i]], axis=0)
        y = jnp.dot(f_ref[...], z, preferred_element_type=F32)
        yc = jnp.concatenate([y[:n2], y[n2:]], axis=1).astype(BF16)
        out = jnp.dot(yc, c_ref[...], preferred_element_type=F32) * scale
        o_ref[:, i * width:(i + 1) * width] = out.astype(o_ref.dtype)


def _fourier_mix(u, batch, twiddles):
    n, width = u.shape
    s = n // batch
    n2 = FN_N2
    n1 = s // n2
    tc, ts = twiddles
    c1, s1 = _dft_cos_sin(n1)
    f1 = jnp.asarray(np.concatenate([c1, -s1], axis=0), F32).astype(BF16)
    c2, s2 = _dft_cos_sin(n2)
    f2 = jnp.asarray(np.block([[c2, s2], [-s2, c2]]), F32).astype(BF16)
    cg, sg = _dft_cos_sin(FN_GROUP_DIM)
    eye = np.eye(width // FN_GROUP_DIM)
    fc = jnp.asarray(np.concatenate([np.kron(eye, cg), np.kron(eye, sg)], axis=0), F32).astype(BF16)

    x2 = u.reshape(batch, n1, n2 * width)
    tn = 16 * width
    zr, zi = pl.pallas_call(
        _fourier_stage1_body,
        out_shape=[jax.ShapeDtypeStruct((batch, n1, n2 * width), BF16)] * 2,
        grid=(batch, n2 * width // tn),
        in_specs=[pl.BlockSpec((None, n1, tn), lambda b, j: (b, 0, j)),
                  pl.BlockSpec((2 * n1, n1), lambda b, j: (0, 0)),
                  pl.BlockSpec((n1, tn), lambda b, j: (0, j)),
                  pl.BlockSpec((n1, tn), lambda b, j: (0, j))],
        out_specs=[pl.BlockSpec((None, n1, tn), lambda b, j: (b, 0, j))] * 2,
        compiler_params=_cparams("arbitrary", "arbitrary"),
        name="fourier_stage1",
    )(x2, f1, tc, ts)

    zr4 = zr.reshape(batch, n1, n2, width)
    zi4 = zi.reshape(batch, n1, n2, width)
    nk = FN_K1_PER_STEP
    out = pl.pallas_call(
        functools.partial(_fourier_stage2_body, scale=float(1.0 / np.sqrt(s * FN_GROUP_DIM))),
        out_shape=jax.ShapeDtypeStruct((batch, n2, n1 * width), BF16),
        grid=(batch, n1 // nk),
        in_specs=[pl.BlockSpec((None, nk, n2, width), lambda b, j: (b, j, 0, 0)),
                  pl.BlockSpec((None, nk, n2, width), lambda b, j: (b, j, 0, 0)),
                  pl.BlockSpec((2 * n2, 2 * n2), lambda b, j: (0, 0)),
                  pl.BlockSpec((2 * width, width), lambda b, j: (0, 0))],
        out_specs=pl.BlockSpec((None, n2, nk * width), lambda b, j: (b, 0, j)),
        compiler_params=_cparams("arbitrary", "arbitrary"),
        name="fourier_stage2",
    )(zr4, zi4, f2, fc)
    return out.reshape(n, width)


def _fourier_twiddles(s, width):
    n2 = FN_N2
    n1 = s // n2
    ang = (2.0 * np.pi / s) * (jnp.arange(n1, dtype=F32)[:, None] * jnp.arange(n2, dtype=F32)[None, :])
    tc = jnp.broadcast_to(jnp.cos(ang)[:, :, None], (n1, n2, width)).reshape(n1, n2 * width)
    ts = jnp.broadcast_to(jnp.sin(ang)[:, :, None], (n1, n2, width)).reshape(n1, n2 * width)
    return tc, ts


LRU_LANES = 128
LRU_SEGMENTS = 16
LRU_JCHUNK = 32
SUBLANES = 8


def _lru_gate_weights(wa_all, wx_all):
    depth, _, nb, db, _ = wa_all.shape
    ncol = nb // 2

    def blockdiag(w):
        w = w.reshape(depth, ncol, 2, db, db)
        z = jnp.zeros_like(w[:, :, 0])
        top = jnp.concatenate([w[:, :, 0], z], axis=-1)
        bot = jnp.concatenate([z, w[:, :, 1]], axis=-1)
        return jnp.concatenate([top, bot], axis=-2)

    parts = [blockdiag(wa_all[:, 0]), blockdiag(wx_all[:, 0]), blockdiag(wa_all[:, 1]), blockdiag(wx_all[:, 1])]
    return jnp.concatenate(parts, axis=-1).astype(BF16)


def _gelu_tanh(x):
    return 0.5 * x * (1.0 + jnp.tanh(np.sqrt(2.0 / np.pi) * (x + 0.044715 * (x * x * x))))


def _lru_body(ux_ref, cw_ref, cb_ref, w_ref, ba_ref, bx_ref, lam_ref, o_ref, h_scr, p_scr):
    n_j, n_g, lanes = ux_ref.shape
    jc = LRU_JCHUNK
    n_chunks = n_j // jc
    seg = lax.broadcasted_iota(jnp.int32, (n_g, lanes), 0)
    seg3 = lax.broadcasted_iota(jnp.int32, (jc, n_g, lanes), 1)
    row3 = lax.broadcasted_iota(jnp.int32, (jc, n_g, lanes), 0)

    def from_prev_segment(x):
        return jnp.where(seg >= 1, pltpu.roll(x, 1, axis=0), 0.0)

    def from_next_segment(x):
        return jnp.where(seg < n_g - 1, pltpu.roll(x, n_g - 1, axis=0), 0.0)

    def conv_chunk(j0):
        main = ux_ref[pl.ds(j0, jc)].astype(F32)
        lo_in = ux_ref[pl.ds(jnp.maximum(j0 - CONV_PAD_LEFT, 0), CONV_PAD_LEFT)].astype(F32)
        tail = ux_ref[n_j - CONV_PAD_LEFT:n_j].astype(F32)
        lo_wrap = jnp.stack([from_prev_segment(tail[r]) for r in range(CONV_PAD_LEFT)], axis=0)
        lo = jnp.where(j0 > 0, lo_in, lo_wrap)
        n_hi = CONV_W - 1 - CONV_PAD_LEFT
        hi_in = ux_ref[pl.ds(jnp.minimum(j0 + jc, n_j - n_hi), n_hi)].astype(F32)
        head = ux_ref[0:n_hi].astype(F32)
        hi_wrap = jnp.stack([from_next_segment(head[r]) for r in range(n_hi)], axis=0)
        hi = jnp.where(j0 + jc < n_j, hi_in, hi_wrap)
        ext = jnp.concatenate([lo, main, hi], axis=0)
        c = ext[0:jc] * cw_ref[0:1, :] + cb_ref[...]
        for tap in range(1, CONV_W):
            c = c + ext[tap:tap + jc] * cw_ref[tap:tap + 1, :]
        return c

    def gates(c, pre, d, j0):
        r = _sigmoid(pre[:, 2 * d * lanes:(2 * d + 1) * lanes] + ba_ref[d:d + 1, :])
        i = _sigmoid(pre[:, (2 * d + 1) * lanes:(2 * d + 2) * lanes] + bx_ref[d:d + 1, :])
        lam = lam_ref[d:d + 1, :]
        softplus = jnp.maximum(-lam, 0.0) + jnp.log(1.0 + jnp.exp(-jnp.abs(lam)))
        a = jnp.exp(-LRU_C * r * softplus)
        om = 1.0 - a * a
        mult = jnp.where(om > 0.0, om * lax.rsqrt(om), 0.0)
        a = a.reshape(jc, n_g, lanes)
        gain = (mult * i).reshape(jc, n_g, lanes)
        first = (seg3 == (n_g - 1) * d) & (row3 + j0 == (n_j - 1) * d)
        return a, jnp.where(first, i.reshape(jc, n_g, lanes), gain) * c

    def local_scan(a, b, carry, d):
        h, p = carry
        hs, ps = [None] * jc, [None] * jc
        for jj in (range(jc) if d == 0 else range(jc - 1, -1, -1)):
            h = a[jj] * h + b[jj]
            p = a[jj] * p
            hs[jj], ps[jj] = h, p
        return jnp.stack(hs, axis=0), jnp.stack(ps, axis=0), (h, p)

    scan_init = (jnp.zeros((n_g, lanes), F32), jnp.ones((n_g, lanes), F32))

    def gate_and_forward_body(ci, carry):
        j0 = ci * jc
        rows = pl.ds(j0, jc)
        c = conv_chunk(j0)
        pre = jnp.dot(c.reshape(jc * n_g, lanes).astype(BF16), w_ref[...], preferred_element_type=F32)
        a_b, b_b = gates(c, pre, 1, j0)
        p_scr[1, rows] = a_b
        h_scr[1, rows] = b_b
        a_f, b_f = gates(c, pre, 0, j0)
        h_scr[0, rows], p_scr[0, rows], carry = local_scan(a_f, b_f, carry, 0)
        return carry

    lax.fori_loop(0, n_chunks, gate_and_forward_body, scan_init)

    def backward_body(ci, carry):
        rows = pl.ds((n_chunks - 1 - ci) * jc, jc)
        h_scr[1, rows], p_scr[1, rows], carry = local_scan(p_scr[1, rows], h_scr[1, rows], carry, 1)
        return carry

    lax.fori_loop(0, n_chunks, backward_body, scan_init)

    def carry_in(d):
        edge = n_j - 1 if d == 0 else 0
        h_end, p_end = h_scr[d, edge], p_scr[d, edge]
        state = jnp.zeros((1, lanes), F32)
        out = jnp.zeros((n_g, lanes), F32)
        for g in (range(n_g) if d == 0 else range(n_g - 1, -1, -1)):
            out = jnp.where(seg == g, state, out)
            state = h_end[g:g + 1] + p_end[g:g + 1] * state
        return out

    e_fwd, e_bwd = carry_in(0), carry_in(1)

    def out_body(ci, carry):
        j0 = ci * jc
        rows = pl.ds(j0, jc)
        h = h_scr[0, rows] + p_scr[0, rows] * e_fwd + h_scr[1, rows] + p_scr[1, rows] * e_bwd
        o_ref[rows] = h.astype(o_ref.dtype)
        return carry

    lax.fori_loop(0, n_chunks, out_body, 0)


def _recurrent_branch(u_x, conv_w, conv_b, w_gate, ba, bx, lam, layer, batch):
    n, width = u_x.shape
    s = n // batch
    depth = conv_w.shape[0]
    ncol = width // LRU_LANES
    n_g = LRU_SEGMENTS
    n_j = s // n_g
    assert n_j % LRU_JCHUNK == 0
    ux_seg = jnp.transpose(u_x.reshape(batch, n_g, n_j, width), (0, 2, 1, 3))
    cb3 = conv_b.reshape(depth, 1, width)
    seq_spec = pl.BlockSpec((None, n_j, n_g, LRU_LANES), lambda b, c: (b, 0, 0, c))
    par = lambda rows: pl.BlockSpec((None, rows, LRU_LANES), lambda b, c: (layer, 0, c))
    state = pltpu.VMEM((2, n_j, n_g, LRU_LANES), F32)
    out = pl.pallas_call(
        _lru_body,
        out_shape=jax.ShapeDtypeStruct((batch, n_j, n_g, width), BF16),
        grid=(batch, ncol),
        in_specs=[seq_spec, par(CONV_W), par(1),
                  pl.BlockSpec((None, None, LRU_LANES, 4 * LRU_LANES), lambda b, c: (layer, c, 0, 0)),
                  par(2), par(2), par(2)],
        out_specs=seq_spec,
        scratch_shapes=[state, state],
        compiler_params=_cparams("arbitrary", "arbitrary"),
        name="rg_lru",
    )(ux_seg, conv_w, cb3, w_gate, ba, bx, lam)
    return jnp.transpose(out, (0, 2, 1, 3)).reshape(n, width)


def _merge_body(ya_ref, yb_ref, hc_ref, ug_ref, gt_ref, h_ref, wb_ref, wo_ref, o_ref):
    d = h_ref.shape[1]
    yc = (hc_ref[...].astype(F32) * _gelu_tanh(ug_ref[...].astype(F32))).astype(BF16)
    merged = None
    for kbr, y in enumerate((ya_ref[...], yb_ref[...], yc)):
        ybr = jnp.dot(y, wb_ref[kbr], preferred_element_type=F32)
        term = _sigmoid(gt_ref[:, kbr * d:(kbr + 1) * d].astype(F32)) * ybr
        merged = term if merged is None else merged + term
    o_ref[...] = h_ref[...] + jnp.dot(merged.astype(BF16), wo_ref[...], preferred_element_type=F32)


def _merge(ya, yb, hc, ug, gates, h, wb_all, wo_all, layer, tm=1024):
    n, d = h.shape
    bw = ya.shape[1]
    ytile = pl.BlockSpec((tm, bw), lambda t: (t, 0))
    return pl.pallas_call(
        _merge_body,
        out_shape=jax.ShapeDtypeStruct((n, d), F32),
        grid=(n // tm,),
        in_specs=[ytile, ytile, ytile, ytile,
                  pl.BlockSpec((tm, 3 * d), lambda t: (t, 0)),
                  pl.BlockSpec((tm, d), lambda t: (t, 0)),
                  pl.BlockSpec((None, 3, bw, d), lambda t: (layer, 0, 0, 0)),
                  pl.BlockSpec((None, d, d), lambda t: (layer, 0, 0))],
        out_specs=pl.BlockSpec((tm, d), lambda t: (t, 0)),
        compiler_params=_cparams("arbitrary"),
        name="branch_merge",
    )(ya, yb, hc, ug, gates, h, wb_all, wo_all)


def _ffn_body(h_ref, g_ref, wg_ref, wu_ref, wd_ref, o_ref, xn_ref, acc_ref):
    j = pl.program_id(1)

    @pl.when(j == 0)
    def _():
        h = h_ref[...]
        xn_ref[...] = _rms_norm_f32(h, g_ref[...]).astype(BF16)
        acc_ref[...] = h

    xn = xn_ref[...]
    gate = jnp.dot(xn, wg_ref[...].astype(BF16), preferred_element_type=F32)
    up = jnp.dot(xn, wu_ref[...].astype(BF16), preferred_element_type=F32)
    act = gate * _sigmoid(gate) * up
    acc_ref[...] += jnp.dot(act.astype(BF16), wd_ref[...].astype(BF16), preferred_element_type=F32)

    @pl.when(j == pl.num_programs(1) - 1)
    def _():
        o_ref[...] = acc_ref[...]


def _ffn(h, g_all, layer, w_gu_all, w_down_all, widx, tm=1024, tf=512):
    n, d = h.shape
    depth = g_all.shape[0]
    nf = w_down_all.shape[-2] // tf
    return pl.pallas_call(
        _ffn_body,
        out_shape=jax.ShapeDtypeStruct((n, d), F32),
        grid=(n // tm, nf),
        in_specs=[pl.BlockSpec((tm, d), lambda t, j: (t, 0)),
                  pl.BlockSpec((None, 1, d), lambda t, j: (layer, 0, 0)),
                  pl.BlockSpec((None, d, tf), lambda t, j: (widx, 0, j)),
                  pl.BlockSpec((None, d, tf), lambda t, j: (widx, 0, j + nf)),
                  pl.BlockSpec((None, tf, d), lambda t, j: (widx, j, 0))],
        out_specs=pl.BlockSpec((tm, d), lambda t, j: (t, 0)),
        scratch_shapes=[pltpu.VMEM((tm, d), BF16), pltpu.VMEM((tm, d), F32)],
        compiler_params=_cparams("arbitrary", "arbitrary"),
        name="dense_ffn",
    )(h, g_all.reshape(depth, 1, d), w_gu_all, w_gu_all, w_down_all)


MOE_TOKEN_TILE = 512
MOE_ROW_TILE = 512
MOE_SEG_ALIGN = 16
MOE_TOP_K = 2
MOE_CHUNK_SIZES = tuple(MOE_SEG_ALIGN << b for b in range(5, -1, -1))
assert MOE_CHUNK_SIZES[0] == MOE_TOKEN_TILE


def _moe_compact_rows(ne):
    rows = MOE_TOKEN_TILE * MOE_TOP_K + ne * (MOE_SEG_ALIGN - 1)
    return -(-rows // MOE_SEG_ALIGN) * MOE_SEG_ALIGN


def _moe_sorted_rows(n, ne):
    rows = n * MOE_TOP_K + (n // MOE_TOKEN_TILE) * ne * (MOE_SEG_ALIGN - 1) + ne * (MOE_ROW_TILE - MOE_SEG_ALIGN)
    return -(-rows // MOE_ROW_TILE) * MOE_ROW_TILE


def _router_body(h_ref, g_ref, wrt_ref, xn_ref, posr_ref, wrow_ref, posc_ref, cnt_ref, before_scr, eye_scr):
    t_tokens = h_ref.shape[0]

    @pl.when(pl.program_id(0) == 0)
    def _():
        r_i = lax.broadcasted_iota(jnp.int32, (t_tokens, t_tokens), 0)
        c_i = lax.broadcasted_iota(jnp.int32, (t_tokens, t_tokens), 1)
        before_scr[...] = jnp.where(r_i < c_i, 1.0, 0.0).astype(BF16)
        eye_scr[...] = jnp.where(r_i == c_i, 1.0, 0.0).astype(BF16)

    xn = _rms_norm_f32(h_ref[...], g_ref[...])
    xn_hi = xn.astype(BF16)
    xn_ref[...] = xn_hi
    nt_dims = (((1,), (1,)), ((), ()))
    xn_lo = (xn - xn_hi.astype(F32)).astype(BF16)
    w = wrt_ref[...]
    ne = w.shape[0]
    w_hi = w.astype(BF16)
    w_lo = (w - w_hi.astype(F32)).astype(BF16)
    by_hi = lax.dot_general(jnp.concatenate([w_hi, w_lo], axis=0), xn_hi, nt_dims, preferred_element_type=F32)
    logits = by_hi[:ne] + by_hi[ne:] + lax.dot_general(w_hi, xn_lo, nt_dims, preferred_element_type=F32)
    sub = lax.broadcasted_iota(jnp.int32, logits.shape, 0)
    m1 = jnp.max(logits, axis=0, keepdims=True)
    i1 = jnp.min(jnp.where(logits == m1, sub, ne), axis=0, keepdims=True)
    rest = jnp.where(sub == i1, -jnp.inf, logits)
    m2 = jnp.max(rest, axis=0, keepdims=True)
    i2 = jnp.min(jnp.where(rest == m2, sub, ne), axis=0, keepdims=True)
    e = jnp.exp(m2 - m1)
    wrow_ref[...] = jnp.concatenate([1.0 / (1.0 + e), e / (1.0 + e)], axis=0)

    sel1, sel2 = sub == i1, sub == i2
    memb = jnp.where(sel1, 1.0, jnp.where(sel2, 1.0, 0.0))
    rank = jnp.dot(memb.astype(BF16), before_scr[...], preferred_element_type=F32)
    sub_c = lax.broadcasted_iota(jnp.int32, cnt_ref.shape, 0)
    cnt_out = jnp.zeros(cnt_ref.shape, F32)
    base = rank
    running = jnp.zeros((1, 1), F32)
    for ex in range(ne):
        c = jnp.sum(memb[ex:ex + 1, :], axis=1, keepdims=True)
        cnt_out = jnp.where(sub_c == ex, c, cnt_out)
        base = jnp.where(sub == ex, base + running, base)
        running = running + jnp.floor((c + (MOE_SEG_ALIGN - 1)) * (1.0 / MOE_SEG_ALIGN)) * MOE_SEG_ALIGN
    cnt_ref[...] = cnt_out.astype(jnp.int32)
    pos1 = jnp.sum(jnp.where(sel1, base, 0.0), axis=0, keepdims=True)
    pos2 = jnp.sum(jnp.where(sel2, base, 0.0), axis=0, keepdims=True)
    posr_ref[...] = jnp.concatenate([pos1, pos2], axis=0).astype(jnp.int32)
    digits = []
    for pos in (pos1, pos2):
        hi = jnp.floor(pos * (1.0 / 128.0))
        digits += [hi, pos - 128.0 * hi]
    pad = jnp.zeros((2 * SUBLANES - len(digits), t_tokens), F32)
    dig_t = lax.dot_general(eye_scr[...], jnp.concatenate(digits + [pad], axis=0).astype(BF16), nt_dims,
                            preferred_element_type=F32)
    d_i = lax.broadcasted_iota(jnp.int32, (2 * SUBLANES, posc_ref.shape[1]), 0)
    c_i = lax.broadcasted_iota(jnp.int32, (2 * SUBLANES, posc_ref.shape[1]), 1)
    recombine = jnp.where(d_i == 2 * c_i, 128.0, jnp.where(d_i == 2 * c_i + 1, 1.0, 0.0)).astype(BF16)
    posc = jnp.dot(dig_t.astype(BF16), recombine, preferred_element_type=F32)
    posc_ref[...] = posc.astype(jnp.int32)


def _router(h, g_all, layer, wr_all, widx):
    n, d = h.shape
    depth = g_all.shape[0]
    ne = wr_all.shape[-1]
    tm = MOE_TOKEN_TILE
    nt = n // tm
    wrt = jnp.swapaxes(wr_all, 1, 2)
    return pl.pallas_call(
        _router_body,
        out_shape=[jax.ShapeDtypeStruct((n, d), BF16),
                   jax.ShapeDtypeStruct((MOE_TOP_K, n), jnp.int32),
                   jax.ShapeDtypeStruct((MOE_TOP_K, n), F32),
                   jax.ShapeDtypeStruct((n, SUBLANES), jnp.int32),
                   jax.ShapeDtypeStruct((nt, ne, 128), jnp.int32)],
        grid=(nt,),
        in_specs=[pl.BlockSpec((tm, d), lambda t: (t, 0)),
                  pl.BlockSpec((None, 1, d), lambda t: (layer, 0, 0)),
                  pl.BlockSpec((None, ne, d), lambda t: (widx, 0, 0))],
        out_specs=[pl.BlockSpec((tm, d), lambda t: (t, 0)),
                   pl.BlockSpec((MOE_TOP_K, tm), lambda t: (0, t)),
                   pl.BlockSpec((MOE_TOP_K, tm), lambda t: (0, t)),
                   pl.BlockSpec((tm, SUBLANES), lambda t: (t, 0)),
                   pl.BlockSpec((None, ne, 128), lambda t: (t, 0, 0))],
        scratch_shapes=[pltpu.VMEM((tm, tm), BF16), pltpu.VMEM((tm, tm), BF16)],
        compiler_params=_cparams("arbitrary"),
        name="router",
    )(h, g_all.reshape(depth, 1, d), wrt)


def _moe_tables(cnt, n_sorted_rows):
    cnt = cnt[:, :, 0]
    nt, ne = cnt.shape
    seg = (cnt + (MOE_SEG_ALIGN - 1)) // MOE_SEG_ALIGN * MOE_SEG_ALIGN
    seg_off = jnp.cumsum(seg, axis=1) - seg
    e_rows = jnp.sum(seg, axis=0)
    e_tiles = (e_rows + (MOE_ROW_TILE - 1)) // MOE_ROW_TILE
    e_cum = jnp.cumsum(e_tiles)
    e_base = (e_cum - e_tiles) * MOE_ROW_TILE
    dst = e_base[None, :] + jnp.cumsum(seg, axis=0) - seg
    n_used = e_cum[-1]
    tile_ids = jnp.minimum(jnp.arange(n_sorted_rows // MOE_ROW_TILE, dtype=jnp.int32), n_used - 1)
    tile_expert = jnp.sum(tile_ids[:, None] >= e_cum[None, :], axis=1)
    last_tile_row = e_base + (e_tiles - 1) * MOE_ROW_TILE
    i32 = lambda a: a.astype(jnp.int32)
    return dict(seg_off=i32(seg_off.reshape(-1)), seg_len=i32(seg.reshape(-1)), dst=i32(dst.reshape(-1)),
                n_used=i32(n_used.reshape(1)), tile_expert=i32(tile_expert),
                last_tile_row=i32(last_tile_row), has_rows=i32(e_tiles > 0))


def _moe_chunk_copies(t, ne, seg_off_ref, seg_len_ref, dst_ref, make_copy, act):
    for ex in range(ne):
        idx = t * ne + ex
        off = seg_off_ref[idx]
        ln = seg_len_ref[idx]
        row = dst_ref[idx]
        for size in MOE_CHUNK_SIZES:
            take = (ln & size) != 0

            @pl.when(take)
            def _(off=off, row=row, size=size):
                for cp in make_copy(pl.multiple_of(off, MOE_SEG_ALIGN), pl.multiple_of(row, MOE_SEG_ALIGN), size):
                    act(cp)

            step = jnp.where(take, size, 0)
            off = off + step
            row = row + step


def _dispatch_body(seg_off_ref, seg_len_ref, dst_ref, last_row_ref, has_rows_ref, n_used_ref,
                   xn_ref, posr_ref, wrow_ref, xs_ref, ws_ref, cbuf, wbuf, zx, zw, sems, zsem):
    t = pl.program_id(0)
    n_tiles = pl.num_programs(0)
    ne = last_row_ref.shape[0]
    n_rows, t_tokens = cbuf.shape[1], xn_ref.shape[0]

    def zero_copies(row):
        row = pl.multiple_of(row, MOE_ROW_TILE)
        return (pltpu.make_async_copy(zx, xs_ref.at[pl.ds(row, MOE_ROW_TILE)], zsem),
                pltpu.make_async_copy(zw, ws_ref.at[pl.ds(row, MOE_ROW_TILE)], zsem))

    @pl.when(t == 0)
    def _():
        zx[...] = jnp.zeros_like(zx)
        zw[...] = jnp.zeros_like(zw)
        for act in (lambda cp: cp.start(), lambda cp: cp.wait()):
            for ex in range(ne):
                @pl.when(has_rows_ref[ex] != 0)
                def _(ex=ex):
                    for cp in zero_copies(last_row_ref[ex]):
                        act(cp)

        def tail_body(i, carry):
            for cp in zero_copies(i * MOE_ROW_TILE):
                cp.start()
                cp.wait()
            return carry

        lax.fori_loop(n_used_ref[0], xs_ref.shape[0] // MOE_ROW_TILE, tail_body, 0)

    r_iota = lax.broadcasted_iota(jnp.int32, (n_rows, t_tokens), 0)
    hit1 = r_iota == posr_ref[0:1, :]
    hit2 = r_iota == posr_ref[1:2, :]
    onehot = jnp.where(hit1, 1.0, jnp.where(hit2, 1.0, 0.0)).astype(BF16)
    slot = t % 2
    cbuf[slot] = jnp.dot(onehot, xn_ref[...], preferred_element_type=F32).astype(BF16)
    wsel = jnp.where(hit1, wrow_ref[0:1, :], jnp.where(hit2, wrow_ref[1:2, :], 0.0))
    wbuf[slot] = jnp.broadcast_to(jnp.sum(wsel, axis=1, keepdims=True), wbuf.shape[1:])

    def drain(tile, act):
        s = tile % 2

        def make_copy(off, row, size):
            return (pltpu.make_async_copy(cbuf.at[s, pl.ds(off, size)], xs_ref.at[pl.ds(row, size)], sems.at[s]),
                    pltpu.make_async_copy(wbuf.at[s, pl.ds(off, size)], ws_ref.at[pl.ds(row, size)], sems.at[s]))

        _moe_chunk_copies(tile, ne, seg_off_ref, seg_len_ref, dst_ref, make_copy, act)

    drain(t, lambda cp: cp.start())

    @pl.when(t > 0)
    def _():
        drain(t - 1, lambda cp: cp.wait())

    @pl.when(t == n_tiles - 1)
    def _():
        drain(t, lambda cp: cp.wait())


def _dispatch(xn, posr, wrow, tables, n_sorted_rows, ne):
    n, d = xn.shape
    tm = MOE_TOKEN_TILE
    n_rows = _moe_compact_rows(ne)
    grid_spec = pltpu.PrefetchScalarGridSpec(
        num_scalar_prefetch=6,
        grid=(n // tm,),
        in_specs=[pl.BlockSpec((tm, d), lambda t, *_: (t, 0)),
                  pl.BlockSpec((MOE_TOP_K, tm), lambda t, *_: (0, t)),
                  pl.BlockSpec((MOE_TOP_K, tm), lambda t, *_: (0, t))],
        out_specs=[pl.BlockSpec(memory_space=pl.ANY), pl.BlockSpec(memory_space=pl.ANY)],
        scratch_shapes=[pltpu.VMEM((2, n_rows, d), BF16), pltpu.VMEM((2, n_rows, 128), F32),
                        pltpu.VMEM((MOE_ROW_TILE, d), BF16), pltpu.VMEM((MOE_ROW_TILE, 128), F32),
                        pltpu.SemaphoreType.DMA((2,)), pltpu.SemaphoreType.DMA],
    )
    return pl.pallas_call(
        _dispatch_body,
        out_shape=[jax.ShapeDtypeStruct((n_sorted_rows, d), BF16), jax.ShapeDtypeStruct((n_sorted_rows, 128), F32)],
        grid_spec=grid_spec,
        compiler_params=_cparams("arbitrary"),
        name="moe_dispatch",
    )(tables['seg_off'], tables['seg_len'], tables['dst'], tables['last_tile_row'], tables['has_rows'], tables['n_used'],
      xn, posr, wrow)


def _experts_body(tile_expert_ref, n_used_ref, xs_ref, ws_ref, wgu_ref, wd_ref, ys_ref, wgu16, wd16):
    i = pl.program_id(0)
    used = i < n_used_ref[0]
    prev = tile_expert_ref[jnp.maximum(i - 1, 0)]
    new_expert = jnp.logical_or(i == 0, tile_expert_ref[i] != prev)
    f = wd_ref.shape[0]
    fc = 512

    @pl.when(jnp.logical_and(used, new_expert))
    def _():
        for c in range(2 * f // fc):
            wgu16[:, c * fc:(c + 1) * fc] = wgu_ref[:, c * fc:(c + 1) * fc].astype(BF16)
        for c in range(f // fc):
            wd16[c * fc:(c + 1) * fc, :] = wd_ref[c * fc:(c + 1) * fc, :].astype(BF16)

    @pl.when(used)
    def _():
        x = xs_ref[...]
        w = ws_ref[...]
        wrep = jnp.concatenate([w] * (fc // w.shape[1]), axis=1)
        acc = None
        for c in range(f // fc):
            gate = jnp.dot(x, wgu16[:, c * fc:(c + 1) * fc], preferred_element_type=F32)
            up = jnp.dot(x, wgu16[:, f + c * fc:f + (c + 1) * fc], preferred_element_type=F32)
            act = (gate * _sigmoid(gate) * up * wrep).astype(BF16)
            part = jnp.dot(act, wd16[c * fc:(c + 1) * fc, :], preferred_element_type=F32)
            acc = part if acc is None else acc + part
        ys_ref[...] = acc.astype(ys_ref.dtype)

    @pl.when(jnp.logical_not(used))
    def _():
        ys_ref[...] = jnp.zeros_like(ys_ref)


def _experts(xs, ws, tables, w_gu_all, w_down_all, widx):
    rows, d = xs.shape
    f = w_down_all.shape[-2]
    tile = lambda w: pl.BlockSpec((MOE_ROW_TILE, w), lambda i, te, nu: (jnp.maximum(jnp.minimum(i, nu[0] - 1), 0), 0))
    grid_spec = pltpu.PrefetchScalarGridSpec(
        num_scalar_prefetch=2,
        grid=(rows // MOE_ROW_TILE,),
        in_specs=[tile(d), tile(ws.shape[1]),
                  pl.BlockSpec((None, None, d, 2 * f), lambda i, te, nu: (widx, te[i], 0, 0)),
                  pl.BlockSpec((None, None, f, d), lambda i, te, nu: (widx, te[i], 0, 0), pipeline_mode=pl.Buffered(1))],
        out_specs=pl.BlockSpec((MOE_ROW_TILE, d), lambda i, te, nu: (i, 0)),
        scratch_shapes=[pltpu.VMEM((d, 2 * f), BF16), pltpu.VMEM((f, d), BF16)],
    )
    return pl.pallas_call(
        _experts_body,
        out_shape=jax.ShapeDtypeStruct((rows, d), BF16),
        grid_spec=grid_spec,
        compiler_params=_cparams("arbitrary"),
        name="moe_experts",
    )(tables['tile_expert'], tables['n_used'], xs, ws, w_gu_all, w_down_all)


def _combine_body(seg_off_ref, seg_len_ref, dst_ref, ys_ref, posc_ref, h_ref, fg_ref, o_ref, ybuf, sems, *, ne, final_norm):
    t = pl.program_id(0)
    n_tiles = pl.num_programs(0)
    t_tokens, n_rows = h_ref.shape[0], ybuf.shape[1]

    def fetch(tile, act):
        slot = tile % 2

        def make_copy(off, row, size):
            return (pltpu.make_async_copy(ys_ref.at[pl.ds(row, size)], ybuf.at[slot, pl.ds(off, size)], sems.at[slot]),)

        _moe_chunk_copies(tile, ne, seg_off_ref, seg_len_ref, dst_ref, make_copy, act)

    @pl.when(t == 0)
    def _():
        ybuf[...] = jnp.zeros_like(ybuf)
        fetch(t, lambda cp: cp.start())

    @pl.when(t + 1 < n_tiles)
    def _():
        fetch(t + 1, lambda cp: cp.start())

    fetch(t, lambda cp: cp.wait())

    pos = posc_ref[...]
    lane_r = lax.broadcasted_iota(jnp.int32, (t_tokens, n_rows), 1)
    onehot = jnp.where(lane_r == pos[:, 0:1], 1.0, jnp.where(lane_r == pos[:, 1:2], 1.0, 0.0)).astype(BF16)
    out = h_ref[...] + jnp.dot(onehot, ybuf[t % 2], preferred_element_type=F32)
    o_ref[...] = _rms_norm_f32(out, fg_ref[...]) if final_norm else out


def _combine(ys, posc, h, tables, ne, final_g=None):
    n, d = h.shape
    tm = MOE_TOKEN_TILE
    final_norm = final_g is not None
    fg = final_g.reshape(1, d) if final_norm else jnp.ones((1, d), F32)
    grid_spec = pltpu.PrefetchScalarGridSpec(
        num_scalar_prefetch=3,
        grid=(n // tm,),
        in_specs=[pl.BlockSpec(memory_space=pl.ANY),
                  pl.BlockSpec((tm, posc.shape[1]), lambda t, *_: (t, 0)),
                  pl.BlockSpec((tm, d), lambda t, *_: (t, 0)),
                  pl.BlockSpec((1, d), lambda t, *_: (0, 0))],
        out_specs=pl.BlockSpec((tm, d), lambda t, *_: (t, 0)),
        scratch_shapes=[pltpu.VMEM((2, _moe_compact_rows(ne), d), BF16), pltpu.SemaphoreType.DMA((2,))],
    )
    return pl.pallas_call(
        functools.partial(_combine_body, ne=ne, final_norm=final_norm),
        out_shape=jax.ShapeDtypeStruct((n, d), F32),
        grid_spec=grid_spec,
        compiler_params=_cparams("arbitrary"),
        name="moe_combine",
    )(tables['seg_off'], tables['seg_len'], tables['dst'], ys, posc, h, fg)


def _moe(h, g_all, layer, wr_all, w_gu_all, w_down_all, widx, final_g=None):
    n = h.shape[0]
    ne = wr_all.shape[-1]
    xn, posr, wrow, posc, cnt = _router(h, g_all, layer, wr_all, widx)
    n_sorted_rows = _moe_sorted_rows(n, ne)
    tables = _moe_tables(cnt, n_sorted_rows)
    xs, ws = _dispatch(xn, posr, wrow, tables, n_sorted_rows, ne)
    ys = _experts(xs, ws, tables, w_gu_all, w_down_all, widx)
    return _combine(ys, posc, h, tables, ne, final_g)


def _final_norm_body(h_ref, g_ref, o_ref):
    o_ref[...] = _rms_norm_f32(h_ref[...], g_ref[...])


def _final_norm(h, g, tm=1024):
    n, d = h.shape
    return pl.pallas_call(
        _final_norm_body,
        out_shape=jax.ShapeDtypeStruct((n, d), F32),
        grid=(n // tm,),
        in_specs=[pl.BlockSpec((tm, d), lambda t: (t, 0)), pl.BlockSpec((1, d), lambda t: (0, 0))],
        out_specs=pl.BlockSpec((tm, d), lambda t: (t, 0)),
        compiler_params=_cparams("arbitrary"),
        name="final_norm",
    )(h, g.reshape(1, d))


def kernel(x, norm_mix_g, w_in, b_in, na_rpb, conv_w, conv_b, lru_wa, lru_ba, lru_wx, lru_bx, lru_lambda, w_branch, w_out, norm_ffn_g, ffn_w_gu, ffn_w_down, router_w, moe_w_gu, moe_w_down, final_g):
    batch, seq, d = x.shape
    depth = w_in.shape[0]
    n = batch * seq
    bw = w_branch.shape[2]
    h = x.reshape(n, d)

    tables = _na_bias_tables(na_rpb)
    twiddles = _fourier_twiddles(seq, bw)
    w_gate = _lru_gate_weights(lru_wa, lru_wx)
    wb16 = w_branch.astype(BF16)
    wo16 = w_out.astype(BF16)
    tn = 3 * bw

    for l in range(depth):
        q, k, v = _norm_proj(h, norm_mix_g, w_in, b_in, l, 0, 1, tn, 3)
        u_f, u_x, u_g = _norm_proj(h, norm_mix_g, w_in, b_in, l, 1, 1, tn, 3)
        (gates,) = _norm_proj(h, norm_mix_g, w_in, b_in, l, 2, (3 * d) // tn, tn, 1)
        y_a = _neighbourhood_attention(q, k, v, tables, l, batch)
        y_b = _fourier_mix(u_f, batch, twiddles)
        h_c = _recurrent_branch(u_x, conv_w, conv_b, w_gate, lru_ba, lru_bx, lru_lambda, l, batch)
        h = _merge(y_a, y_b, h_c, u_g, gates, h, wb16, wo16, l)
        if l % 2 == 0:
            h = _ffn(h, norm_ffn_g, l, ffn_w_gu, ffn_w_down, l // 2)
        else:
            h = _moe(h, norm_ffn_g, l, router_w, moe_w_gu, moe_w_down, l // 2,
                     final_g=final_g if l == depth - 1 else None)
    if depth % 2 == 1:
        h = _final_norm(h, final_g)
    return h.reshape(batch, seq, d)
```

```python
import functools

import numpy as np
import jax
import jax.numpy as jnp
from jax import lax
from jax.experimental import pallas as pl
from jax.experimental.pallas import tpu as pltpu

F32 = jnp.float32
BF16 = jnp.bfloat16

RMS_EPS = 1e-6
GRID_W = 64
NA_HEADS = 8
NA_HEAD_DIM = 64
NA_KH = 8
NA_KW = 16
NA_ROWS_PER_BLOCK = 8
NA_ROWS_IN_FLIGHT = 4
LRU_C = 8.0
CONV_W = 4
CONV_PAD_LEFT = 2
N_EXPERTS = 8
MASK_VALUE = -1e30

VMEM_LIMIT_BYTES = 56 * 1024 * 1024


def _cparams(*sem):
    return pltpu.CompilerParams(dimension_semantics=sem, vmem_limit_bytes=VMEM_LIMIT_BYTES)


def _rms_norm_f32(x, g):
    ms = jnp.mean(x * x, axis=-1, keepdims=True)
    return x * lax.rsqrt(ms + RMS_EPS) * g


def _sigmoid(x):
    return 1.0 / (1.0 + jnp.exp(-x))


def _norm_proj_body(x_ref, g_ref, w_ref, b_ref, *out_refs):
    xn = _rms_norm_f32(x_ref[...], g_ref[...]).astype(BF16)
    r = jnp.dot(xn, w_ref[...].astype(BF16), preferred_element_type=F32) + b_ref[...]
    width = r.shape[1] // len(out_refs)
    for i, o in enumerate(out_refs):
        o[...] = r[:, i * width:(i + 1) * width].astype(o.dtype)


def _norm_proj(h, g_all, w_all, b_all, layer, col_block0, n_col_blocks, tn, n_out, tm=1024):
    n, d = h.shape
    depth = w_all.shape[0]
    g3 = g_all.reshape(depth, 1, d)
    b3 = b_all.reshape(depth, 1, -1)
    width = tn // n_out
    out_shape = [jax.ShapeDtypeStruct((n, n_col_blocks * width), BF16) for _ in range(n_out)]
    return pl.pallas_call(
        _norm_proj_body,
        out_shape=out_shape,
        grid=(n_col_blocks, n // tm),
        in_specs=[
            pl.BlockSpec((tm, d), lambda c, t: (t, 0)),
            pl.BlockSpec((None, 1, d), lambda c, t: (layer, 0, 0)),
            pl.BlockSpec((None, d, tn), lambda c, t: (layer, 0, col_block0 + c)),
            pl.BlockSpec((None, 1, tn), lambda c, t: (layer, 0, col_block0 + c)),
        ],
        out_specs=[pl.BlockSpec((tm, width), lambda c, t: (t, c)) for _ in range(n_out)],
        compiler_params=_cparams("arbitrary", "arbitrary"),
        name="norm_proj",
    )(h, g3, w_all, b3)


def _na_bias_tables(rpb_all):
    cols = np.arange(GRID_W)
    col_start = np.clip(cols - NA_KW // 2, 0, GRID_W - NA_KW)
    cc = np.arange(GRID_W)[None, :]
    in_win = (cc >= col_start[:, None]) & (cc < col_start[:, None] + NA_KW)
    col_off = cc - cols[:, None] + (NA_KW - 1)
    onehot = np.zeros((2 * NA_KW - 1, GRID_W, GRID_W), np.float32)
    cq, ck = np.nonzero(in_win)
    onehot[col_off[cq, ck], cq, ck] = 1.0
    t = jnp.einsum('lhro,ocd->lhrcd', rpb_all.astype(F32), jnp.asarray(onehot),
                   precision=lax.Precision.HIGHEST)
    t = jnp.where(jnp.asarray(in_win)[None, None, None], t, MASK_VALUE)
    depth, heads, n_off = t.shape[:3]
    t = t.reshape(depth, heads // 2, 2, n_off, GRID_W, GRID_W)
    t = jnp.transpose(t, (0, 1, 3, 2, 4, 5)).reshape(depth, heads // 2, n_off, 2 * GRID_W, GRID_W)
    return jnp.concatenate([t[:, :, :-1], t[:, :, 1:]], axis=-1)


def _na_body(q_ref, kp_ref, kc_ref, kn_ref, vp_ref, vc_ref, vn_ref, tbl_ref, o_ref, kbuf, vbuf, s_scr, e_scr, *, n_blocks):
    j = pl.program_id(1)
    blk = NA_ROWS_PER_BLOCK * GRID_W
    kbuf[0:blk, :] = kp_ref[...]
    kbuf[blk:2 * blk, :] = kc_ref[...]
    kbuf[2 * blk:3 * blk, :] = kn_ref[...]
    vbuf[0:blk, :] = vp_ref[...]
    vbuf[blk:2 * blk, :] = vc_ref[...]
    vbuf[2 * blk:3 * blk, :] = vn_ref[...]
    band = NA_KH * GRID_W
    half = NA_KH // 2
    lane = lax.broadcasted_iota(jnp.int32, (GRID_W, 2 * NA_HEAD_DIM), 1)
    lo = lane < NA_HEAD_DIM
    qscale = NA_HEAD_DIM ** -0.5
    head_mask = (jnp.where(lo, qscale, 0.0).astype(BF16), jnp.where(lo, 0.0, qscale).astype(BF16))

    n_pairs = NA_HEADS // 2
    cols = [slice(p * 2 * NA_HEAD_DIM, (p + 1) * 2 * NA_HEAD_DIM) for p in range(n_pairs)]

    def row_offsets(rl):
        start_first = NA_ROWS_PER_BLOCK + jnp.maximum(rl - half, 0)
        start_last = NA_ROWS_PER_BLOCK + jnp.minimum(rl - half, 0)
        start = jnp.where(j == 0, start_first, jnp.where(j == n_blocks - 1, start_last, rl + half))
        delta = jnp.where(j == 0, jnp.minimum(rl, half), jnp.where(j == n_blocks - 1, jnp.maximum(rl, half), half))
        return pl.multiple_of(rl * GRID_W, GRID_W), pl.multiple_of(start * GRID_W, GRID_W), delta

    def scores(rl, slot):
        qrow, krow, delta = row_offsets(rl)
        for p in range(n_pairs):
            q2 = q_ref[pl.ds(qrow, GRID_W), cols[p]]
            qs = jnp.concatenate([q2 * head_mask[0], q2 * head_mask[1]], axis=0)
            k2 = kbuf[pl.ds(krow, band), cols[p]]
            s = lax.dot_general(qs, k2, (((1,), (1,)), ((), ())), preferred_element_type=F32)
            bias = jnp.concatenate([tbl_ref[p, 2 * m - delta + (NA_KH - 1)] for m in range(NA_KH // 2)], axis=1)
            s_scr[slot, p] = s + bias

    def softmax(slot):
        inv_l = []
        for p in range(n_pairs):
            s = s_scr[slot, p]
            m = jnp.max(s, axis=-1, keepdims=True)
            e = jnp.exp(s - m)
            inv_l.append(1.0 / jnp.sum(e, axis=-1, keepdims=True))
            e_scr[slot, p] = e.astype(BF16)
        return inv_l

    def weighted_values(rl, slot, inv_l):
        qrow, krow, _ = row_offsets(rl)
        for p in range(n_pairs):
            v2 = vbuf[pl.ds(krow, band), cols[p]]
            o = jnp.dot(e_scr[slot, p], v2, preferred_element_type=F32) * inv_l[p]
            o_ref[pl.ds(qrow, GRID_W), cols[p]] = jnp.where(lo, o[:GRID_W], o[GRID_W:]).astype(o_ref.dtype)

    def rows_body(it, carry):
        rows = [it * NA_ROWS_IN_FLIGHT + r for r in range(NA_ROWS_IN_FLIGHT)]
        for slot, rl in enumerate(rows):
            scores(rl, slot)
        inv = [softmax(slot) for slot in range(NA_ROWS_IN_FLIGHT)]
        for slot, rl in enumerate(rows):
            weighted_values(rl, slot, inv[slot])
        return carry

    lax.fori_loop(0, NA_ROWS_PER_BLOCK // NA_ROWS_IN_FLIGHT, rows_body, 0)


def _neighbourhood_attention(q, k, v, tables, layer, batch):
    n, width = q.shape
    n_pairs = NA_HEADS // 2
    blk = NA_ROWS_PER_BLOCK * GRID_W
    n_blocks = n // batch // blk
    assert n_blocks >= 2

    def tok(off):
        return lambda b, j: (b * n_blocks + jnp.clip(j + off, 0, n_blocks - 1), 0)

    tile = lambda off: pl.BlockSpec((blk, width), tok(off))
    return pl.pallas_call(
        functools.partial(_na_body, n_blocks=n_blocks),
        out_shape=jax.ShapeDtypeStruct((n, width), BF16),
        grid=(batch, n_blocks),
        in_specs=[tile(0), tile(-1), tile(0), tile(1), tile(-1), tile(0), tile(1),
                  pl.BlockSpec((None,) + tables.shape[1:], lambda b, j: (layer, 0, 0, 0, 0))],
        out_specs=tile(0),
        scratch_shapes=[pltpu.VMEM((3 * blk, width), BF16), pltpu.VMEM((3 * blk, width), BF16),
                        pltpu.VMEM((NA_ROWS_IN_FLIGHT, n_pairs, 2 * GRID_W, NA_KH * GRID_W), F32),
                        pltpu.VMEM((NA_ROWS_IN_FLIGHT, n_pairs, 2 * GRID_W, NA_KH * GRID_W), BF16)],
        compiler_params=_cparams("arbitrary", "arbitrary"),
        name="neigh_attn",
    )(q, k, k, k, v, v, v, tables)


FN_N2 = 128
FN_GROUP_DIM = 64
FN_GROUP = 16


def _dft_cos_sin(n):
    ang = 2.0 * np.pi * (np.outer(np.arange(n), np.arange(n)) % n) / n
    return np.cos(ang), np.sin(ang)


def _fourier_stage1_body(x_ref, f_ref, tc_ref, ts_ref, zr_ref, zi_ref):
    n1, n_b, width = x_ref.shape
    xs = jnp.swapaxes(x_ref[...], 0, 1)
    rep = width // tc_ref.shape[2]
    zr_all, zi_all = [], []
    for b in range(n_b):
        z = jnp.dot(f_ref[...], xs[b], preferred_element_type=F32)
        zr, zi = z[:n1], z[n1:]
        tc = jnp.concatenate([tc_ref[b]] * rep, axis=1)
        ts = jnp.concatenate([ts_ref[b]] * rep, axis=1)
        zr_all.append((zr * tc + zi * ts).astype(zr_ref.dtype))
        zi_all.append((zi * tc - zr * ts).astype(zi_ref.dtype))
    zr_ref[...] = jnp.swapaxes(jnp.stack(zr_all, axis=0), 0, 1)
    zi_ref[...] = jnp.swapaxes(jnp.stack(zi_all, axis=0), 0, 1)


def _fourier_stage2_body(zr_ref, zi_ref, f_ref, c_ref, o_ref, *, scale):
    n2 = zr_ref.shape[1]
    outs = []
    for i in range(zr_ref.shape[0]):
        z = jnp.concatenate([zr_ref[i], zi_ref[i]], axis=0)
        y = jnp.dot(f_ref[...], z, preferred_element_type=F32)
        yc = jnp.concatenate([y[:n2], y[n2:]], axis=1).astype(BF16)
        out = jnp.dot(yc, c_ref[...], preferred_element_type=F32) * scale
        outs.append(out.astype(o_ref.dtype))
    o_ref[...] = jnp.swapaxes(jnp.stack(outs, axis=0), 0, 1)


def _fourier_mix(u, batch, twiddles):
    n, width = u.shape
    s = n // batch
    n2 = FN_N2
    n1 = s // n2
    tc, ts = twiddles
    c1, s1 = _dft_cos_sin(n1)
    f1 = jnp.asarray(np.concatenate([c1, -s1], axis=0), F32).astype(BF16)
    c2, s2 = _dft_cos_sin(n2)
    f2 = jnp.asarray(np.block([[c2, s2], [-s2, c2]]), F32).astype(BF16)
    cg, sg = _dft_cos_sin(FN_GROUP_DIM)
    eye = np.eye(width // FN_GROUP_DIM)
    fc = jnp.asarray(np.concatenate([np.kron(eye, cg), np.kron(eye, sg)], axis=0), F32).astype(BF16)

    grp = FN_GROUP
    pos_blk = pl.BlockSpec((None, n1, grp, width), lambda b, j: (b, 0, j, 0))
    tw_blk = pl.BlockSpec((grp,) + tc.shape[1:], lambda b, j: (j, 0, 0))
    zr, zi = pl.pallas_call(
        _fourier_stage1_body,
        out_shape=[jax.ShapeDtypeStruct((batch, n1, n2, width), BF16)] * 2,
        grid=(batch, n2 // grp),
        in_specs=[pos_blk, pl.BlockSpec((2 * n1, n1), lambda b, j: (0, 0)), tw_blk, tw_blk],
        out_specs=[pos_blk, pos_blk],
        compiler_params=_cparams("arbitrary", "arbitrary"),
        name="fourier_stage1",
    )(u.reshape(batch, n1, n2, width), f1, tc, ts)

    freq_blk = pl.BlockSpec((None, grp, n2, width), lambda b, j: (b, j, 0, 0))
    out = pl.pallas_call(
        functools.partial(_fourier_stage2_body, scale=float(1.0 / np.sqrt(s * FN_GROUP_DIM))),
        out_shape=jax.ShapeDtypeStruct((batch, n2, n1, width), BF16),
        grid=(batch, n1 // grp),
        in_specs=[freq_blk, freq_blk,
                  pl.BlockSpec((2 * n2, 2 * n2), lambda b, j: (0, 0)),
                  pl.BlockSpec((2 * width, width), lambda b, j: (0, 0))],
        out_specs=pl.BlockSpec((None, n2, grp, width), lambda b, j: (b, 0, j, 0)),
        compiler_params=_cparams("arbitrary", "arbitrary"),
        name="fourier_stage2",
    )(zr, zi, f2, fc)
    return out.reshape(n, width)


def _fourier_twiddles(s):
    n2 = FN_N2
    n1 = s // n2
    ang = (2.0 * np.pi / s) * (jnp.arange(n2, dtype=F32)[:, None] * jnp.arange(n1, dtype=F32)[None, :])
    rep = lambda t: jnp.broadcast_to(t[:, :, None], (n2, n1, 128))
    return rep(jnp.cos(ang)), rep(jnp.sin(ang))


LRU_LANES = 128
LRU_SEGMENTS = 16
LRU_JCHUNK = 32
SUBLANES = 8


def _lru_gate_weights(wa_all, wx_all):
    depth, _, nb, db, _ = wa_all.shape
    ncol = nb // 2

    def blockdiag(w):
        w = w.reshape(depth, ncol, 2, db, db)
        z = jnp.zeros_like(w[:, :, 0])
        top = jnp.concatenate([w[:, :, 0], z], axis=-1)
        bot = jnp.concatenate([z, w[:, :, 1]], axis=-1)
        return jnp.concatenate([top, bot], axis=-2)

    parts = [blockdiag(wa_all[:, 0]), blockdiag(wx_all[:, 0]), blockdiag(wa_all[:, 1]), blockdiag(wx_all[:, 1])]
    return jnp.concatenate(parts, axis=-1).astype(BF16)


def _gelu_tanh(x):
    return 0.5 * x * (1.0 + jnp.tanh(np.sqrt(2.0 / np.pi) * (x + 0.044715 * (x * x * x))))


def _lru_body(useq_ref, cw_ref, cb_ref, w_ref, ba_ref, bx_ref, lam_ref, oseq_ref, ux_ref, o_ref, h_scr, p_scr):
    n_j, n_g, lanes = ux_ref.shape
    ux_ref[...] = jnp.swapaxes(useq_ref[...].reshape(n_g, n_j, lanes), 0, 1)
    jc = LRU_JCHUNK
    n_chunks = n_j // jc
    seg = lax.broadcasted_iota(jnp.int32, (n_g, lanes), 0)
    seg3 = lax.broadcasted_iota(jnp.int32, (jc, n_g, lanes), 1)
    row3 = lax.broadcasted_iota(jnp.int32, (jc, n_g, lanes), 0)

    def from_prev_segment(x):
        return jnp.where(seg >= 1, pltpu.roll(x, 1, axis=0), 0.0)

    def from_next_segment(x):
        return jnp.where(seg < n_g - 1, pltpu.roll(x, n_g - 1, axis=0), 0.0)

    def conv_chunk(j0):
        main = ux_ref[pl.ds(j0, jc)].astype(F32)
        lo_in = ux_ref[pl.ds(jnp.maximum(j0 - CONV_PAD_LEFT, 0), CONV_PAD_LEFT)].astype(F32)
        tail = ux_ref[n_j - CONV_PAD_LEFT:n_j].astype(F32)
        lo_wrap = jnp.stack([from_prev_segment(tail[r]) for r in range(CONV_PAD_LEFT)], axis=0)
        lo = jnp.where(j0 > 0, lo_in, lo_wrap)
        n_hi = CONV_W - 1 - CONV_PAD_LEFT
        hi_in = ux_ref[pl.ds(jnp.minimum(j0 + jc, n_j - n_hi), n_hi)].astype(F32)
        head = ux_ref[0:n_hi].astype(F32)
        hi_wrap = jnp.stack([from_next_segment(head[r]) for r in range(n_hi)], axis=0)
        hi = jnp.where(j0 + jc < n_j, hi_in, hi_wrap)
        ext = jnp.concatenate([lo, main, hi], axis=0)
        c = ext[0:jc] * cw_ref[0:1, :] + cb_ref[...]
        for tap in range(1, CONV_W):
            c = c + ext[tap:tap + jc] * cw_ref[tap:tap + 1, :]
        return c

    def gates(c, pre, d, j0):
        r = _sigmoid(pre[:, 2 * d * lanes:(2 * d + 1) * lanes] + ba_ref[d:d + 1, :])
        i = _sigmoid(pre[:, (2 * d + 1) * lanes:(2 * d + 2) * lanes] + bx_ref[d:d + 1, :])
        lam = lam_ref[d:d + 1, :]
        softplus = jnp.maximum(-lam, 0.0) + jnp.log(1.0 + jnp.exp(-jnp.abs(lam)))
        a = jnp.exp(-LRU_C * r * softplus)
        om = 1.0 - a * a
        mult = jnp.where(om > 0.0, om * lax.rsqrt(om), 0.0)
        a = a.reshape(jc, n_g, lanes)
        gain = (mult * i).reshape(jc, n_g, lanes)
        first = (seg3 == (n_g - 1) * d) & (row3 + j0 == (n_j - 1) * d)
        return a, jnp.where(first, i.reshape(jc, n_g, lanes), gain) * c

    def local_scan(a, b, carry, d):
        h, p = carry
        hs, ps = [None] * jc, [None] * jc
        for jj in (range(jc) if d == 0 else range(jc - 1, -1, -1)):
            h = a[jj] * h + b[jj]
            p = a[jj] * p
            hs[jj], ps[jj] = h, p
        return jnp.stack(hs, axis=0), jnp.stack(ps, axis=0), (h, p)

    scan_init = (jnp.zeros((n_g, lanes), F32), jnp.ones((n_g, lanes), F32))

    def gate_and_forward_body(ci, carry):
        j0 = ci * jc
        rows = pl.ds(j0, jc)
        c = conv_chunk(j0)
        pre = jnp.dot(c.reshape(jc * n_g, lanes).astype(BF16), w_ref[...], preferred_element_type=F32)
        a_b, b_b = gates(c, pre, 1, j0)
        p_scr[1, rows] = a_b
        h_scr[1, rows] = b_b
        a_f, b_f = gates(c, pre, 0, j0)
        h_scr[0, rows], p_scr[0, rows], carry = local_scan(a_f, b_f, carry, 0)
        return carry

    lax.fori_loop(0, n_chunks, gate_and_forward_body, scan_init)

    def backward_body(ci, carry):
        rows = pl.ds((n_chunks - 1 - ci) * jc, jc)
        h_scr[1, rows], p_scr[1, rows], carry = local_scan(p_scr[1, rows], h_scr[1, rows], carry, 1)
        return carry

    lax.fori_loop(0, n_chunks, backward_body, scan_init)

    def carry_in(d):
        edge = n_j - 1 if d == 0 else 0
        h_end, p_end = h_scr[d, edge], p_scr[d, edge]
        state = jnp.zeros((1, lanes), F32)
        out = jnp.zeros((n_g, lanes), F32)
        for g in (range(n_g) if d == 0 else range(n_g - 1, -1, -1)):
            out = jnp.where(seg == g, state, out)
            state = h_end[g:g + 1] + p_end[g:g + 1] * state
        return out

    e_fwd, e_bwd = carry_in(0), carry_in(1)

    def out_body(ci, carry):
        j0 = ci * jc
        rows = pl.ds(j0, jc)
        h = h_scr[0, rows] + p_scr[0, rows] * e_fwd + h_scr[1, rows] + p_scr[1, rows] * e_bwd
        o_ref[rows] = h.astype(o_ref.dtype)
        return carry

    lax.fori_loop(0, n_chunks, out_body, 0)
    oseq_ref[...] = jnp.swapaxes(o_ref[...], 0, 1).reshape(n_g * n_j, lanes)


def _recurrent_branch(u_x, conv_w, conv_b, w_gate, ba, bx, lam, layer, batch):
    n, width = u_x.shape
    s = n // batch
    depth = conv_w.shape[0]
    ncol = width // LRU_LANES
    n_g = LRU_SEGMENTS
    n_j = s // n_g
    assert n_j % LRU_JCHUNK == 0
    cb3 = conv_b.reshape(depth, 1, width)
    seq_spec = pl.BlockSpec((None, s, LRU_LANES), lambda b, c: (b, 0, c))
    par = lambda rows: pl.BlockSpec((None, rows, LRU_LANES), lambda b, c: (layer, 0, c))
    seg_copy = pltpu.VMEM((n_j, n_g, LRU_LANES), BF16)
    state = pltpu.VMEM((2, n_j, n_g, LRU_LANES), F32)
    out = pl.pallas_call(
        _lru_body,
        out_shape=jax.ShapeDtypeStruct((batch, s, width), BF16),
        grid=(batch, ncol),
        in_specs=[seq_spec, par(CONV_W), par(1),
                  pl.BlockSpec((None, None, LRU_LANES, 4 * LRU_LANES), lambda b, c: (layer, c, 0, 0)),
                  par(2), par(2), par(2)],
        out_specs=seq_spec,
        scratch_shapes=[seg_copy, seg_copy, state, state],
        compiler_params=_cparams("arbitrary", "arbitrary"),
        name="rg_lru",
    )(u_x.reshape(batch, s, width), conv_w, cb3, w_gate, ba, bx, lam)
    return out.reshape(n, width)


def _merge_body(ya_ref, yb_ref, hc_ref, ug_ref, gt_ref, h_ref, wb_ref, wo_ref, o_ref):
    d = h_ref.shape[1]
    yc = (hc_ref[...].astype(F32) * _gelu_tanh(ug_ref[...].astype(F32))).astype(BF16)
    merged = None
    for kbr, y in enumerate((ya_ref[...], yb_ref[...], yc)):
        ybr = jnp.dot(y, wb_ref[kbr], preferred_element_type=F32)
        term = _sigmoid(gt_ref[:, kbr * d:(kbr + 1) * d].astype(F32)) * ybr
        merged = term if merged is None else merged + term
    o_ref[...] = h_ref[...] + jnp.dot(merged.astype(BF16), wo_ref[...], preferred_element_type=F32)


def _merge(ya, yb, hc, ug, gates, h, wb_all, wo_all, layer, tm=1024):
    n, d = h.shape
    bw = ya.shape[1]
    ytile = pl.BlockSpec((tm, bw), lambda t: (t, 0))
    return pl.pallas_call(
        _merge_body,
        out_shape=jax.ShapeDtypeStruct((n, d), F32),
        grid=(n // tm,),
        in_specs=[ytile, ytile, ytile, ytile,
                  pl.BlockSpec((tm, 3 * d), lambda t: (t, 0)),
                  pl.BlockSpec((tm, d), lambda t: (t, 0)),
                  pl.BlockSpec((None, 3, bw, d), lambda t: (layer, 0, 0, 0)),
                  pl.BlockSpec((None, d, d), lambda t: (layer, 0, 0))],
        out_specs=pl.BlockSpec((tm, d), lambda t: (t, 0)),
        compiler_params=_cparams("arbitrary"),
        name="branch_merge",
    )(ya, yb, hc, ug, gates, h, wb_all, wo_all)


def _ffn_body(h_ref, g_ref, wg_ref, wu_ref, wd_ref, o_ref, xn_ref, acc_ref):
    j = pl.program_id(1)

    @pl.when(j == 0)
    def _():
        h = h_ref[...]
        xn_ref[...] = _rms_norm_f32(h, g_ref[...]).astype(BF16)
        acc_ref[...] = h

    xn = xn_ref[...]
    gate = jnp.dot(xn, wg_ref[...].astype(BF16), preferred_element_type=F32)
    up = jnp.dot(xn, wu_ref[...].astype(BF16), preferred_element_type=F32)
    act = gate * _sigmoid(gate) * up
    acc_ref[...] += jnp.dot(act.astype(BF16), wd_ref[...].astype(BF16), preferred_element_type=F32)

    @pl.when(j == pl.num_programs(1) - 1)
    def _():
        o_ref[...] = acc_ref[...]


def _ffn(h, g_all, layer, w_gu_all, w_down_all, widx, tm=1024, tf=512):
    n, d = h.shape
    depth = g_all.shape[0]
    nf = w_down_all.shape[-2] // tf
    return pl.pallas_call(
        _ffn_body,
        out_shape=jax.ShapeDtypeStruct((n, d), F32),
        grid=(n // tm, nf),
        in_specs=[pl.BlockSpec((tm, d), lambda t, j: (t, 0)),
                  pl.BlockSpec((None, 1, d), lambda t, j: (layer, 0, 0)),
                  pl.BlockSpec((None, d, tf), lambda t, j: (widx, 0, j)),
                  pl.BlockSpec((None, d, tf), lambda t, j: (widx, 0, j + nf)),
                  pl.BlockSpec((None, tf, d), lambda t, j: (widx, j, 0))],
        out_specs=pl.BlockSpec((tm, d), lambda t, j: (t, 0)),
        scratch_shapes=[pltpu.VMEM((tm, d), BF16), pltpu.VMEM((tm, d), F32)],
        compiler_params=_cparams("arbitrary", "arbitrary"),
        name="dense_ffn",
    )(h, g_all.reshape(depth, 1, d), w_gu_all, w_gu_all, w_down_all)


MOE_TOKEN_TILE = 512
MOE_ROW_TILE = 512
MOE_SEG_ALIGN = 16
MOE_TOP_K = 2
MOE_CHUNK_SIZES = tuple(MOE_SEG_ALIGN << b for b in range(5, -1, -1))
assert MOE_CHUNK_SIZES[0] == MOE_TOKEN_TILE


def _moe_compact_rows(ne):
    rows = MOE_TOKEN_TILE * MOE_TOP_K + ne * (MOE_SEG_ALIGN - 1)
    return -(-rows // MOE_SEG_ALIGN) * MOE_SEG_ALIGN


def _moe_sorted_rows(n, ne):
    rows = n * MOE_TOP_K + (n // MOE_TOKEN_TILE) * ne * (MOE_SEG_ALIGN - 1) + ne * (MOE_ROW_TILE - MOE_SEG_ALIGN)
    return -(-rows // MOE_ROW_TILE) * MOE_ROW_TILE


def _router_body(h_ref, g_ref, wrt_ref, xn_ref, posr_ref, wrow_ref, posc_ref, cnt_ref, before_scr, eye_scr):
    t_tokens = h_ref.shape[0]

    @pl.when(pl.program_id(0) == 0)
    def _():
        r_i = lax.broadcasted_iota(jnp.int32, (t_tokens, t_tokens), 0)
        c_i = lax.broadcasted_iota(jnp.int32, (t_tokens, t_tokens), 1)
        before_scr[...] = jnp.where(r_i < c_i, 1.0, 0.0).astype(BF16)
        eye_scr[...] = jnp.where(r_i == c_i, 1.0, 0.0).astype(BF16)

    xn = _rms_norm_f32(h_ref[...], g_ref[...])
    xn_hi = xn.astype(BF16)
    xn_ref[...] = xn_hi
    nt_dims = (((1,), (1,)), ((), ()))
    xn_lo = (xn - xn_hi.astype(F32)).astype(BF16)
    w = wrt_ref[...]
    ne = w.shape[0]
    w_hi = w.astype(BF16)
    w_lo = (w - w_hi.astype(F32)).astype(BF16)
    by_hi = lax.dot_general(jnp.concatenate([w_hi, w_lo], axis=0), xn_hi, nt_dims, preferred_element_type=F32)
    logits = by_hi[:ne] + by_hi[ne:] + lax.dot_general(w_hi, xn_lo, nt_dims, preferred_element_type=F32)
    sub = lax.broadcasted_iota(jnp.int32, logits.shape, 0)
    m1 = jnp.max(logits, axis=0, keepdims=True)
    i1 = jnp.min(jnp.where(logits == m1, sub, ne), axis=0, keepdims=True)
    rest = jnp.where(sub == i1, -jnp.inf, logits)
    m2 = jnp.max(rest, axis=0, keepdims=True)
    i2 = jnp.min(jnp.where(rest == m2, sub, ne), axis=0, keepdims=True)
    e = jnp.exp(m2 - m1)
    wrow_ref[...] = jnp.concatenate([1.0 / (1.0 + e), e / (1.0 + e)], axis=0)

    sel1, sel2 = sub == i1, sub == i2
    memb = jnp.where(sel1, 1.0, jnp.where(sel2, 1.0, 0.0))
    rank = jnp.dot(memb.astype(BF16), before_scr[...], preferred_element_type=F32)
    sub_c = lax.broadcasted_iota(jnp.int32, cnt_ref.shape, 0)
    cnt_out = jnp.zeros(cnt_ref.shape, F32)
    base = rank
    running = jnp.zeros((1, 1), F32)
    for ex in range(ne):
        c = jnp.sum(memb[ex:ex + 1, :], axis=1, keepdims=True)
        cnt_out = jnp.where(sub_c == ex, c, cnt_out)
        base = jnp.where(sub == ex, base + running, base)
        running = running + jnp.floor((c + (MOE_SEG_ALIGN - 1)) * (1.0 / MOE_SEG_ALIGN)) * MOE_SEG_ALIGN
    cnt_ref[...] = cnt_out.astype(jnp.int32)
    pos1 = jnp.sum(jnp.where(sel1, base, 0.0), axis=0, keepdims=True)
    pos2 = jnp.sum(jnp.where(sel2, base, 0.0), axis=0, keepdims=True)
    posr_ref[...] = jnp.concatenate([pos1, pos2], axis=0).astype(jnp.int32)
    digits = []
    for pos in (pos1, pos2):
        hi = jnp.floor(pos * (1.0 / 128.0))
        digits += [hi, pos - 128.0 * hi]
    pad = jnp.zeros((2 * SUBLANES - len(digits), t_tokens), F32)
    dig_t = lax.dot_general(eye_scr[...], jnp.concatenate(digits + [pad], axis=0).astype(BF16), nt_dims,
                            preferred_element_type=F32)
    d_i = lax.broadcasted_iota(jnp.int32, (2 * SUBLANES, posc_ref.shape[1]), 0)
    c_i = lax.broadcasted_iota(jnp.int32, (2 * SUBLANES, posc_ref.shape[1]), 1)
    recombine = jnp.where(d_i == 2 * c_i, 128.0, jnp.where(d_i == 2 * c_i + 1, 1.0, 0.0)).astype(BF16)
    posc = jnp.dot(dig_t.astype(BF16), recombine, preferred_element_type=F32)
    posc_ref[...] = posc.astype(jnp.int32)


def _router(h, g_all, layer, wr_all, widx):
    n, d = h.shape
    depth = g_all.shape[0]
    ne = wr_all.shape[-1]
    tm = MOE_TOKEN_TILE
    nt = n // tm
    wrt = jnp.swapaxes(wr_all, 1, 2)
    return pl.pallas_call(
        _router_body,
        out_shape=[jax.ShapeDtypeStruct((n, d), BF16),
                   jax.ShapeDtypeStruct((MOE_TOP_K, n), jnp.int32),
                   jax.ShapeDtypeStruct((MOE_TOP_K, n), F32),
                   jax.ShapeDtypeStruct((n, SUBLANES), jnp.int32),
                   jax.ShapeDtypeStruct((nt, ne, 128), jnp.int32)],
        grid=(nt,),
        in_specs=[pl.BlockSpec((tm, d), lambda t: (t, 0)),
                  pl.BlockSpec((None, 1, d), lambda t: (layer, 0, 0)),
                  pl.BlockSpec((None, ne, d), lambda t: (widx, 0, 0))],
        out_specs=[pl.BlockSpec((tm, d), lambda t: (t, 0)),
                   pl.BlockSpec((MOE_TOP_K, tm), lambda t: (0, t)),
                   pl.BlockSpec((MOE_TOP_K, tm), lambda t: (0, t)),
                   pl.BlockSpec((tm, SUBLANES), lambda t: (t, 0)),
                   pl.BlockSpec((None, ne, 128), lambda t: (t, 0, 0))],
        scratch_shapes=[pltpu.VMEM((tm, tm), BF16), pltpu.VMEM((tm, tm), BF16)],
        compiler_params=_cparams("arbitrary"),
        name="router",
    )(h, g_all.reshape(depth, 1, d), wrt)


def _moe_tables(cnt, n_sorted_rows):
    cnt = cnt[:, :, 0]
    nt, ne = cnt.shape
    seg = (cnt + (MOE_SEG_ALIGN - 1)) // MOE_SEG_ALIGN * MOE_SEG_ALIGN
    seg_off = jnp.cumsum(seg, axis=1) - seg
    e_rows = jnp.sum(seg, axis=0)
    e_tiles = (e_rows + (MOE_ROW_TILE - 1)) // MOE_ROW_TILE
    e_cum = jnp.cumsum(e_tiles)
    e_base = (e_cum - e_tiles) * MOE_ROW_TILE
    dst = e_base[None, :] + jnp.cumsum(seg, axis=0) - seg
    n_used = e_cum[-1]
    tile_ids = jnp.minimum(jnp.arange(n_sorted_rows // MOE_ROW_TILE, dtype=jnp.int32), n_used - 1)
    tile_expert = jnp.sum(tile_ids[:, None] >= e_cum[None, :], axis=1)
    last_tile_row = e_base + (e_tiles - 1) * MOE_ROW_TILE
    i32 = lambda a: a.astype(jnp.int32)
    return dict(seg_off=i32(seg_off.reshape(-1)), seg_len=i32(seg.reshape(-1)), dst=i32(dst.reshape(-1)),
                n_used=i32(n_used.reshape(1)), tile_expert=i32(tile_expert),
                last_tile_row=i32(last_tile_row), has_rows=i32(e_tiles > 0))


def _moe_chunk_copies(t, ne, seg_off_ref, seg_len_ref, dst_ref, make_copy, act):
    for ex in range(ne):
        idx = t * ne + ex
        off = seg_off_ref[idx]
        ln = seg_len_ref[idx]
        row = dst_ref[idx]
        for size in MOE_CHUNK_SIZES:
            take = (ln & size) != 0

            @pl.when(take)
            def _(off=off, row=row, size=size):
                for cp in make_copy(pl.multiple_of(off, MOE_SEG_ALIGN), pl.multiple_of(row, MOE_SEG_ALIGN), size):
                    act(cp)

            step = jnp.where(take, size, 0)
            off = off + step
            row = row + step


def _dispatch_body(seg_off_ref, seg_len_ref, dst_ref, last_row_ref, has_rows_ref, n_used_ref,
                   xn_ref, posr_ref, wrow_ref, xs_ref, ws_ref, cbuf, wbuf, zx, zw, sems, zsem):
    t = pl.program_id(0)
    n_tiles = pl.num_programs(0)
    ne = last_row_ref.shape[0]
    n_rows, t_tokens = cbuf.shape[1], xn_ref.shape[0]

    def zero_copies(row):
        row = pl.multiple_of(row, MOE_ROW_TILE)
        return (pltpu.make_async_copy(zx, xs_ref.at[pl.ds(row, MOE_ROW_TILE)], zsem),
                pltpu.make_async_copy(zw, ws_ref.at[pl.ds(row, MOE_ROW_TILE)], zsem))

    @pl.when(t == 0)
    def _():
        zx[...] = jnp.zeros_like(zx)
        zw[...] = jnp.zeros_like(zw)
        for act in (lambda cp: cp.start(), lambda cp: cp.wait()):
            for ex in range(ne):
                @pl.when(has_rows_ref[ex] != 0)
                def _(ex=ex):
                    for cp in zero_copies(last_row_ref[ex]):
                        act(cp)

        def tail_body(i, carry):
            for cp in zero_copies(i * MOE_ROW_TILE):
                cp.start()
                cp.wait()
            return carry

        lax.fori_loop(n_used_ref[0], xs_ref.shape[0] // MOE_ROW_TILE, tail_body, 0)

    r_iota = lax.broadcasted_iota(jnp.int32, (n_rows, t_tokens), 0)
    hit1 = r_iota == posr_ref[0:1, :]
    hit2 = r_iota == posr_ref[1:2, :]
    onehot = jnp.where(hit1, 1.0, jnp.where(hit2, 1.0, 0.0)).astype(BF16)
    slot = t % 2
    cbuf[slot] = jnp.dot(onehot, xn_ref[...], preferred_element_type=F32).astype(BF16)
    wsel = jnp.where(hit1, wrow_ref[0:1, :], jnp.where(hit2, wrow_ref[1:2, :], 0.0))
    wbuf[slot] = jnp.broadcast_to(jnp.sum(wsel, axis=1, keepdims=True), wbuf.shape[1:])

    def drain(tile, act):
        s = tile % 2

        def make_copy(off, row, size):
            return (pltpu.make_async_copy(cbuf.at[s, pl.ds(off, size)], xs_ref.at[pl.ds(row, size)], sems.at[s]),
                    pltpu.make_async_copy(wbuf.at[s, pl.ds(off, size)], ws_ref.at[pl.ds(row, size)], sems.at[s]))

        _moe_chunk_copies(tile, ne, seg_off_ref, seg_len_ref, dst_ref, make_copy, act)

    drain(t, lambda cp: cp.start())

    @pl.when(t > 0)
    def _():
        drain(t - 1, lambda cp: cp.wait())

    @pl.when(t == n_tiles - 1)
    def _():
        drain(t, lambda cp: cp.wait())


def _dispatch(xn, posr, wrow, tables, n_sorted_rows, ne):
    n, d = xn.shape
    tm = MOE_TOKEN_TILE
    n_rows = _moe_compact_rows(ne)
    grid_spec = pltpu.PrefetchScalarGridSpec(
        num_scalar_prefetch=6,
        grid=(n // tm,),
        in_specs=[pl.BlockSpec((tm, d), lambda t, *_: (t, 0)),
                  pl.BlockSpec((MOE_TOP_K, tm), lambda t, *_: (0, t)),
                  pl.BlockSpec((MOE_TOP_K, tm), lambda t, *_: (0, t))],
        out_specs=[pl.BlockSpec(memory_space=pl.ANY), pl.BlockSpec(memory_space=pl.ANY)],
        scratch_shapes=[pltpu.VMEM((2, n_rows, d), BF16), pltpu.VMEM((2, n_rows, 128), F32),
                        pltpu.VMEM((MOE_ROW_TILE, d), BF16), pltpu.VMEM((MOE_ROW_TILE, 128), F32),
                        pltpu.SemaphoreType.DMA((2,)), pltpu.SemaphoreType.DMA],
    )
    return pl.pallas_call(
        _dispatch_body,
        out_shape=[jax.ShapeDtypeStruct((n_sorted_rows, d), BF16), jax.ShapeDtypeStruct((n_sorted_rows, 128), F32)],
        grid_spec=grid_spec,
        compiler_params=_cparams("arbitrary"),
        name="moe_dispatch",
    )(tables['seg_off'], tables['seg_len'], tables['dst'], tables['last_tile_row'], tables['has_rows'], tables['n_used'],
      xn, posr, wrow)


def _experts_body(tile_expert_ref, n_used_ref, xs_ref, ws_ref, wgu_ref, wd_ref, ys_ref, wgu16, wd16):
    i = pl.program_id(0)
    used = i < n_used_ref[0]
    prev = tile_expert_ref[jnp.maximum(i - 1, 0)]
    new_expert = jnp.logical_or(i == 0, tile_expert_ref[i] != prev)
    f = wd_ref.shape[0]
    fc = 512

    @pl.when(jnp.logical_and(used, new_expert))
    def _():
        for c in range(2 * f // fc):
            wgu16[:, c * fc:(c + 1) * fc] = wgu_ref[:, c * fc:(c + 1) * fc].astype(BF16)
        for c in range(f // fc):
            wd16[c * fc:(c + 1) * fc, :] = wd_ref[c * fc:(c + 1) * fc, :].astype(BF16)

    @pl.when(used)
    def _():
        x = xs_ref[...]
        w = ws_ref[...]
        wrep = jnp.concatenate([w] * (fc // w.shape[1]), axis=1)
        acc = None
        for c in range(f // fc):
            gate = jnp.dot(x, wgu16[:, c * fc:(c + 1) * fc], preferred_element_type=F32)
            up = jnp.dot(x, wgu16[:, f + c * fc:f + (c + 1) * fc], preferred_element_type=F32)
            act = (gate * _sigmoid(gate) * up * wrep).astype(BF16)
            part = jnp.dot(act, wd16[c * fc:(c + 1) * fc, :], preferred_element_type=F32)
            acc = part if acc is None else acc + part
        ys_ref[...] = acc.astype(ys_ref.dtype)

    @pl.when(jnp.logical_not(used))
    def _():
        ys_ref[...] = jnp.zeros_like(ys_ref)


def _experts(xs, ws, tables, w_gu_all, w_down_all, widx):
    rows, d = xs.shape
    f = w_down_all.shape[-2]
    tile = lambda w: pl.BlockSpec((MOE_ROW_TILE, w), lambda i, te, nu: (jnp.maximum(jnp.minimum(i, nu[0] - 1), 0), 0))
    grid_spec = pltpu.PrefetchScalarGridSpec(
        num_scalar_prefetch=2,
        grid=(rows // MOE_ROW_TILE,),
        in_specs=[tile(d), tile(ws.shape[1]),
                  pl.BlockSpec((None, None, d, 2 * f), lambda i, te, nu: (widx, te[i], 0, 0)),
                  pl.BlockSpec((None, None, f, d), lambda i, te, nu: (widx, te[i], 0, 0), pipeline_mode=pl.Buffered(1))],
        out_specs=pl.BlockSpec((MOE_ROW_TILE, d), lambda i, te, nu: (i, 0)),
        scratch_shapes=[pltpu.VMEM((d, 2 * f), BF16), pltpu.VMEM((f, d), BF16)],
    )
    return pl.pallas_call(
        _experts_body,
        out_shape=jax.ShapeDtypeStruct((rows, d), BF16),
        grid_spec=grid_spec,
        compiler_params=_cparams("arbitrary"),
        name="moe_experts",
    )(tables['tile_expert'], tables['n_used'], xs, ws, w_gu_all, w_down_all)


def _combine_body(seg_off_ref, seg_len_ref, dst_ref, ys_ref, posc_ref, h_ref, fg_ref, o_ref, ybuf, sems, *, ne, final_norm):
    t = pl.program_id(0)
    n_tiles = pl.num_programs(0)
    t_tokens, n_rows = h_ref.shape[0], ybuf.shape[1]

    def fetch(tile, act):
        slot = tile % 2

        def make_copy(off, row, size):
            return (pltpu.make_async_copy(ys_ref.at[pl.ds(row, size)], ybuf.at[slot, pl.ds(off, size)], sems.at[slot]),)

        _moe_chunk_copies(tile, ne, seg_off_ref, seg_len_ref, dst_ref, make_copy, act)

    @pl.when(t == 0)
    def _():
        ybuf[...] = jnp.zeros_like(ybuf)
        fetch(t, lambda cp: cp.start())

    @pl.when(t + 1 < n_tiles)
    def _():
        fetch(t + 1, lambda cp: cp.start())

    fetch(t, lambda cp: cp.wait())

    pos = posc_ref[...]
    lane_r = lax.broadcasted_iota(jnp.int32, (t_tokens, n_rows), 1)
    onehot = jnp.where(lane_r == pos[:, 0:1], 1.0, jnp.where(lane_r == pos[:, 1:2], 1.0, 0.0)).astype(BF16)
    out = h_ref[...] + jnp.dot(onehot, ybuf[t % 2], preferred_element_type=F32)
    o_ref[...] = _rms_norm_f32(out, fg_ref[...]) if final_norm else out


def _combine(ys, posc, h, tables, ne, final_g=None):
    n, d = h.shape
    tm = MOE_TOKEN_TILE
    final_norm = final_g is not None
    fg = final_g.reshape(1, d) if final_norm else jnp.ones((1, d), F32)
    grid_spec = pltpu.PrefetchScalarGridSpec(
        num_scalar_prefetch=3,
        grid=(n // tm,),
        in_specs=[pl.BlockSpec(memory_space=pl.ANY),
                  pl.BlockSpec((tm, posc.shape[1]), lambda t, *_: (t, 0)),
                  pl.BlockSpec((tm, d), lambda t, *_: (t, 0)),
                  pl.BlockSpec((1, d), lambda t, *_: (0, 0))],
        out_specs=pl.BlockSpec((tm, d), lambda t, *_: (t, 0)),
        scratch_shapes=[pltpu.VMEM((2, _moe_compact_rows(ne), d), BF16), pltpu.SemaphoreType.DMA((2,))],
    )
    return pl.pallas_call(
        functools.partial(_combine_body, ne=ne, final_norm=final_norm),
        out_shape=jax.ShapeDtypeStruct((n, d), F32),
        grid_spec=grid_spec,
        compiler_params=_cparams("arbitrary"),
        name="moe_combine",
    )(tables['seg_off'], tables['seg_len'], tables['dst'], ys, posc, h, fg)


def _moe(h, g_all, layer, wr_all, w_gu_all, w_down_all, widx, final_g=None):
    n = h.shape[0]
    ne = wr_all.shape[-1]
    xn, posr, wrow, posc, cnt = _router(h, g_all, layer, wr_all, widx)
    n_sorted_rows = _moe_sorted_rows(n, ne)
    tables = _moe_tables(cnt, n_sorted_rows)
    xs, ws = _dispatch(xn, posr, wrow, tables, n_sorted_rows, ne)
    ys = _experts(xs, ws, tables, w_gu_all, w_down_all, widx)
    return _combine(ys, posc, h, tables, ne, final_g)


def _final_norm_body(h_ref, g_ref, o_ref):
    o_ref[...] = _rms_norm_f32(h_ref[...], g_ref[...])


def _final_norm(h, g, tm=1024):
    n, d = h.shape
    return pl.pallas_call(
        _final_norm_body,
        out_shape=jax.ShapeDtypeStruct((n, d), F32),
        grid=(n // tm,),
        in_specs=[pl.BlockSpec((tm, d), lambda t: (t, 0)), pl.BlockSpec((1, d), lambda t: (0, 0))],
        out_specs=pl.BlockSpec((tm, d), lambda t: (t, 0)),
        compiler_params=_cparams("arbitrary"),
        name="final_norm",
    )(h, g.reshape(1, d))


def kernel(x, norm_mix_g, w_in, b_in, na_rpb, conv_w, conv_b, lru_wa, lru_ba, lru_wx, lru_bx, lru_lambda, w_branch, w_out, norm_ffn_g, ffn_w_gu, ffn_w_down, router_w, moe_w_gu, moe_w_down, final_g):
    batch, seq, d = x.shape
    depth = w_in.shape[0]
    n = batch * seq
    bw = w_branch.shape[2]
    h = x.reshape(n, d)

    tables = _na_bias_tables(na_rpb)
    twiddles = _fourier_twiddles(seq)
    w_gate = _lru_gate_weights(lru_wa, lru_wx)
    wb16 = w_branch.astype(BF16)
    wo16 = w_out.astype(BF16)
    tn = 3 * bw

    for l in range(depth):
        q, k, v = _norm_proj(h, norm_mix_g, w_in, b_in, l, 0, 1, tn, 3)
        u_f, u_x, u_g = _norm_proj(h, norm_mix_g, w_in, b_in, l, 1, 1, tn, 3)
        (gates,) = _norm_proj(h, norm_mix_g, w_in, b_in, l, 2, (3 * d) // tn, tn, 1)
        y_a = _neighbourhood_attention(q, k, v, tables, l, batch)
        y_b = _fourier_mix(u_f, batch, twiddles)
        h_c = _recurrent_branch(u_x, conv_w, conv_b, w_gate, lru_ba, lru_bx, lru_lambda, l, batch)
        h = _merge(y_a, y_b, h_c, u_g, gates, h, wb16, wo16, l)
        if l % 2 == 0:
            h = _ffn(h, norm_ffn_g, l, ffn_w_gu, ffn_w_down, l // 2)
        else:
            h = _moe(h, norm_ffn_g, l, router_w, moe_w_gu, moe_w_down, l // 2,
                     final_g=final_g if l == depth - 1 else None)
    if depth % 2 == 1:
        h = _final_norm(h, final_g)
    return h.reshape(batch, seq, d)
```

```python
import functools

import numpy as np
import jax
import jax.numpy as jnp
from jax import lax
from jax.experimental import pallas as pl
from jax.experimental.pallas import tpu as pltpu

F32 = jnp.float32
BF16 = jnp.bfloat16

RMS_EPS = 1e-6
GRID_W = 64
NA_HEADS = 8
NA_HEAD_DIM = 64
NA_KH = 8
NA_KW = 16
NA_ROWS_PER_BLOCK = 8
NA_ROWS_IN_FLIGHT = 4
LRU_C = 8.0
CONV_W = 4
CONV_PAD_LEFT = 2
N_EXPERTS = 8
MASK_VALUE = -1e30

VMEM_LIMIT_BYTES = 56 * 1024 * 1024


def _cparams(*sem):
    return pltpu.CompilerParams(dimension_semantics=sem, vmem_limit_bytes=VMEM_LIMIT_BYTES)


def _rms_norm_f32(x, g):
    ms = jnp.mean(x * x, axis=-1, keepdims=True)
    return x * lax.rsqrt(ms + RMS_EPS) * g


def _sigmoid(x):
    return 1.0 / (1.0 + jnp.exp(-x))


def _rms_norm_body(h_ref, g_ref, o_ref):
    o_ref[...] = _rms_norm_f32(h_ref[...], g_ref[...]).astype(o_ref.dtype)


def _rms_norm(h, g, out_dtype, tm=1024):
    n, d = h.shape
    return pl.pallas_call(
        _rms_norm_body,
        out_shape=jax.ShapeDtypeStruct((n, d), out_dtype),
        grid=(n // tm,),
        in_specs=[pl.BlockSpec((tm, d), lambda t: (t, 0)), pl.BlockSpec((1, d), lambda t: (0, 0))],
        out_specs=pl.BlockSpec((tm, d), lambda t: (t, 0)),
        compiler_params=_cparams("arbitrary"),
        name="rms_norm",
    )(h, g.reshape(1, d))


def _proj_body(x_ref, w_ref, b_ref, *out_refs):
    r = jnp.dot(x_ref[...], w_ref[...].astype(BF16), preferred_element_type=F32) + b_ref[...]
    width = r.shape[1] // len(out_refs)
    for i, o in enumerate(out_refs):
        o[...] = r[:, i * width:(i + 1) * width].astype(o.dtype)


def _proj(xn, w_all, b_all, layer, col_block, tn, n_out, tm=1024):
    n, d = xn.shape
    depth = w_all.shape[0]
    b3 = b_all.reshape(depth, 1, -1)
    width = tn // n_out
    return pl.pallas_call(
        _proj_body,
        out_shape=[jax.ShapeDtypeStruct((n, width), BF16) for _ in range(n_out)],
        grid=(n // tm,),
        in_specs=[
            pl.BlockSpec((tm, d), lambda t: (t, 0)),
            pl.BlockSpec((None, d, tn), lambda t: (layer, 0, col_block)),
            pl.BlockSpec((None, 1, tn), lambda t: (layer, 0, col_block)),
        ],
        out_specs=[pl.BlockSpec((tm, width), lambda t: (t, 0)) for _ in range(n_out)],
        compiler_params=_cparams("arbitrary"),
        name="in_proj",
    )(xn, w_all, b3)


def _na_bias_tables(rpb_all):
    cols = np.arange(GRID_W)
    col_start = np.clip(cols - NA_KW // 2, 0, GRID_W - NA_KW)
    cc = np.arange(GRID_W)[None, :]
    in_win = (cc >= col_start[:, None]) & (cc < col_start[:, None] + NA_KW)
    col_off = cc - cols[:, None] + (NA_KW - 1)
    onehot = np.zeros((2 * NA_KW - 1, GRID_W, GRID_W), np.float32)
    cq, ck = np.nonzero(in_win)
    onehot[col_off[cq, ck], cq, ck] = 1.0
    t = jnp.einsum('lhro,ocd->lhrcd', rpb_all.astype(F32), jnp.asarray(onehot),
                   precision=lax.Precision.HIGHEST)
    t = jnp.where(jnp.asarray(in_win)[None, None, None], t, MASK_VALUE)
    depth, heads, n_off = t.shape[:3]
    t = t.reshape(depth, heads // 2, 2, n_off, GRID_W, GRID_W)
    t = jnp.transpose(t, (0, 1, 3, 2, 4, 5)).reshape(depth, heads // 2, n_off, 2 * GRID_W, GRID_W)
    return jnp.concatenate([t[:, :, :-1], t[:, :, 1:]], axis=-1)


def _na_body(q_ref, kp_ref, kc_ref, kn_ref, vp_ref, vc_ref, vn_ref, tbl_ref, o_ref, kbuf, vbuf, s_scr, e_scr, *, n_blocks):
    j = pl.program_id(1)
    blk = NA_ROWS_PER_BLOCK * GRID_W
    kbuf[0:blk, :] = kp_ref[...]
    kbuf[blk:2 * blk, :] = kc_ref[...]
    kbuf[2 * blk:3 * blk, :] = kn_ref[...]
    vbuf[0:blk, :] = vp_ref[...]
    vbuf[blk:2 * blk, :] = vc_ref[...]
    vbuf[2 * blk:3 * blk, :] = vn_ref[...]
    band = NA_KH * GRID_W
    half = NA_KH // 2
    lane = lax.broadcasted_iota(jnp.int32, (GRID_W, 2 * NA_HEAD_DIM), 1)
    lo = lane < NA_HEAD_DIM
    qscale = NA_HEAD_DIM ** -0.5
    head_mask = (jnp.where(lo, qscale, 0.0).astype(BF16), jnp.where(lo, 0.0, qscale).astype(BF16))

    n_pairs = NA_HEADS // 2
    cols = [slice(p * 2 * NA_HEAD_DIM, (p + 1) * 2 * NA_HEAD_DIM) for p in range(n_pairs)]

    def row_offsets(rl):
        start_first = NA_ROWS_PER_BLOCK + jnp.maximum(rl - half, 0)
        start_last = NA_ROWS_PER_BLOCK + jnp.minimum(rl - half, 0)
        start = jnp.where(j == 0, start_first, jnp.where(j == n_blocks - 1, start_last, rl + half))
        delta = jnp.where(j == 0, jnp.minimum(rl, half), jnp.where(j == n_blocks - 1, jnp.maximum(rl, half), half))
        return pl.multiple_of(rl * GRID_W, GRID_W), pl.multiple_of(start * GRID_W, GRID_W), delta

    def scores(rl, slot):
        qrow, krow, delta = row_offsets(rl)
        for p in range(n_pairs):
            q2 = q_ref[pl.ds(qrow, GRID_W), cols[p]]
            qs = jnp.concatenate([q2 * head_mask[0], q2 * head_mask[1]], axis=0)
            k2 = kbuf[pl.ds(krow, band), cols[p]]
            s = lax.dot_general(qs, k2, (((1,), (1,)), ((), ())), preferred_element_type=F32)
            bias = jnp.concatenate([tbl_ref[p, 2 * m - delta + (NA_KH - 1)] for m in range(NA_KH // 2)], axis=1)
            s_scr[slot, p] = s + bias

    def softmax(slot):
        inv_l = []
        for p in range(n_pairs):
            s = s_scr[slot, p]
            m = jnp.max(s, axis=-1, keepdims=True)
            e = jnp.exp(s - m)
            inv_l.append(1.0 / jnp.sum(e, axis=-1, keepdims=True))
            e_scr[slot, p] = e.astype(BF16)
        return inv_l

    def weighted_values(rl, slot, inv_l):
        qrow, krow, _ = row_offsets(rl)
        for p in range(n_pairs):
            v2 = vbuf[pl.ds(krow, band), cols[p]]
            o = jnp.dot(e_scr[slot, p], v2, preferred_element_type=F32) * inv_l[p]
            o_ref[pl.ds(qrow, GRID_W), cols[p]] = jnp.where(lo, o[:GRID_W], o[GRID_W:]).astype(o_ref.dtype)

    def rows_body(it, carry):
        rows = [it * NA_ROWS_IN_FLIGHT + r for r in range(NA_ROWS_IN_FLIGHT)]
        for slot, rl in enumerate(rows):
            scores(rl, slot)
        inv = [softmax(slot) for slot in range(NA_ROWS_IN_FLIGHT)]
        for slot, rl in enumerate(rows):
            weighted_values(rl, slot, inv[slot])
        return carry

    lax.fori_loop(0, NA_ROWS_PER_BLOCK // NA_ROWS_IN_FLIGHT, rows_body, 0)


def _neighbourhood_attention(q, k, v, tables, layer, batch):
    n, width = q.shape
    n_pairs = NA_HEADS // 2
    blk = NA_ROWS_PER_BLOCK * GRID_W
    n_blocks = n // batch // blk
    assert n_blocks >= 2

    def tok(off):
        return lambda b, j: (b * n_blocks + jnp.clip(j + off, 0, n_blocks - 1), 0)

    tile = lambda off: pl.BlockSpec((blk, width), tok(off))
    return pl.pallas_call(
        functools.partial(_na_body, n_blocks=n_blocks),
        out_shape=jax.ShapeDtypeStruct((n, width), BF16),
        grid=(batch, n_blocks),
        in_specs=[tile(0), tile(-1), tile(0), tile(1), tile(-1), tile(0), tile(1),
                  pl.BlockSpec((None,) + tables.shape[1:], lambda b, j: (layer, 0, 0, 0, 0))],
        out_specs=tile(0),
        scratch_shapes=[pltpu.VMEM((3 * blk, width), BF16), pltpu.VMEM((3 * blk, width), BF16),
                        pltpu.VMEM((NA_ROWS_IN_FLIGHT, n_pairs, 2 * GRID_W, NA_KH * GRID_W), F32),
                        pltpu.VMEM((NA_ROWS_IN_FLIGHT, n_pairs, 2 * GRID_W, NA_KH * GRID_W), BF16)],
        compiler_params=_cparams("arbitrary", "arbitrary"),
        name="neigh_attn",
    )(q, k, k, k, v, v, v, tables)


FN_N2 = 128
FN_GROUP_DIM = 64
FN_GROUP = 16


def _dft_cos_sin(n):
    ang = 2.0 * np.pi * (np.outer(np.arange(n), np.arange(n)) % n) / n
    return np.cos(ang), np.sin(ang)


def _fourier_stage1_body(x_ref, f_ref, tc_ref, ts_ref, zr_ref, zi_ref):
    n1, n_b, width = x_ref.shape
    xs = jnp.swapaxes(x_ref[...], 0, 1)
    rep = width // tc_ref.shape[2]
    zr_all, zi_all = [], []
    for b in range(n_b):
        z = jnp.dot(f_ref[...], xs[b], preferred_element_type=F32)
        zr, zi = z[:n1], z[n1:]
        tc = jnp.concatenate([tc_ref[b]] * rep, axis=1)
        ts = jnp.concatenate([ts_ref[b]] * rep, axis=1)
        zr_all.append((zr * tc + zi * ts).astype(zr_ref.dtype))
        zi_all.append((zi * tc - zr * ts).astype(zi_ref.dtype))
    zr_ref[...] = jnp.swapaxes(jnp.stack(zr_all, axis=0), 0, 1)
    zi_ref[...] = jnp.swapaxes(jnp.stack(zi_all, axis=0), 0, 1)


def _fourier_stage2_body(zr_ref, zi_ref, f_ref, c_ref, o_ref, *, scale):
    n2 = zr_ref.shape[1]
    outs = []
    for i in range(zr_ref.shape[0]):
        z = jnp.concatenate([zr_ref[i], zi_ref[i]], axis=0)
        y = jnp.dot(f_ref[...], z, preferred_element_type=F32)
        yc = jnp.concatenate([y[:n2], y[n2:]], axis=1).astype(BF16)
        out = jnp.dot(yc, c_ref[...], preferred_element_type=F32) * scale
        outs.append(out.astype(o_ref.dtype))
    o_ref[...] = jnp.swapaxes(jnp.stack(outs, axis=0), 0, 1)


def _fourier_mix(u, batch, twiddles):
    n, width = u.shape
    s = n // batch
    n2 = FN_N2
    n1 = s // n2
    tc, ts = twiddles
    c1, s1 = _dft_cos_sin(n1)
    f1 = jnp.asarray(np.concatenate([c1, -s1], axis=0), F32).astype(BF16)
    c2, s2 = _dft_cos_sin(n2)
    f2 = jnp.asarray(np.block([[c2, s2], [-s2, c2]]), F32).astype(BF16)
    cg, sg = _dft_cos_sin(FN_GROUP_DIM)
    eye = np.eye(width // FN_GROUP_DIM)
    fc = jnp.asarray(np.concatenate([np.kron(eye, cg), np.kron(eye, sg)], axis=0), F32).astype(BF16)

    grp = FN_GROUP
    pos_blk = pl.BlockSpec((None, n1, grp, width), lambda b, j: (b, 0, j, 0))
    tw_blk = pl.BlockSpec((grp,) + tc.shape[1:], lambda b, j: (j, 0, 0))
    zr, zi = pl.pallas_call(
        _fourier_stage1_body,
        out_shape=[jax.ShapeDtypeStruct((batch, n1, n2, width), BF16)] * 2,
        grid=(batch, n2 // grp),
        in_specs=[pos_blk, pl.BlockSpec((2 * n1, n1), lambda b, j: (0, 0)), tw_blk, tw_blk],
        out_specs=[pos_blk, pos_blk],
        compiler_params=_cparams("arbitrary", "arbitrary"),
        name="fourier_stage1",
    )(u.reshape(batch, n1, n2, width), f1, tc, ts)

    freq_blk = pl.BlockSpec((None, grp, n2, width), lambda b, j: (b, j, 0, 0))
    out = pl.pallas_call(
        functools.partial(_fourier_stage2_body, scale=float(1.0 / np.sqrt(s * FN_GROUP_DIM))),
        out_shape=jax.ShapeDtypeStruct((batch, n2, n1, width), BF16),
        grid=(batch, n1 // grp),
        in_specs=[freq_blk, freq_blk,
                  pl.BlockSpec((2 * n2, 2 * n2), lambda b, j: (0, 0)),
                  pl.BlockSpec((2 * width, width), lambda b, j: (0, 0))],
        out_specs=pl.BlockSpec((None, n2, grp, width), lambda b, j: (b, 0, j, 0)),
        compiler_params=_cparams("arbitrary", "arbitrary"),
        name="fourier_stage2",
    )(zr, zi, f2, fc)
    return out.reshape(n, width)


def _fourier_twiddles(s):
    n2 = FN_N2
    n1 = s // n2
    ang = (2.0 * np.pi / s) * (jnp.arange(n2, dtype=F32)[:, None] * jnp.arange(n1, dtype=F32)[None, :])
    rep = lambda t: jnp.broadcast_to(t[:, :, None], (n2, n1, 128))
    return rep(jnp.cos(ang)), rep(jnp.sin(ang))


LRU_LANES = 128
LRU_SEGMENTS = 16
LRU_JCHUNK = 32
SUBLANES = 8


def _lru_gate_weights(wa_all, wx_all):
    depth, _, nb, db, _ = wa_all.shape
    ncol = nb // 2

    def blockdiag(w):
        w = w.reshape(depth, ncol, 2, db, db)
        z = jnp.zeros_like(w[:, :, 0])
        top = jnp.concatenate([w[:, :, 0], z], axis=-1)
        bot = jnp.concatenate([z, w[:, :, 1]], axis=-1)
        return jnp.concatenate([top, bot], axis=-2)

    parts = [blockdiag(wa_all[:, 0]), blockdiag(wx_all[:, 0]), blockdiag(wa_all[:, 1]), blockdiag(wx_all[:, 1])]
    return jnp.concatenate(parts, axis=-1).astype(BF16)


def _gelu_tanh(x):
    return 0.5 * x * (1.0 + jnp.tanh(np.sqrt(2.0 / np.pi) * (x + 0.044715 * (x * x * x))))


def _lru_body(useq_ref, cw_ref, cb_ref, w_ref, ba_ref, bx_ref, lam_ref, oseq_ref, ux_ref, o_ref, h_scr, p_scr):
    n_j, n_g, lanes = ux_ref.shape
    ux_ref[...] = jnp.swapaxes(useq_ref[...].reshape(n_g, n_j, lanes), 0, 1)
    jc = LRU_JCHUNK
    n_chunks = n_j // jc
    seg = lax.broadcasted_iota(jnp.int32, (n_g, lanes), 0)
    seg3 = lax.broadcasted_iota(jnp.int32, (jc, n_g, lanes), 1)
    row3 = lax.broadcasted_iota(jnp.int32, (jc, n_g, lanes), 0)

    def from_prev_segment(x):
        return jnp.where(seg >= 1, pltpu.roll(x, 1, axis=0), 0.0)

    def from_next_segment(x):
        return jnp.where(seg < n_g - 1, pltpu.roll(x, n_g - 1, axis=0), 0.0)

    def conv_chunk(j0):
        main = ux_ref[pl.ds(j0, jc)].astype(F32)
        lo_in = ux_ref[pl.ds(jnp.maximum(j0 - CONV_PAD_LEFT, 0), CONV_PAD_LEFT)].astype(F32)
        tail = ux_ref[n_j - CONV_PAD_LEFT:n_j].astype(F32)
        lo_wrap = jnp.stack([from_prev_segment(tail[r]) for r in range(CONV_PAD_LEFT)], axis=0)
        lo = jnp.where(j0 > 0, lo_in, lo_wrap)
        n_hi = CONV_W - 1 - CONV_PAD_LEFT
        hi_in = ux_ref[pl.ds(jnp.minimum(j0 + jc, n_j - n_hi), n_hi)].astype(F32)
        head = ux_ref[0:n_hi].astype(F32)
        hi_wrap = jnp.stack([from_next_segment(head[r]) for r in range(n_hi)], axis=0)
        hi = jnp.where(j0 + jc < n_j, hi_in, hi_wrap)
        ext = jnp.concatenate([lo, main, hi], axis=0)
        c = ext[0:jc] * cw_ref[0:1, :] + cb_ref[...]
        for tap in range(1, CONV_W):
            c = c + ext[tap:tap + jc] * cw_ref[tap:tap + 1, :]
        return c

    def gates(c, pre, d, j0):
        r = _sigmoid(pre[:, 2 * d * lanes:(2 * d + 1) * lanes] + ba_ref[d:d + 1, :])
        i = _sigmoid(pre[:, (2 * d + 1) * lanes:(2 * d + 2) * lanes] + bx_ref[d:d + 1, :])
        lam = lam_ref[d:d + 1, :]
        softplus = jnp.maximum(-lam, 0.0) + jnp.log(1.0 + jnp.exp(-jnp.abs(lam)))
        a = jnp.exp(-LRU_C * r * softplus)
        om = 1.0 - a * a
        mult = jnp.where(om > 0.0, om * lax.rsqrt(om), 0.0)
        a = a.reshape(jc, n_g, lanes)
        gain = (mult * i).reshape(jc, n_g, lanes)
        first = (seg3 == (n_g - 1) * d) & (row3 + j0 == (n_j - 1) * d)
        return a, jnp.where(first, i.reshape(jc, n_g, lanes), gain) * c

    def local_scan(a, b, carry, d):
        h, p = carry
        hs, ps = [None] * jc, [None] * jc
        for jj in (range(jc) if d == 0 else range(jc - 1, -1, -1)):
            h = a[jj] * h + b[jj]
            p = a[jj] * p
            hs[jj], ps[jj] = h, p
        return jnp.stack(hs, axis=0), jnp.stack(ps, axis=0), (h, p)

    scan_init = (jnp.zeros((n_g, lanes), F32), jnp.ones((n_g, lanes), F32))

    def gate_and_forward_body(ci, carry):
        j0 = ci * jc
        rows = pl.ds(j0, jc)
        c = conv_chunk(j0)
        pre = jnp.dot(c.reshape(jc * n_g, lanes).astype(BF16), w_ref[...], preferred_element_type=F32)
        a_b, b_b = gates(c, pre, 1, j0)
        p_scr[1, rows] = a_b
        h_scr[1, rows] = b_b
        a_f, b_f = gates(c, pre, 0, j0)
        h_scr[0, rows], p_scr[0, rows], carry = local_scan(a_f, b_f, carry, 0)
        return carry

    lax.fori_loop(0, n_chunks, gate_and_forward_body, scan_init)

    def backward_body(ci, carry):
        rows = pl.ds((n_chunks - 1 - ci) * jc, jc)
        h_scr[1, rows], p_scr[1, rows], carry = local_scan(p_scr[1, rows], h_scr[1, rows], carry, 1)
        return carry

    lax.fori_loop(0, n_chunks, backward_body, scan_init)

    def carry_in(d):
        edge = n_j - 1 if d == 0 else 0
        h_end, p_end = h_scr[d, edge], p_scr[d, edge]
        state = jnp.zeros((1, lanes), F32)
        out = jnp.zeros((n_g, lanes), F32)
        for g in (range(n_g) if d == 0 else range(n_g - 1, -1, -1)):
            out = jnp.where(seg == g, state, out)
            state = h_end[g:g + 1] + p_end[g:g + 1] * state
        return out

    e_fwd, e_bwd = carry_in(0), carry_in(1)

    def out_body(ci, carry):
        j0 = ci * jc
        rows = pl.ds(j0, jc)
        h = h_scr[0, rows] + p_scr[0, rows] * e_fwd + h_scr[1, rows] + p_scr[1, rows] * e_bwd
        o_ref[rows] = h.astype(o_ref.dtype)
        return carry

    lax.fori_loop(0, n_chunks, out_body, 0)
    oseq_ref[...] = jnp.swapaxes(o_ref[...], 0, 1).reshape(n_g * n_j, lanes)


def _recurrent_branch(u_x, conv_w, conv_b, w_gate, ba, bx, lam, layer, batch):
    n, width = u_x.shape
    s = n // batch
    depth = conv_w.shape[0]
    ncol = width // LRU_LANES
    n_g = LRU_SEGMENTS
    n_j = s // n_g
    assert n_j % LRU_JCHUNK == 0
    cb3 = conv_b.reshape(depth, 1, width)
    seq_spec = pl.BlockSpec((None, s, LRU_LANES), lambda b, c: (b, 0, c))
    par = lambda rows: pl.BlockSpec((None, rows, LRU_LANES), lambda b, c: (layer, 0, c))
    seg_copy = pltpu.VMEM((n_j, n_g, LRU_LANES), BF16)
    state = pltpu.VMEM((2, n_j, n_g, LRU_LANES), F32)
    out = pl.pallas_call(
        _lru_body,
        out_shape=jax.ShapeDtypeStruct((batch, s, width), BF16),
        grid=(batch, ncol),
        in_specs=[seq_spec, par(CONV_W), par(1),
                  pl.BlockSpec((None, None, LRU_LANES, 4 * LRU_LANES), lambda b, c: (layer, c, 0, 0)),
                  par(2), par(2), par(2)],
        out_specs=seq_spec,
        scratch_shapes=[seg_copy, seg_copy, state, state],
        compiler_params=_cparams("arbitrary", "arbitrary"),
        name="rg_lru",
    )(u_x.reshape(batch, s, width), conv_w, cb3, w_gate, ba, bx, lam)
    return out.reshape(n, width)


def _merge_body(ya_ref, yb_ref, hc_ref, ug_ref, xn_ref, h_ref, wgate_ref, bgate_ref, wb_ref, wo_ref, o_ref, wgate16):
    d = h_ref.shape[1]

    @pl.when(pl.program_id(0) == 0)
    def _():
        for kbr in range(wgate16.shape[1] // d):
            wgate16[:, kbr * d:(kbr + 1) * d] = wgate_ref[:, kbr * d:(kbr + 1) * d].astype(BF16)

    xn = xn_ref[...]
    yc = (hc_ref[...].astype(F32) * _gelu_tanh(ug_ref[...].astype(F32))).astype(BF16)
    merged = None
    for kbr, y in enumerate((ya_ref[...], yb_ref[...], yc)):
        cols = slice(kbr * d, (kbr + 1) * d)
        gate = jnp.dot(xn, wgate16[:, cols], preferred_element_type=F32) + bgate_ref[:, cols]
        ybr = jnp.dot(y, wb_ref[kbr], preferred_element_type=F32)
        term = _sigmoid(gate) * ybr
        merged = term if merged is None else merged + term
    o_ref[...] = h_ref[...] + jnp.dot(merged.astype(BF16), wo_ref[...], preferred_element_type=F32)


def _merge(ya, yb, hc, ug, xn, h, w_in_all, b_in_all, wb_all, wo_all, layer, tm=512):
    n, d = h.shape
    bw = ya.shape[1]
    depth = w_in_all.shape[0]
    gate_block = w_in_all.shape[2] // (3 * d) - 1
    assert (gate_block + 1) * 3 * d == w_in_all.shape[2]
    once = pl.Buffered(1)
    ytile = pl.BlockSpec((tm, bw), lambda t: (t, 0))
    return pl.pallas_call(
        _merge_body,
        out_shape=jax.ShapeDtypeStruct((n, d), F32),
        grid=(n // tm,),
        in_specs=[ytile, ytile, ytile, ytile,
                  pl.BlockSpec((tm, d), lambda t: (t, 0)),
                  pl.BlockSpec((tm, d), lambda t: (t, 0)),
                  pl.BlockSpec((None, d, 3 * d), lambda t: (layer, 0, gate_block), pipeline_mode=once),
                  pl.BlockSpec((None, 1, 3 * d), lambda t: (layer, 0, gate_block)),
                  pl.BlockSpec((None, 3, bw, d), lambda t: (layer, 0, 0, 0), pipeline_mode=once),
                  pl.BlockSpec((None, d, d), lambda t: (layer, 0, 0), pipeline_mode=once)],
        out_specs=pl.BlockSpec((tm, d), lambda t: (t, 0)),
        scratch_shapes=[pltpu.VMEM((d, 3 * d), BF16)],
        compiler_params=_cparams("arbitrary"),
        name="branch_merge",
    )(ya, yb, hc, ug, xn, h, w_in_all, b_in_all.reshape(depth, 1, -1), wb_all, wo_all)


def _ffn_body(h_ref, g_ref, gnext_ref, wg_ref, wu_ref, wd_ref, o_ref, xnext_ref, xn_ref):
    j = pl.program_id(1)

    @pl.when(j == 0)
    def _():
        h = h_ref[...]
        xn_ref[...] = _rms_norm_f32(h, g_ref[...]).astype(BF16)
        o_ref[...] = h

    xn = xn_ref[...]
    gate = jnp.dot(xn, wg_ref[...].astype(BF16), preferred_element_type=F32)
    up = jnp.dot(xn, wu_ref[...].astype(BF16), preferred_element_type=F32)
    act = gate * _sigmoid(gate) * up
    o_ref[...] += jnp.dot(act.astype(BF16), wd_ref[...].astype(BF16), preferred_element_type=F32)

    @pl.when(j == pl.num_programs(1) - 1)
    def _():
        xnext_ref[...] = _rms_norm_f32(o_ref[...], gnext_ref[...]).astype(xnext_ref.dtype)


def _ffn(h, g, g_next, w_gu_all, w_down_all, widx, tm=1024, tf=512):
    n, d = h.shape
    nf = w_down_all.shape[-2] // tf
    gain = pl.BlockSpec((1, d), lambda t, j: (0, 0))
    tile = pl.BlockSpec((tm, d), lambda t, j: (t, 0))
    return pl.pallas_call(
        _ffn_body,
        out_shape=[jax.ShapeDtypeStruct((n, d), F32), jax.ShapeDtypeStruct((n, d), BF16)],
        grid=(n // tm, nf),
        in_specs=[tile, gain, gain,
                  pl.BlockSpec((None, d, tf), lambda t, j: (widx, 0, j)),
                  pl.BlockSpec((None, d, tf), lambda t, j: (widx, 0, j + nf)),
                  pl.BlockSpec((None, tf, d), lambda t, j: (widx, j, 0))],
        out_specs=[tile, tile],
        scratch_shapes=[pltpu.VMEM((tm, d), BF16)],
        compiler_params=_cparams("arbitrary", "arbitrary"),
        name="dense_ffn",
    )(h, g.reshape(1, d), g_next.reshape(1, d), w_gu_all, w_gu_all, w_down_all)


MOE_TOKEN_TILE = 512
MOE_ROW_TILE = 512
MOE_SEG_ALIGN = 16
MOE_TOP_K = 2
MOE_CHUNK_SIZES = tuple(MOE_SEG_ALIGN << b for b in range(5, -1, -1))
assert MOE_CHUNK_SIZES[0] == MOE_TOKEN_TILE


def _moe_compact_rows(ne):
    rows = MOE_TOKEN_TILE * MOE_TOP_K + ne * (MOE_SEG_ALIGN - 1)
    return -(-rows // MOE_SEG_ALIGN) * MOE_SEG_ALIGN


def _moe_sorted_rows(n, ne):
    rows = n * MOE_TOP_K + (n // MOE_TOKEN_TILE) * ne * (MOE_SEG_ALIGN - 1) + ne * (MOE_ROW_TILE - MOE_SEG_ALIGN)
    return -(-rows // MOE_ROW_TILE) * MOE_ROW_TILE


def _router_body(h_ref, g_ref, wrt_ref, xn_ref, posr_ref, wrow_ref, posc_ref, cnt_ref, before_scr, eye_scr):
    t_tokens = h_ref.shape[0]

    @pl.when(pl.program_id(0) == 0)
    def _():
        r_i = lax.broadcasted_iota(jnp.int32, (t_tokens, t_tokens), 0)
        c_i = lax.broadcasted_iota(jnp.int32, (t_tokens, t_tokens), 1)
        before_scr[...] = jnp.where(r_i < c_i, 1.0, 0.0).astype(BF16)
        eye_scr[...] = jnp.where(r_i == c_i, 1.0, 0.0).astype(BF16)

    xn = _rms_norm_f32(h_ref[...], g_ref[...])
    xn_hi = xn.astype(BF16)
    xn_ref[...] = xn_hi
    nt_dims = (((1,), (1,)), ((), ()))
    xn_lo = (xn - xn_hi.astype(F32)).astype(BF16)
    w = wrt_ref[...]
    ne = w.shape[0]
    w_hi = w.astype(BF16)
    w_lo = (w - w_hi.astype(F32)).astype(BF16)
    by_hi = lax.dot_general(jnp.concatenate([w_hi, w_lo], axis=0), xn_hi, nt_dims, preferred_element_type=F32)
    logits = by_hi[:ne] + by_hi[ne:] + lax.dot_general(w_hi, xn_lo, nt_dims, preferred_element_type=F32)
    sub = lax.broadcasted_iota(jnp.int32, logits.shape, 0)
    m1 = jnp.max(logits, axis=0, keepdims=True)
    i1 = jnp.min(jnp.where(logits == m1, sub, ne), axis=0, keepdims=True)
    rest = jnp.where(sub == i1, -jnp.inf, logits)
    m2 = jnp.max(rest, axis=0, keepdims=True)
    i2 = jnp.min(jnp.where(rest == m2, sub, ne), axis=0, keepdims=True)
    e = jnp.exp(m2 - m1)
    wrow_ref[...] = jnp.concatenate([1.0 / (1.0 + e), e / (1.0 + e)], axis=0)

    sel1, sel2 = sub == i1, sub == i2
    memb = jnp.where(sel1, 1.0, jnp.where(sel2, 1.0, 0.0))
    rank = jnp.dot(memb.astype(BF16), before_scr[...], preferred_element_type=F32)
    sub_c = lax.broadcasted_iota(jnp.int32, cnt_ref.shape, 0)
    cnt_out = jnp.zeros(cnt_ref.shape, F32)
    base = rank
    running = jnp.zeros((1, 1), F32)
    for ex in range(ne):
        c = jnp.sum(memb[ex:ex + 1, :], axis=1, keepdims=True)
        cnt_out = jnp.where(sub_c == ex, c, cnt_out)
        base = jnp.where(sub == ex, base + running, base)
        running = running + jnp.floor((c + (MOE_SEG_ALIGN - 1)) * (1.0 / MOE_SEG_ALIGN)) * MOE_SEG_ALIGN
    cnt_ref[...] = cnt_out.astype(jnp.int32)
    pos1 = jnp.sum(jnp.where(sel1, base, 0.0), axis=0, keepdims=True)
    pos2 = jnp.sum(jnp.where(sel2, base, 0.0), axis=0, keepdims=True)
    posr_ref[...] = jnp.concatenate([pos1, pos2], axis=0).astype(jnp.int32)
    digits = []
    for pos in (pos1, pos2):
        hi = jnp.floor(pos * (1.0 / 128.0))
        digits += [hi, pos - 128.0 * hi]
    pad = jnp.zeros((2 * SUBLANES - len(digits), t_tokens), F32)
    dig_t = lax.dot_general(eye_scr[...], jnp.concatenate(digits + [pad], axis=0).astype(BF16), nt_dims,
                            preferred_element_type=F32)
    d_i = lax.broadcasted_iota(jnp.int32, (2 * SUBLANES, posc_ref.shape[1]), 0)
    c_i = lax.broadcasted_iota(jnp.int32, (2 * SUBLANES, posc_ref.shape[1]), 1)
    recombine = jnp.where(d_i == 2 * c_i, 128.0, jnp.where(d_i == 2 * c_i + 1, 1.0, 0.0)).astype(BF16)
    posc = jnp.dot(dig_t.astype(BF16), recombine, preferred_element_type=F32)
    posc_ref[...] = posc.astype(jnp.int32)


def _router(h, g_all, layer, wr_all, widx):
    n, d = h.shape
    depth = g_all.shape[0]
    ne = wr_all.shape[-1]
    tm = MOE_TOKEN_TILE
    nt = n // tm
    wrt = jnp.swapaxes(wr_all, 1, 2)
    return pl.pallas_call(
        _router_body,
        out_shape=[jax.ShapeDtypeStruct((n, d), BF16),
                   jax.ShapeDtypeStruct((MOE_TOP_K, n), jnp.int32),
                   jax.ShapeDtypeStruct((MOE_TOP_K, n), F32),
                   jax.ShapeDtypeStruct((n, SUBLANES), jnp.int32),
                   jax.ShapeDtypeStruct((nt, ne, 128), jnp.int32)],
        grid=(nt,),
        in_specs=[pl.BlockSpec((tm, d), lambda t: (t, 0)),
                  pl.BlockSpec((None, 1, d), lambda t: (layer, 0, 0)),
                  pl.BlockSpec((None, ne, d), lambda t: (widx, 0, 0))],
        out_specs=[pl.BlockSpec((tm, d), lambda t: (t, 0)),
                   pl.BlockSpec((MOE_TOP_K, tm), lambda t: (0, t)),
                   pl.BlockSpec((MOE_TOP_K, tm), lambda t: (0, t)),
                   pl.BlockSpec((tm, SUBLANES), lambda t: (t, 0)),
                   pl.BlockSpec((None, ne, 128), lambda t: (t, 0, 0))],
        scratch_shapes=[pltpu.VMEM((tm, tm), BF16), pltpu.VMEM((tm, tm), BF16)],
        compiler_params=_cparams("arbitrary"),
        name="router",
    )(h, g_all.reshape(depth, 1, d), wrt)


def _moe_tables(cnt, n_sorted_rows):
    cnt = cnt[:, :, 0]
    nt, ne = cnt.shape
    seg = (cnt + (MOE_SEG_ALIGN - 1)) // MOE_SEG_ALIGN * MOE_SEG_ALIGN
    seg_off = jnp.cumsum(seg, axis=1) - seg
    e_rows = jnp.sum(seg, axis=0)
    e_tiles = (e_rows + (MOE_ROW_TILE - 1)) // MOE_ROW_TILE
    e_cum = jnp.cumsum(e_tiles)
    e_base = (e_cum - e_tiles) * MOE_ROW_TILE
    dst = e_base[None, :] + jnp.cumsum(seg, axis=0) - seg
    n_used = e_cum[-1]
    tile_ids = jnp.minimum(jnp.arange(n_sorted_rows // MOE_ROW_TILE, dtype=jnp.int32), n_used - 1)
    tile_expert = jnp.sum(tile_ids[:, None] >= e_cum[None, :], axis=1)
    last_tile_row = e_base + (e_tiles - 1) * MOE_ROW_TILE
    i32 = lambda a: a.astype(jnp.int32)
    return dict(seg_off=i32(seg_off.reshape(-1)), seg_len=i32(seg.reshape(-1)), dst=i32(dst.reshape(-1)),
                n_used=i32(n_used.reshape(1)), tile_expert=i32(tile_expert),
                last_tile_row=i32(last_tile_row), has_rows=i32(e_tiles > 0))


def _moe_chunk_copies(t, ne, seg_off_ref, seg_len_ref, dst_ref, make_copy, act):
    for ex in range(ne):
        idx = t * ne + ex
        off = seg_off_ref[idx]
        ln = seg_len_ref[idx]
        row = dst_ref[idx]
        for size in MOE_CHUNK_SIZES:
            take = (ln & size) != 0

            @pl.when(take)
            def _(off=off, row=row, size=size):
                for cp in make_copy(pl.multiple_of(off, MOE_SEG_ALIGN), pl.multiple_of(row, MOE_SEG_ALIGN), size):
                    act(cp)

            step = jnp.where(take, size, 0)
            off = off + step
            row = row + step


def _dispatch_body(seg_off_ref, seg_len_ref, dst_ref, last_row_ref, has_rows_ref, n_used_ref,
                   xn_ref, posr_ref, wrow_ref, xs_ref, ws_ref, cbuf, wbuf, zx, zw, sems, zsem):
    t = pl.program_id(0)
    n_tiles = pl.num_programs(0)
    ne = last_row_ref.shape[0]
    n_rows, t_tokens = cbuf.shape[1], xn_ref.shape[0]

    def zero_copies(row):
        row = pl.multiple_of(row, MOE_ROW_TILE)
        return (pltpu.make_async_copy(zx, xs_ref.at[pl.ds(row, MOE_ROW_TILE)], zsem),
                pltpu.make_async_copy(zw, ws_ref.at[pl.ds(row, MOE_ROW_TILE)], zsem))

    @pl.when(t == 0)
    def _():
        zx[...] = jnp.zeros_like(zx)
        zw[...] = jnp.zeros_like(zw)
        for act in (lambda cp: cp.start(), lambda cp: cp.wait()):
            for ex in range(ne):
                @pl.when(has_rows_ref[ex] != 0)
                def _(ex=ex):
                    for cp in zero_copies(last_row_ref[ex]):
                        act(cp)

        def tail_body(i, carry):
            for cp in zero_copies(i * MOE_ROW_TILE):
                cp.start()
                cp.wait()
            return carry

        lax.fori_loop(n_used_ref[0], xs_ref.shape[0] // MOE_ROW_TILE, tail_body, 0)

    r_iota = lax.broadcasted_iota(jnp.int32, (n_rows, t_tokens), 0)
    hit1 = r_iota == posr_ref[0:1, :]
    hit2 = r_iota == posr_ref[1:2, :]
    onehot = jnp.where(hit1, 1.0, jnp.where(hit2, 1.0, 0.0)).astype(BF16)
    slot = t % 2
    cbuf[slot] = jnp.dot(onehot, xn_ref[...], preferred_element_type=F32).astype(BF16)
    wsel = jnp.where(hit1, wrow_ref[0:1, :], jnp.where(hit2, wrow_ref[1:2, :], 0.0))
    wbuf[slot] = jnp.broadcast_to(jnp.sum(wsel, axis=1, keepdims=True), wbuf.shape[1:])

    def drain(tile, act):
        s = tile % 2

        def make_copy(off, row, size):
            return (pltpu.make_async_copy(cbuf.at[s, pl.ds(off, size)], xs_ref.at[pl.ds(row, size)], sems.at[s]),
                    pltpu.make_async_copy(wbuf.at[s, pl.ds(off, size)], ws_ref.at[pl.ds(row, size)], sems.at[s]))

        _moe_chunk_copies(tile, ne, seg_off_ref, seg_len_ref, dst_ref, make_copy, act)

    drain(t, lambda cp: cp.start())

    @pl.when(t > 0)
    def _():
        drain(t - 1, lambda cp: cp.wait())

    @pl.when(t == n_tiles - 1)
    def _():
        drain(t, lambda cp: cp.wait())


def _dispatch(xn, posr, wrow, tables, n_sorted_rows, ne):
    n, d = xn.shape
    tm = MOE_TOKEN_TILE
    n_rows = _moe_compact_rows(ne)
    grid_spec = pltpu.PrefetchScalarGridSpec(
        num_scalar_prefetch=6,
        grid=(n // tm,),
        in_specs=[pl.BlockSpec((tm, d), lambda t, *_: (t, 0)),
                  pl.BlockSpec((MOE_TOP_K, tm), lambda t, *_: (0, t)),
                  pl.BlockSpec((MOE_TOP_K, tm), lambda t, *_: (0, t))],
        out_specs=[pl.BlockSpec(memory_space=pl.ANY), pl.BlockSpec(memory_space=pl.ANY)],
        scratch_shapes=[pltpu.VMEM((2, n_rows, d), BF16), pltpu.VMEM((2, n_rows, 128), F32),
                        pltpu.VMEM((MOE_ROW_TILE, d), BF16), pltpu.VMEM((MOE_ROW_TILE, 128), F32),
                        pltpu.SemaphoreType.DMA((2,)), pltpu.SemaphoreType.DMA],
    )
    return pl.pallas_call(
        _dispatch_body,
        out_shape=[jax.ShapeDtypeStruct((n_sorted_rows, d), BF16), jax.ShapeDtypeStruct((n_sorted_rows, 128), F32)],
        grid_spec=grid_spec,
        compiler_params=_cparams("arbitrary"),
        name="moe_dispatch",
    )(tables['seg_off'], tables['seg_len'], tables['dst'], tables['last_tile_row'], tables['has_rows'], tables['n_used'],
      xn, posr, wrow)


def _experts_body(tile_expert_ref, n_used_ref, xs_ref, ws_ref, wgu_ref, wd_ref, ys_ref, wgu16, wd16):
    i = pl.program_id(0)
    used = i < n_used_ref[0]
    prev = tile_expert_ref[jnp.maximum(i - 1, 0)]
    new_expert = jnp.logical_or(i == 0, tile_expert_ref[i] != prev)
    f = wd_ref.shape[0]
    fc = 512

    @pl.when(jnp.logical_and(used, new_expert))
    def _():
        for c in range(2 * f // fc):
            wgu16[:, c * fc:(c + 1) * fc] = wgu_ref[:, c * fc:(c + 1) * fc].astype(BF16)
        for c in range(f // fc):
            wd16[c * fc:(c + 1) * fc, :] = wd_ref[c * fc:(c + 1) * fc, :].astype(BF16)

    @pl.when(used)
    def _():
        x = xs_ref[...]
        w = ws_ref[...]
        wrep = jnp.concatenate([w] * (fc // w.shape[1]), axis=1)
        acc = None
        for c in range(f // fc):
            gate = jnp.dot(x, wgu16[:, c * fc:(c + 1) * fc], preferred_element_type=F32)
            up = jnp.dot(x, wgu16[:, f + c * fc:f + (c + 1) * fc], preferred_element_type=F32)
            act = (gate * _sigmoid(gate) * up * wrep).astype(BF16)
            part = jnp.dot(act, wd16[c * fc:(c + 1) * fc, :], preferred_element_type=F32)
            acc = part if acc is None else acc + part
        ys_ref[...] = acc.astype(ys_ref.dtype)

    @pl.when(jnp.logical_not(used))
    def _():
        ys_ref[...] = jnp.zeros_like(ys_ref)


def _experts(xs, ws, tables, w_gu_all, w_down_all, widx):
    rows, d = xs.shape
    f = w_down_all.shape[-2]
    tile = lambda w: pl.BlockSpec((MOE_ROW_TILE, w), lambda i, te, nu: (jnp.maximum(jnp.minimum(i, nu[0] - 1), 0), 0))
    grid_spec = pltpu.PrefetchScalarGridSpec(
        num_scalar_prefetch=2,
        grid=(rows // MOE_ROW_TILE,),
        in_specs=[tile(d), tile(ws.shape[1]),
                  pl.BlockSpec((None, None, d, 2 * f), lambda i, te, nu: (widx, te[i], 0, 0)),
                  pl.BlockSpec((None, None, f, d), lambda i, te, nu: (widx, te[i], 0, 0), pipeline_mode=pl.Buffered(1))],
        out_specs=pl.BlockSpec((MOE_ROW_TILE, d), lambda i, te, nu: (i, 0)),
        scratch_shapes=[pltpu.VMEM((d, 2 * f), BF16), pltpu.VMEM((f, d), BF16)],
    )
    return pl.pallas_call(
        _experts_body,
        out_shape=jax.ShapeDtypeStruct((rows, d), BF16),
        grid_spec=grid_spec,
        compiler_params=_cparams("arbitrary"),
        name="moe_experts",
    )(tables['tile_expert'], tables['n_used'], xs, ws, w_gu_all, w_down_all)


def _combine_body(seg_off_ref, seg_len_ref, dst_ref, ys_ref, posc_ref, h_ref, gain_ref, *rest, ne, final):
    *out_refs, ybuf, sems = rest
    t = pl.program_id(0)
    n_tiles = pl.num_programs(0)
    t_tokens, n_rows = h_ref.shape[0], ybuf.shape[1]

    def fetch(tile, act):
        slot = tile % 2

        def make_copy(off, row, size):
            return (pltpu.make_async_copy(ys_ref.at[pl.ds(row, size)], ybuf.at[slot, pl.ds(off, size)], sems.at[slot]),)

        _moe_chunk_copies(tile, ne, seg_off_ref, seg_len_ref, dst_ref, make_copy, act)

    @pl.when(t == 0)
    def _():
        ybuf[...] = jnp.zeros_like(ybuf)
        fetch(t, lambda cp: cp.start())

    @pl.when(t + 1 < n_tiles)
    def _():
        fetch(t + 1, lambda cp: cp.start())

    fetch(t, lambda cp: cp.wait())

    pos = posc_ref[...]
    lane_r = lax.broadcasted_iota(jnp.int32, (t_tokens, n_rows), 1)
    onehot = jnp.where(lane_r == pos[:, 0:1], 1.0, jnp.where(lane_r == pos[:, 1:2], 1.0, 0.0)).astype(BF16)
    out = h_ref[...] + jnp.dot(onehot, ybuf[t % 2], preferred_element_type=F32)
    normed = _rms_norm_f32(out, gain_ref[...])
    if final:
        out_refs[0][...] = normed
    else:
        out_refs[0][...] = out
        out_refs[1][...] = normed.astype(out_refs[1].dtype)


def _combine(ys, posc, h, tables, ne, gain, final):
    n, d = h.shape
    tm = MOE_TOKEN_TILE
    tile = pl.BlockSpec((tm, d), lambda t, *_: (t, 0))
    grid_spec = pltpu.PrefetchScalarGridSpec(
        num_scalar_prefetch=3,
        grid=(n // tm,),
        in_specs=[pl.BlockSpec(memory_space=pl.ANY),
                  pl.BlockSpec((tm, posc.shape[1]), lambda t, *_: (t, 0)),
                  tile,
                  pl.BlockSpec((1, d), lambda t, *_: (0, 0))],
        out_specs=[tile] if final else [tile, tile],
        scratch_shapes=[pltpu.VMEM((2, _moe_compact_rows(ne), d), BF16), pltpu.SemaphoreType.DMA((2,))],
    )
    res = jax.ShapeDtypeStruct((n, d), F32)
    return pl.pallas_call(
        functools.partial(_combine_body, ne=ne, final=final),
        out_shape=[res] if final else [res, jax.ShapeDtypeStruct((n, d), BF16)],
        grid_spec=grid_spec,
        compiler_params=_cparams("arbitrary"),
        name="moe_combine",
    )(tables['seg_off'], tables['seg_len'], tables['dst'], ys, posc, h, gain.reshape(1, d))


def _moe(h, g_all, layer, wr_all, w_gu_all, w_down_all, widx, gain, final):
    n = h.shape[0]
    ne = wr_all.shape[-1]
    xn, posr, wrow, posc, cnt = _router(h, g_all, layer, wr_all, widx)
    n_sorted_rows = _moe_sorted_rows(n, ne)
    tables = _moe_tables(cnt, n_sorted_rows)
    xs, ws = _dispatch(xn, posr, wrow, tables, n_sorted_rows, ne)
    ys = _experts(xs, ws, tables, w_gu_all, w_down_all, widx)
    return _combine(ys, posc, h, tables, ne, gain, final)


def kernel(x, norm_mix_g, w_in, b_in, na_rpb, conv_w, conv_b, lru_wa, lru_ba, lru_wx, lru_bx, lru_lambda, w_branch, w_out, norm_ffn_g, ffn_w_gu, ffn_w_down, router_w, moe_w_gu, moe_w_down, final_g):
    batch, seq, d = x.shape
    depth = w_in.shape[0]
    n = batch * seq
    bw = w_branch.shape[2]
    h = x.reshape(n, d)

    tables = _na_bias_tables(na_rpb)
    twiddles = _fourier_twiddles(seq)
    w_gate = _lru_gate_weights(lru_wa, lru_wx)
    wb16 = w_branch.astype(BF16)
    wo16 = w_out.astype(BF16)
    tn = 3 * bw

    xn = _rms_norm(h, norm_mix_g[0], BF16)
    for l in range(depth):
        last = l == depth - 1
        q, k, v = _proj(xn, w_in, b_in, l, 0, tn, 3)
        u_f, u_x, u_g = _proj(xn, w_in, b_in, l, 1, tn, 3)
        y_a = _neighbourhood_attention(q, k, v, tables, l, batch)
        y_b = _fourier_mix(u_f, batch, twiddles)
        h_c = _recurrent_branch(u_x, conv_w, conv_b, w_gate, lru_ba, lru_bx, lru_lambda, l, batch)
        h = _merge(y_a, y_b, h_c, u_g, xn, h, w_in, b_in, wb16, wo16, l)
        next_gain = final_g if last else norm_mix_g[l + 1]
        if l % 2 == 0:
            h, xn = _ffn(h, norm_ffn_g[l], next_gain, ffn_w_gu, ffn_w_down, l // 2)
            if last:
                h = _rms_norm(h, final_g, F32)
        elif last:
            (h,) = _moe(h, norm_ffn_g, l, router_w, moe_w_gu, moe_w_down, l // 2, next_gain, final=True)
        else:
            h, xn = _moe(h, norm_ffn_g, l, router_w, moe_w_gu, moe_w_down, l // 2, next_gain, final=False)
    return h.reshape(batch, seq, d)
```

```python
import functools

import numpy as np
import jax
import jax.numpy as jnp
from jax import lax
from jax.experimental import pallas as pl
from jax.experimental.pallas import tpu as pltpu

F32 = jnp.float32
BF16 = jnp.bfloat16

RMS_EPS = 1e-6
GRID_W = 64
NA_HEADS = 8
NA_HEAD_DIM = 64
NA_KH = 8
NA_KW = 16
NA_ROWS_PER_BLOCK = 8
NA_ROWS_IN_FLIGHT = 4
LRU_C = 8.0
CONV_W = 4
CONV_PAD_LEFT = 2
N_EXPERTS = 8
MASK_VALUE = -1e30

VMEM_LIMIT_BYTES = 56 * 1024 * 1024


def _cparams(*sem):
    return pltpu.CompilerParams(dimension_semantics=sem, vmem_limit_bytes=VMEM_LIMIT_BYTES)


def _rms_norm_f32(x, g):
    ms = jnp.mean(x * x, axis=-1, keepdims=True)
    return x * lax.rsqrt(ms + RMS_EPS) * g


def _sigmoid(x):
    return 1.0 / (1.0 + jnp.exp(-x))


def _rms_norm_body(h_ref, g_ref, o_ref):
    o_ref[...] = _rms_norm_f32(h_ref[...], g_ref[...]).astype(o_ref.dtype)


def _rms_norm(h, g, out_dtype, tm=1024):
    n, d = h.shape
    return pl.pallas_call(
        _rms_norm_body,
        out_shape=jax.ShapeDtypeStruct((n, d), out_dtype),
        grid=(n // tm,),
        in_specs=[pl.BlockSpec((tm, d), lambda t: (t, 0)), pl.BlockSpec((1, d), lambda t: (0, 0))],
        out_specs=pl.BlockSpec((tm, d), lambda t: (t, 0)),
        compiler_params=_cparams("arbitrary"),
        name="rms_norm",
    )(h, g.reshape(1, d))


def _proj_body(x_ref, w_ref, b_ref, *out_refs):
    r = jnp.dot(x_ref[...], w_ref[...].astype(BF16), preferred_element_type=F32) + b_ref[...]
    width = r.shape[1] // len(out_refs)
    for i, o in enumerate(out_refs):
        o[...] = r[:, i * width:(i + 1) * width].astype(o.dtype)


def _proj(xn, w_all, b_all, layer, col_block, tn, n_out, tm=2048):
    n, d = xn.shape
    depth = w_all.shape[0]
    b3 = b_all.reshape(depth, 1, -1)
    width = tn // n_out
    return pl.pallas_call(
        _proj_body,
        out_shape=[jax.ShapeDtypeStruct((n, width), BF16) for _ in range(n_out)],
        grid=(n // tm,),
        in_specs=[
            pl.BlockSpec((tm, d), lambda t: (t, 0)),
            pl.BlockSpec((None, d, tn), lambda t: (layer, 0, col_block)),
            pl.BlockSpec((None, 1, tn), lambda t: (layer, 0, col_block)),
        ],
        out_specs=[pl.BlockSpec((tm, width), lambda t: (t, 0)) for _ in range(n_out)],
        compiler_params=_cparams("arbitrary"),
        name="in_proj",
    )(xn, w_all, b3)


def _na_bias_tables(rpb_all):
    cols = np.arange(GRID_W)
    col_start = np.clip(cols - NA_KW // 2, 0, GRID_W - NA_KW)
    cc = np.arange(GRID_W)[None, :]
    in_win = (cc >= col_start[:, None]) & (cc < col_start[:, None] + NA_KW)
    col_off = cc - cols[:, None] + (NA_KW - 1)
    onehot = np.zeros((2 * NA_KW - 1, GRID_W, GRID_W), np.float32)
    cq, ck = np.nonzero(in_win)
    onehot[col_off[cq, ck], cq, ck] = 1.0
    t = jnp.einsum('lhro,ocd->lhrcd', rpb_all.astype(F32), jnp.asarray(onehot),
                   precision=lax.Precision.HIGHEST)
    t = jnp.where(jnp.asarray(in_win)[None, None, None], t, MASK_VALUE)
    depth, heads, n_off = t.shape[:3]
    t = t.reshape(depth, heads // 2, 2, n_off, GRID_W, GRID_W)
    t = jnp.transpose(t, (0, 1, 3, 2, 4, 5)).reshape(depth, heads // 2, n_off, 2 * GRID_W, GRID_W)
    return jnp.concatenate([t[:, :, :-1], t[:, :, 1:]], axis=-1)


def _na_body(q_ref, kbuf, vbuf, tbl_ref, o_ref, s_scr, e_scr, *, n_blocks):
    j = pl.program_id(1)
    band = NA_KH * GRID_W
    half = NA_KH // 2
    lane = lax.broadcasted_iota(jnp.int32, (GRID_W, 2 * NA_HEAD_DIM), 1)
    lo = lane < NA_HEAD_DIM
    qscale = NA_HEAD_DIM ** -0.5
    head_mask = (jnp.where(lo, qscale, 0.0).astype(BF16), jnp.where(lo, 0.0, qscale).astype(BF16))

    n_pairs = NA_HEADS // 2
    cols = [slice(p * 2 * NA_HEAD_DIM, (p + 1) * 2 * NA_HEAD_DIM) for p in range(n_pairs)]

    def row_offsets(rl):
        start_first = jnp.maximum(rl - half, 0)
        start_last = jnp.minimum(rl + half, NA_ROWS_PER_BLOCK)
        start = jnp.where(j == 0, start_first, jnp.where(j == n_blocks - 1, start_last, rl))
        delta = jnp.where(j == 0, jnp.minimum(rl, half), jnp.where(j == n_blocks - 1, jnp.maximum(rl, half), half))
        return pl.multiple_of(rl * GRID_W, GRID_W), pl.multiple_of(start * GRID_W, GRID_W), delta

    def scores(rl, slot):
        qrow, krow, delta = row_offsets(rl)
        for p in range(n_pairs):
            q2 = q_ref[pl.ds(qrow, GRID_W), cols[p]]
            qs = jnp.concatenate([q2 * head_mask[0], q2 * head_mask[1]], axis=0)
            k2 = kbuf[pl.ds(krow, band), cols[p]]
            s = lax.dot_general(qs, k2, (((1,), (1,)), ((), ())), preferred_element_type=F32)
            bias = jnp.concatenate([tbl_ref[p, 2 * m - delta + (NA_KH - 1)] for m in range(NA_KH // 2)], axis=1)
            s_scr[slot, p] = s + bias

    def softmax(slot):
        inv_l = []
        for p in range(n_pairs):
            s = s_scr[slot, p]
            m = jnp.max(s, axis=-1, keepdims=True)
            e = jnp.exp(s - m)
            inv_l.append(1.0 / jnp.sum(e, axis=-1, keepdims=True))
            e_scr[slot, p] = e.astype(BF16)
        return inv_l

    def weighted_values(rl, slot, inv_l):
        qrow, krow, _ = row_offsets(rl)
        for p in range(n_pairs):
            v2 = vbuf[pl.ds(krow, band), cols[p]]
            o = jnp.dot(e_scr[slot, p], v2, preferred_element_type=F32) * inv_l[p]
            o_ref[pl.ds(qrow, GRID_W), cols[p]] = jnp.where(lo, o[:GRID_W], o[GRID_W:]).astype(o_ref.dtype)

    def rows_body(it, carry):
        rows = [it * NA_ROWS_IN_FLIGHT + r for r in range(NA_ROWS_IN_FLIGHT)]
        for slot, rl in enumerate(rows):
            scores(rl, slot)
        inv = [softmax(slot) for slot in range(NA_ROWS_IN_FLIGHT)]
        for slot, rl in enumerate(rows):
            weighted_values(rl, slot, inv[slot])
        return carry

    lax.fori_loop(0, NA_ROWS_PER_BLOCK // NA_ROWS_IN_FLIGHT, rows_body, 0)


def _neighbourhood_attention(q, k, v, tables, layer, batch):
    n, width = q.shape
    n_pairs = NA_HEADS // 2
    blk = NA_ROWS_PER_BLOCK * GRID_W
    n_blocks = n // batch // blk
    assert n_blocks >= 2 and NA_KH == NA_ROWS_PER_BLOCK
    s = n // batch
    half_blk = blk // 2

    def window_start(b, j):
        start = b * s + jnp.clip(j * blk - half_blk, 0, s - 2 * blk)
        return pl.multiple_of(start, half_blk), 0

    tile = pl.BlockSpec((blk, width), lambda b, j: (b * n_blocks + j, 0))
    window = pl.BlockSpec((pl.Element(2 * blk), pl.Element(width)), window_start)
    return pl.pallas_call(
        functools.partial(_na_body, n_blocks=n_blocks),
        out_shape=jax.ShapeDtypeStruct((n, width), BF16),
        grid=(batch, n_blocks),
        in_specs=[tile, window, window,
                  pl.BlockSpec((None,) + tables.shape[1:], lambda b, j: (layer, 0, 0, 0, 0))],
        out_specs=tile,
        scratch_shapes=[pltpu.VMEM((NA_ROWS_IN_FLIGHT, n_pairs, 2 * GRID_W, NA_KH * GRID_W), F32),
                        pltpu.VMEM((NA_ROWS_IN_FLIGHT, n_pairs, 2 * GRID_W, NA_KH * GRID_W), BF16)],
        compiler_params=_cparams("arbitrary", "arbitrary"),
        name="neigh_attn",
    )(q, k, v, tables)


FN_N2 = 128
FN_GROUP_DIM = 64
FN_GROUP = 16


def _dft_cos_sin(n):
    ang = 2.0 * np.pi * (np.outer(np.arange(n), np.arange(n)) % n) / n
    return np.cos(ang), np.sin(ang)


def _fourier_stage1_body(x_ref, f_ref, tc_ref, ts_ref, zr_ref, zi_ref):
    n1, n_b, width = x_ref.shape
    xs = jnp.swapaxes(x_ref[...], 0, 1)
    rep = width // tc_ref.shape[2]
    zr_all, zi_all = [], []
    for b in range(n_b):
        z = jnp.dot(f_ref[...], xs[b], preferred_element_type=F32)
        zr, zi = z[:n1], z[n1:]
        tc = jnp.concatenate([tc_ref[b]] * rep, axis=1)
        ts = jnp.concatenate([ts_ref[b]] * rep, axis=1)
        zr_all.append((zr * tc + zi * ts).astype(zr_ref.dtype))
        zi_all.append((zi * tc - zr * ts).astype(zi_ref.dtype))
    zr_ref[...] = jnp.swapaxes(jnp.stack(zr_all, axis=0), 0, 1)
    zi_ref[...] = jnp.swapaxes(jnp.stack(zi_all, axis=0), 0, 1)


def _fourier_stage2_body(zr_ref, zi_ref, f_ref, c_ref, o_ref, *, scale):
    n2 = zr_ref.shape[1]
    outs = []
    for i in range(zr_ref.shape[0]):
        z = jnp.concatenate([zr_ref[i], zi_ref[i]], axis=0)
        y = jnp.dot(f_ref[...], z, preferred_element_type=F32)
        yc = jnp.concatenate([y[:n2], y[n2:]], axis=1).astype(BF16)
        out = jnp.dot(yc, c_ref[...], preferred_element_type=F32) * scale
        outs.append(out.astype(o_ref.dtype))
    o_ref[...] = jnp.swapaxes(jnp.stack(outs, axis=0), 0, 1)


def _fourier_mix(u, batch, twiddles):
    n, width = u.shape
    s = n // batch
    n2 = FN_N2
    n1 = s // n2
    tc, ts = twiddles
    c1, s1 = _dft_cos_sin(n1)
    f1 = jnp.asarray(np.concatenate([c1, -s1], axis=0), F32).astype(BF16)
    c2, s2 = _dft_cos_sin(n2)
    f2 = jnp.asarray(np.block([[c2, s2], [-s2, c2]]), F32).astype(BF16)
    cg, sg = _dft_cos_sin(FN_GROUP_DIM)
    eye = np.eye(width // FN_GROUP_DIM)
    fc = jnp.asarray(np.concatenate([np.kron(eye, cg), np.kron(eye, sg)], axis=0), F32).astype(BF16)

    grp = FN_GROUP
    pos_blk = pl.BlockSpec((None, n1, grp, width), lambda b, j: (b, 0, j, 0))
    tw_blk = pl.BlockSpec((grp,) + tc.shape[1:], lambda b, j: (j, 0, 0))
    zr, zi = pl.pallas_call(
        _fourier_stage1_body,
        out_shape=[jax.ShapeDtypeStruct((batch, n1, n2, width), BF16)] * 2,
        grid=(batch, n2 // grp),
        in_specs=[pos_blk, pl.BlockSpec((2 * n1, n1), lambda b, j: (0, 0)), tw_blk, tw_blk],
        out_specs=[pos_blk, pos_blk],
        compiler_params=_cparams("arbitrary", "arbitrary"),
        name="fourier_stage1",
    )(u.reshape(batch, n1, n2, width), f1, tc, ts)

    freq_blk = pl.BlockSpec((None, grp, n2, width), lambda b, j: (b, j, 0, 0))
    out = pl.pallas_call(
        functools.partial(_fourier_stage2_body, scale=float(1.0 / np.sqrt(s * FN_GROUP_DIM))),
        out_shape=jax.ShapeDtypeStruct((batch, n2, n1, width), BF16),
        grid=(batch, n1 // grp),
        in_specs=[freq_blk, freq_blk,
                  pl.BlockSpec((2 * n2, 2 * n2), lambda b, j: (0, 0)),
                  pl.BlockSpec((2 * width, width), lambda b, j: (0, 0))],
        out_specs=pl.BlockSpec((None, n2, grp, width), lambda b, j: (b, 0, j, 0)),
        compiler_params=_cparams("arbitrary", "arbitrary"),
        name="fourier_stage2",
    )(zr, zi, f2, fc)
    return out.reshape(n, width)


def _fourier_twiddles(s):
    n2 = FN_N2
    n1 = s // n2
    ang = (2.0 * np.pi / s) * (jnp.arange(n2, dtype=F32)[:, None] * jnp.arange(n1, dtype=F32)[None, :])
    rep = lambda t: jnp.broadcast_to(t[:, :, None], (n2, n1, 128))
    return rep(jnp.cos(ang)), rep(jnp.sin(ang))


LRU_LANES = 128
LRU_SEGMENTS = 16
LRU_JCHUNK = 32
SUBLANES = 8


def _lru_gate_weights(wa_all, wx_all):
    depth, _, nb, db, _ = wa_all.shape
    ncol = nb // 2

    def blockdiag(w):
        w = w.reshape(depth, ncol, 2, db, db)
        z = jnp.zeros_like(w[:, :, 0])
        top = jnp.concatenate([w[:, :, 0], z], axis=-1)
        bot = jnp.concatenate([z, w[:, :, 1]], axis=-1)
        return jnp.concatenate([top, bot], axis=-2)

    parts = [blockdiag(wa_all[:, 0]), blockdiag(wx_all[:, 0]), blockdiag(wa_all[:, 1]), blockdiag(wx_all[:, 1])]
    return jnp.concatenate(parts, axis=-1).astype(BF16)


def _gelu_tanh(x):
    return 0.5 * x * (1.0 + jnp.tanh(np.sqrt(2.0 / np.pi) * (x + 0.044715 * (x * x * x))))


def _lru_body(useq_ref, cw_ref, cb_ref, w_ref, ba_ref, bx_ref, lam_ref, oseq_ref, ux_ref, o_ref, h_scr, p_scr):
    n_j, n_g, lanes = ux_ref.shape
    ux_ref[...] = jnp.swapaxes(useq_ref[...].reshape(n_g, n_j, lanes), 0, 1)
    jc = LRU_JCHUNK
    n_chunks = n_j // jc
    seg = lax.broadcasted_iota(jnp.int32, (n_g, lanes), 0)
    seg3 = lax.broadcasted_iota(jnp.int32, (jc, n_g, lanes), 1)
    row3 = lax.broadcasted_iota(jnp.int32, (jc, n_g, lanes), 0)

    def from_prev_segment(x):
        return jnp.where(seg >= 1, pltpu.roll(x, 1, axis=0), 0.0)

    def from_next_segment(x):
        return jnp.where(seg < n_g - 1, pltpu.roll(x, n_g - 1, axis=0), 0.0)

    def conv_chunk(j0):
        main = ux_ref[pl.ds(j0, jc)].astype(F32)
        lo_in = ux_ref[pl.ds(jnp.maximum(j0 - CONV_PAD_LEFT, 0), CONV_PAD_LEFT)].astype(F32)
        tail = ux_ref[n_j - CONV_PAD_LEFT:n_j].astype(F32)
        lo_wrap = jnp.stack([from_prev_segment(tail[r]) for r in range(CONV_PAD_LEFT)], axis=0)
        lo = jnp.where(j0 > 0, lo_in, lo_wrap)
        n_hi = CONV_W - 1 - CONV_PAD_LEFT
        hi_in = ux_ref[pl.ds(jnp.minimum(j0 + jc, n_j - n_hi), n_hi)].astype(F32)
        head = ux_ref[0:n_hi].astype(F32)
        hi_wrap = jnp.stack([from_next_segment(head[r]) for r in range(n_hi)], axis=0)
        hi = jnp.where(j0 + jc < n_j, hi_in, hi_wrap)
        ext = jnp.concatenate([lo, main, hi], axis=0)
        c = ext[0:jc] * cw_ref[0:1, :] + cb_ref[...]
        for tap in range(1, CONV_W):
            c = c + ext[tap:tap + jc] * cw_ref[tap:tap + 1, :]
        return c

    def gates(c, pre, d, j0):
        r = _sigmoid(pre[:, 2 * d * lanes:(2 * d + 1) * lanes] + ba_ref[d:d + 1, :])
        i = _sigmoid(pre[:, (2 * d + 1) * lanes:(2 * d + 2) * lanes] + bx_ref[d:d + 1, :])
        lam = lam_ref[d:d + 1, :]
        softplus = jnp.maximum(-lam, 0.0) + jnp.log(1.0 + jnp.exp(-jnp.abs(lam)))
        a = jnp.exp(-LRU_C * r * softplus)
        om = 1.0 - a * a
        mult = jnp.where(om > 0.0, om * lax.rsqrt(om), 0.0)
        a = a.reshape(jc, n_g, lanes)
        gain = (mult * i).reshape(jc, n_g, lanes)
        first = (seg3 == (n_g - 1) * d) & (row3 + j0 == (n_j - 1) * d)
        return a, jnp.where(first, i.reshape(jc, n_g, lanes), gain) * c

    def local_scan(a, b, carry, d):
        h, p = carry
        hs, ps = [None] * jc, [None] * jc
        for jj in (range(jc) if d == 0 else range(jc - 1, -1, -1)):
            h = a[jj] * h + b[jj]
            p = a[jj] * p
            hs[jj], ps[jj] = h, p
        return jnp.stack(hs, axis=0), jnp.stack(ps, axis=0), (h, p)

    scan_init = (jnp.zeros((n_g, lanes), F32), jnp.ones((n_g, lanes), F32))

    def gate_and_forward_body(ci, carry):
        j0 = ci * jc
        rows = pl.ds(j0, jc)
        c = conv_chunk(j0)
        pre = jnp.dot(c.reshape(jc * n_g, lanes).astype(BF16), w_ref[...], preferred_element_type=F32)
        a_b, b_b = gates(c, pre, 1, j0)
        p_scr[1, rows] = a_b
        h_scr[1, rows] = b_b
        a_f, b_f = gates(c, pre, 0, j0)
        h_scr[0, rows], p_scr[0, rows], carry = local_scan(a_f, b_f, carry, 0)
        return carry

    lax.fori_loop(0, n_chunks, gate_and_forward_body, scan_init)

    def backward_body(ci, carry):
        rows = pl.ds((n_chunks - 1 - ci) * jc, jc)
        h_scr[1, rows], p_scr[1, rows], carry = local_scan(p_scr[1, rows], h_scr[1, rows], carry, 1)
        return carry

    lax.fori_loop(0, n_chunks, backward_body, scan_init)

    def carry_in(d):
        edge = n_j - 1 if d == 0 else 0
        h_end, p_end = h_scr[d, edge], p_scr[d, edge]
        state = jnp.zeros((1, lanes), F32)
        out = jnp.zeros((n_g, lanes), F32)
        for g in (range(n_g) if d == 0 else range(n_g - 1, -1, -1)):
            out = jnp.where(seg == g, state, out)
            state = h_end[g:g + 1] + p_end[g:g + 1] * state
        return out

    e_fwd, e_bwd = carry_in(0), carry_in(1)

    def out_body(ci, carry):
        j0 = ci * jc
        rows = pl.ds(j0, jc)
        h = h_scr[0, rows] + p_scr[0, rows] * e_fwd + h_scr[1, rows] + p_scr[1, rows] * e_bwd
        o_ref[rows] = h.astype(o_ref.dtype)
        return carry

    lax.fori_loop(0, n_chunks, out_body, 0)
    oseq_ref[...] = jnp.swapaxes(o_ref[...], 0, 1).reshape(n_g * n_j, lanes)


def _recurrent_branch(u_x, conv_w, conv_b, w_gate, ba, bx, lam, layer, batch):
    n, width = u_x.shape
    s = n // batch
    depth = conv_w.shape[0]
    ncol = width // LRU_LANES
    n_g = LRU_SEGMENTS
    n_j = s // n_g
    assert n_j % LRU_JCHUNK == 0
    cb3 = conv_b.reshape(depth, 1, width)
    seq_spec = pl.BlockSpec((None, s, LRU_LANES), lambda b, c: (b, 0, c))
    par = lambda rows: pl.BlockSpec((None, rows, LRU_LANES), lambda b, c: (layer, 0, c))
    seg_copy = pltpu.VMEM((n_j, n_g, LRU_LANES), BF16)
    state = pltpu.VMEM((2, n_j, n_g, LRU_LANES), F32)
    out = pl.pallas_call(
        _lru_body,
        out_shape=jax.ShapeDtypeStruct((batch, s, width), BF16),
        grid=(batch, ncol),
        in_specs=[seq_spec, par(CONV_W), par(1),
                  pl.BlockSpec((None, None, LRU_LANES, 4 * LRU_LANES), lambda b, c: (layer, c, 0, 0)),
                  par(2), par(2), par(2)],
        out_specs=seq_spec,
        scratch_shapes=[seg_copy, seg_copy, state, state],
        compiler_params=_cparams("arbitrary", "arbitrary"),
        name="rg_lru",
    )(u_x.reshape(batch, s, width), conv_w, cb3, w_gate, ba, bx, lam)
    return out.reshape(n, width)


def _merge_body(ya_ref, yb_ref, hc_ref, ug_ref, xn_ref, h_ref, wgate_ref, bgate_ref, wb_ref, wo_ref, o_ref, wgate16):
    d = h_ref.shape[1]

    @pl.when(pl.program_id(0) == 0)
    def _():
        for kbr in range(wgate16.shape[1] // d):
            wgate16[:, kbr * d:(kbr + 1) * d] = wgate_ref[:, kbr * d:(kbr + 1) * d].astype(BF16)

    xn = xn_ref[...]
    yc = (hc_ref[...].astype(F32) * _gelu_tanh(ug_ref[...].astype(F32))).astype(BF16)
    merged = None
    for kbr, y in enumerate((ya_ref[...], yb_ref[...], yc)):
        cols = slice(kbr * d, (kbr + 1) * d)
        gate = jnp.dot(xn, wgate16[:, cols], preferred_element_type=F32) + bgate_ref[:, cols]
        ybr = jnp.dot(y, wb_ref[kbr], preferred_element_type=F32)
        term = _sigmoid(gate) * ybr
        merged = term if merged is None else merged + term
    o_ref[...] = h_ref[...] + jnp.dot(merged.astype(BF16), wo_ref[...], preferred_element_type=F32)


def _merge(ya, yb, hc, ug, xn, h, w_in_all, b_in_all, wb_all, wo_all, layer, tm=512):
    n, d = h.shape
    bw = ya.shape[1]
    depth = w_in_all.shape[0]
    gate_block = w_in_all.shape[2] // (3 * d) - 1
    assert (gate_block + 1) * 3 * d == w_in_all.shape[2]
    once = pl.Buffered(1)
    ytile = pl.BlockSpec((tm, bw), lambda t: (t, 0))
    return pl.pallas_call(
        _merge_body,
        out_shape=jax.ShapeDtypeStruct((n, d), F32),
        grid=(n // tm,),
        in_specs=[ytile, ytile, ytile, ytile,
                  pl.BlockSpec((tm, d), lambda t: (t, 0)),
                  pl.BlockSpec((tm, d), lambda t: (t, 0)),
                  pl.BlockSpec((None, d, 3 * d), lambda t: (layer, 0, gate_block), pipeline_mode=once),
                  pl.BlockSpec((None, 1, 3 * d), lambda t: (layer, 0, gate_block)),
                  pl.BlockSpec((None, 3, bw, d), lambda t: (layer, 0, 0, 0), pipeline_mode=once),
                  pl.BlockSpec((None, d, d), lambda t: (layer, 0, 0), pipeline_mode=once)],
        out_specs=pl.BlockSpec((tm, d), lambda t: (t, 0)),
        scratch_shapes=[pltpu.VMEM((d, 3 * d), BF16)],
        compiler_params=_cparams("arbitrary"),
        name="branch_merge",
    )(ya, yb, hc, ug, xn, h, w_in_all, b_in_all.reshape(depth, 1, -1), wb_all, wo_all)


def _ffn_body(h_ref, g_ref, gnext_ref, wg_ref, wu_ref, wd_ref, o_ref, xnext_ref, xn_ref):
    j = pl.program_id(1)

    @pl.when(j == 0)
    def _():
        h = h_ref[...]
        xn_ref[...] = _rms_norm_f32(h, g_ref[...]).astype(BF16)
        o_ref[...] = h

    xn = xn_ref[...]
    gate = jnp.dot(xn, wg_ref[...].astype(BF16), preferred_element_type=F32)
    up = jnp.dot(xn, wu_ref[...].astype(BF16), preferred_element_type=F32)
    act = gate * _sigmoid(gate) * up
    o_ref[...] += jnp.dot(act.astype(BF16), wd_ref[...].astype(BF16), preferred_element_type=F32)

    @pl.when(j == pl.num_programs(1) - 1)
    def _():
        xnext_ref[...] = _rms_norm_f32(o_ref[...], gnext_ref[...]).astype(xnext_ref.dtype)


def _ffn(h, g, g_next, w_gu_all, w_down_all, widx, tm=1024, tf=768):
    n, d = h.shape
    nf = w_down_all.shape[-2] // tf
    gain = pl.BlockSpec((1, d), lambda t, j: (0, 0))
    tile = pl.BlockSpec((tm, d), lambda t, j: (t, 0))
    return pl.pallas_call(
        _ffn_body,
        out_shape=[jax.ShapeDtypeStruct((n, d), F32), jax.ShapeDtypeStruct((n, d), BF16)],
        grid=(n // tm, nf),
        in_specs=[tile, gain, gain,
                  pl.BlockSpec((None, d, tf), lambda t, j: (widx, 0, j)),
                  pl.BlockSpec((None, d, tf), lambda t, j: (widx, 0, j + nf)),
                  pl.BlockSpec((None, tf, d), lambda t, j: (widx, j, 0))],
        out_specs=[tile, tile],
        scratch_shapes=[pltpu.VMEM((tm, d), BF16)],
        compiler_params=_cparams("arbitrary", "arbitrary"),
        name="dense_ffn",
    )(h, g.reshape(1, d), g_next.reshape(1, d), w_gu_all, w_gu_all, w_down_all)


MOE_TOKEN_TILE = 512
MOE_ROW_TILE = 512
MOE_SEG_ALIGN = 16
MOE_TOP_K = 2
MOE_CHUNK_SIZES = tuple(MOE_SEG_ALIGN << b for b in range(5, -1, -1))
assert MOE_CHUNK_SIZES[0] == MOE_TOKEN_TILE


def _moe_compact_rows(ne):
    rows = MOE_TOKEN_TILE * MOE_TOP_K + ne * (MOE_SEG_ALIGN - 1)
    return -(-rows // MOE_SEG_ALIGN) * MOE_SEG_ALIGN


def _moe_sorted_rows(n, ne):
    rows = n * MOE_TOP_K + (n // MOE_TOKEN_TILE) * ne * (MOE_SEG_ALIGN - 1) + ne * (MOE_ROW_TILE - MOE_SEG_ALIGN)
    return -(-rows // MOE_ROW_TILE) * MOE_ROW_TILE


def _router_body(h_ref, g_ref, wrt_ref, xn_ref, posr_ref, wrow_ref, posc_ref, cnt_ref, before_scr, eye_scr):
    t_tokens = h_ref.shape[0]

    @pl.when(pl.program_id(0) == 0)
    def _():
        r_i = lax.broadcasted_iota(jnp.int32, (t_tokens, t_tokens), 0)
        c_i = lax.broadcasted_iota(jnp.int32, (t_tokens, t_tokens), 1)
        before_scr[...] = jnp.where(r_i < c_i, 1.0, 0.0).astype(BF16)
        eye_scr[...] = jnp.where(r_i == c_i, 1.0, 0.0).astype(BF16)

    xn = _rms_norm_f32(h_ref[...], g_ref[...])
    xn_hi = xn.astype(BF16)
    xn_ref[...] = xn_hi
    nt_dims = (((1,), (1,)), ((), ()))
    xn_lo = (xn - xn_hi.astype(F32)).astype(BF16)
    w = wrt_ref[...]
    ne = w.shape[0]
    w_hi = w.astype(BF16)
    w_lo = (w - w_hi.astype(F32)).astype(BF16)
    by_hi = lax.dot_general(jnp.concatenate([w_hi, w_lo], axis=0), xn_hi, nt_dims, preferred_element_type=F32)
    logits = by_hi[:ne] + by_hi[ne:] + lax.dot_general(w_hi, xn_lo, nt_dims, preferred_element_type=F32)
    sub = lax.broadcasted_iota(jnp.int32, logits.shape, 0)
    m1 = jnp.max(logits, axis=0, keepdims=True)
    i1 = jnp.min(jnp.where(logits == m1, sub, ne), axis=0, keepdims=True)
    rest = jnp.where(sub == i1, -jnp.inf, logits)
    m2 = jnp.max(rest, axis=0, keepdims=True)
    i2 = jnp.min(jnp.where(rest == m2, sub, ne), axis=0, keepdims=True)
    e = jnp.exp(m2 - m1)
    wrow_ref[...] = jnp.concatenate([1.0 / (1.0 + e), e / (1.0 + e)], axis=0)

    sel1, sel2 = sub == i1, sub == i2
    memb = jnp.where(sel1, 1.0, jnp.where(sel2, 1.0, 0.0))
    rank = jnp.dot(memb.astype(BF16), before_scr[...], preferred_element_type=F32)
    sub_c = lax.broadcasted_iota(jnp.int32, cnt_ref.shape, 0)
    cnt_out = jnp.zeros(cnt_ref.shape, F32)
    base = rank
    running = jnp.zeros((1, 1), F32)
    for ex in range(ne):
        c = jnp.sum(memb[ex:ex + 1, :], axis=1, keepdims=True)
        cnt_out = jnp.where(sub_c == ex, c, cnt_out)
        base = jnp.where(sub == ex, base + running, base)
        running = running + jnp.floor((c + (MOE_SEG_ALIGN - 1)) * (1.0 / MOE_SEG_ALIGN)) * MOE_SEG_ALIGN
    cnt_ref[...] = cnt_out.astype(jnp.int32)
    pos1 = jnp.sum(jnp.where(sel1, base, 0.0), axis=0, keepdims=True)
    pos2 = jnp.sum(jnp.where(sel2, base, 0.0), axis=0, keepdims=True)
    posr_ref[...] = jnp.concatenate([pos1, pos2], axis=0).astype(jnp.int32)
    digits = []
    for pos in (pos1, pos2):
        hi = jnp.floor(pos * (1.0 / 128.0))
        digits += [hi, pos - 128.0 * hi]
    pad = jnp.zeros((2 * SUBLANES - len(digits), t_tokens), F32)
    dig_t = lax.dot_general(eye_scr[...], jnp.concatenate(digits + [pad], axis=0).astype(BF16), nt_dims,
                            preferred_element_type=F32)
    d_i = lax.broadcasted_iota(jnp.int32, (2 * SUBLANES, posc_ref.shape[1]), 0)
    c_i = lax.broadcasted_iota(jnp.int32, (2 * SUBLANES, posc_ref.shape[1]), 1)
    recombine = jnp.where(d_i == 2 * c_i, 128.0, jnp.where(d_i == 2 * c_i + 1, 1.0, 0.0)).astype(BF16)
    posc = jnp.dot(dig_t.astype(BF16), recombine, preferred_element_type=F32)
    posc_ref[...] = posc.astype(jnp.int32)


def _router(h, g_all, layer, wr_all, widx):
    n, d = h.shape
    depth = g_all.shape[0]
    ne = wr_all.shape[-1]
    tm = MOE_TOKEN_TILE
    nt = n // tm
    wrt = jnp.swapaxes(wr_all, 1, 2)
    return pl.pallas_call(
        _router_body,
        out_shape=[jax.ShapeDtypeStruct((n, d), BF16),
                   jax.ShapeDtypeStruct((MOE_TOP_K, n), jnp.int32),
                   jax.ShapeDtypeStruct((MOE_TOP_K, n), F32),
                   jax.ShapeDtypeStruct((n, SUBLANES), jnp.int32),
                   jax.ShapeDtypeStruct((nt, ne, 128), jnp.int32)],
        grid=(nt,),
        in_specs=[pl.BlockSpec((tm, d), lambda t: (t, 0)),
                  pl.BlockSpec((None, 1, d), lambda t: (layer, 0, 0)),
                  pl.BlockSpec((None, ne, d), lambda t: (widx, 0, 0))],
        out_specs=[pl.BlockSpec((tm, d), lambda t: (t, 0)),
                   pl.BlockSpec((MOE_TOP_K, tm), lambda t: (0, t)),
                   pl.BlockSpec((MOE_TOP_K, tm), lambda t: (0, t)),
                   pl.BlockSpec((tm, SUBLANES), lambda t: (t, 0)),
                   pl.BlockSpec((None, ne, 128), lambda t: (t, 0, 0))],
        scratch_shapes=[pltpu.VMEM((tm, tm), BF16), pltpu.VMEM((tm, tm), BF16)],
        compiler_params=_cparams("arbitrary"),
        name="router",
    )(h, g_all.reshape(depth, 1, d), wrt)


def _moe_tables(cnt, n_sorted_rows):
    cnt = cnt[:, :, 0]
    nt, ne = cnt.shape
    seg = (cnt + (MOE_SEG_ALIGN - 1)) // MOE_SEG_ALIGN * MOE_SEG_ALIGN
    seg_off = jnp.cumsum(seg, axis=1) - seg
    e_rows = jnp.sum(seg, axis=0)
    e_tiles = (e_rows + (MOE_ROW_TILE - 1)) // MOE_ROW_TILE
    e_cum = jnp.cumsum(e_tiles)
    e_base = (e_cum - e_tiles) * MOE_ROW_TILE
    dst = e_base[None, :] + jnp.cumsum(seg, axis=0) - seg
    n_used = e_cum[-1]
    tile_ids = jnp.minimum(jnp.arange(n_sorted_rows // MOE_ROW_TILE, dtype=jnp.int32), n_used - 1)
    tile_expert = jnp.sum(tile_ids[:, None] >= e_cum[None, :], axis=1)
    last_tile_row = e_base + (e_tiles - 1) * MOE_ROW_TILE
    i32 = lambda a: a.astype(jnp.int32)
    return dict(seg_off=i32(seg_off.reshape(-1)), seg_len=i32(seg.reshape(-1)), dst=i32(dst.reshape(-1)),
                n_used=i32(n_used.reshape(1)), tile_expert=i32(tile_expert),
                last_tile_row=i32(last_tile_row), has_rows=i32(e_tiles > 0))


def _moe_chunk_copies(t, ne, seg_off_ref, seg_len_ref, dst_ref, make_copy, act):
    for ex in range(ne):
        idx = t * ne + ex
        off = seg_off_ref[idx]
        ln = seg_len_ref[idx]
        row = dst_ref[idx]
        for size in MOE_CHUNK_SIZES:
            take = (ln & size) != 0

            @pl.when(take)
            def _(off=off, row=row, size=size):
                for cp in make_copy(pl.multiple_of(off, MOE_SEG_ALIGN), pl.multiple_of(row, MOE_SEG_ALIGN), size):
                    act(cp)

            step = jnp.where(take, size, 0)
            off = off + step
            row = row + step


def _dispatch_body(seg_off_ref, seg_len_ref, dst_ref, last_row_ref, has_rows_ref, n_used_ref,
                   xn_ref, posr_ref, wrow_ref, xs_ref, ws_ref, cbuf, wbuf, zx, zw, sems, zsem):
    t = pl.program_id(0)
    n_tiles = pl.num_programs(0)
    ne = last_row_ref.shape[0]
    n_rows, t_tokens = cbuf.shape[1], xn_ref.shape[0]

    def zero_copies(row):
        row = pl.multiple_of(row, MOE_ROW_TILE)
        return (pltpu.make_async_copy(zx, xs_ref.at[pl.ds(row, MOE_ROW_TILE)], zsem),
                pltpu.make_async_copy(zw, ws_ref.at[pl.ds(row, MOE_ROW_TILE)], zsem))

    @pl.when(t == 0)
    def _():
        zx[...] = jnp.zeros_like(zx)
        zw[...] = jnp.zeros_like(zw)
        for act in (lambda cp: cp.start(), lambda cp: cp.wait()):
            for ex in range(ne):
                @pl.when(has_rows_ref[ex] != 0)
                def _(ex=ex):
                    for cp in zero_copies(last_row_ref[ex]):
                        act(cp)

        def tail_body(i, carry):
            for cp in zero_copies(i * MOE_ROW_TILE):
                cp.start()
                cp.wait()
            return carry

        lax.fori_loop(n_used_ref[0], xs_ref.shape[0] // MOE_ROW_TILE, tail_body, 0)

    r_iota = lax.broadcasted_iota(jnp.int32, (n_rows, t_tokens), 0)
    hit1 = r_iota == posr_ref[0:1, :]
    hit2 = r_iota == posr_ref[1:2, :]
    onehot = jnp.where(hit1, 1.0, jnp.where(hit2, 1.0, 0.0)).astype(BF16)
    slot = t % 2
    cbuf[slot] = jnp.dot(onehot, xn_ref[...], preferred_element_type=F32).astype(BF16)
    wsel = jnp.where(hit1, wrow_ref[0:1, :], jnp.where(hit2, wrow_ref[1:2, :], 0.0))
    wbuf[slot] = jnp.broadcast_to(jnp.sum(wsel, axis=1, keepdims=True), wbuf.shape[1:])

    def drain(tile, act):
        s = tile % 2

        def make_copy(off, row, size):
            return (pltpu.make_async_copy(cbuf.at[s, pl.ds(off, size)], xs_ref.at[pl.ds(row, size)], sems.at[s]),
                    pltpu.make_async_copy(wbuf.at[s, pl.ds(off, size)], ws_ref.at[pl.ds(row, size)], sems.at[s]))

        _moe_chunk_copies(tile, ne, seg_off_ref, seg_len_ref, dst_ref, make_copy, act)

    drain(t, lambda cp: cp.start())

    @pl.when(t > 0)
    def _():
        drain(t - 1, lambda cp: cp.wait())

    @pl.when(t == n_tiles - 1)
    def _():
        drain(t, lambda cp: cp.wait())


def _dispatch(xn, posr, wrow, tables, n_sorted_rows, ne):
    n, d = xn.shape
    tm = MOE_TOKEN_TILE
    n_rows = _moe_compact_rows(ne)
    grid_spec = pltpu.PrefetchScalarGridSpec(
        num_scalar_prefetch=6,
        grid=(n // tm,),
        in_specs=[pl.BlockSpec((tm, d), lambda t, *_: (t, 0)),
                  pl.BlockSpec((MOE_TOP_K, tm), lambda t, *_: (0, t)),
                  pl.BlockSpec((MOE_TOP_K, tm), lambda t, *_: (0, t))],
        out_specs=[pl.BlockSpec(memory_space=pl.ANY), pl.BlockSpec(memory_space=pl.ANY)],
        scratch_shapes=[pltpu.VMEM((2, n_rows, d), BF16), pltpu.VMEM((2, n_rows, 128), F32),
                        pltpu.VMEM((MOE_ROW_TILE, d), BF16), pltpu.VMEM((MOE_ROW_TILE, 128), F32),
                        pltpu.SemaphoreType.DMA((2,)), pltpu.SemaphoreType.DMA],
    )
    return pl.pallas_call(
        _dispatch_body,
        out_shape=[jax.ShapeDtypeStruct((n_sorted_rows, d), BF16), jax.ShapeDtypeStruct((n_sorted_rows, 128), F32)],
        grid_spec=grid_spec,
        compiler_params=_cparams("arbitrary"),
        name="moe_dispatch",
    )(tables['seg_off'], tables['seg_len'], tables['dst'], tables['last_tile_row'], tables['has_rows'], tables['n_used'],
      xn, posr, wrow)


def _experts_body(tile_expert_ref, n_used_ref, xs_ref, ws_ref, wgu_ref, wd_ref, ys_ref, wgu16, wd16):
    i = pl.program_id(0)
    used = i < n_used_ref[0]
    prev = tile_expert_ref[jnp.maximum(i - 1, 0)]
    new_expert = jnp.logical_or(i == 0, tile_expert_ref[i] != prev)
    f = wd_ref.shape[0]
    fc = 512

    @pl.when(jnp.logical_and(used, new_expert))
    def _():
        for c in range(2 * f // fc):
            wgu16[:, c * fc:(c + 1) * fc] = wgu_ref[:, c * fc:(c + 1) * fc].astype(BF16)
        for c in range(f // fc):
            wd16[c * fc:(c + 1) * fc, :] = wd_ref[c * fc:(c + 1) * fc, :].astype(BF16)

    @pl.when(used)
    def _():
        x = xs_ref[...]
        w = ws_ref[...]
        wrep = jnp.concatenate([w] * (fc // w.shape[1]), axis=1)
        acc = None
        for c in range(f // fc):
            gate = jnp.dot(x, wgu16[:, c * fc:(c + 1) * fc], preferred_element_type=F32)
            up = jnp.dot(x, wgu16[:, f + c * fc:f + (c + 1) * fc], preferred_element_type=F32)
            act = (gate * _sigmoid(gate) * up * wrep).astype(BF16)
            part = jnp.dot(act, wd16[c * fc:(c + 1) * fc, :], preferred_element_type=F32)
            acc = part if acc is None else acc + part
        ys_ref[...] = acc.astype(ys_ref.dtype)

    @pl.when(jnp.logical_not(used))
    def _():
        ys_ref[...] = jnp.zeros_like(ys_ref)


def _experts(xs, ws, tables, w_gu_all, w_down_all, widx):
    rows, d = xs.shape
    f = w_down_all.shape[-2]
    tile = lambda w: pl.BlockSpec((MOE_ROW_TILE, w), lambda i, te, nu: (jnp.maximum(jnp.minimum(i, nu[0] - 1), 0), 0))
    grid_spec = pltpu.PrefetchScalarGridSpec(
        num_scalar_prefetch=2,
        grid=(rows // MOE_ROW_TILE,),
        in_specs=[tile(d), tile(ws.shape[1]),
                  pl.BlockSpec((None, None, d, 2 * f), lambda i, te, nu: (widx, te[i], 0, 0)),
                  pl.BlockSpec((None, None, f, d), lambda i, te, nu: (widx, te[i], 0, 0), pipeline_mode=pl.Buffered(1))],
        out_specs=pl.BlockSpec((MOE_ROW_TILE, d), lambda i, te, nu: (i, 0)),
        scratch_shapes=[pltpu.VMEM((d, 2 * f), BF16), pltpu.VMEM((f, d), BF16)],
    )
    return pl.pallas_call(
        _experts_body,
        out_shape=jax.ShapeDtypeStruct((rows, d), BF16),
        grid_spec=grid_spec,
        compiler_params=_cparams("arbitrary"),
        name="moe_experts",
    )(tables['tile_expert'], tables['n_used'], xs, ws, w_gu_all, w_down_all)


def _combine_body(seg_off_ref, seg_len_ref, dst_ref, ys_ref, posc_ref, h_ref, gain_ref, *rest, ne, final):
    *out_refs, ybuf, sems = rest
    t = pl.program_id(0)
    n_tiles = pl.num_programs(0)
    t_tokens, n_rows = h_ref.shape[0], ybuf.shape[1]

    def fetch(tile, act):
        slot = tile % 2

        def make_copy(off, row, size):
            return (pltpu.make_async_copy(ys_ref.at[pl.ds(row, size)], ybuf.at[slot, pl.ds(off, size)], sems.at[slot]),)

        _moe_chunk_copies(tile, ne, seg_off_ref, seg_len_ref, dst_ref, make_copy, act)

    @pl.when(t == 0)
    def _():
        ybuf[...] = jnp.zeros_like(ybuf)
        fetch(t, lambda cp: cp.start())

    @pl.when(t + 1 < n_tiles)
    def _():
        fetch(t + 1, lambda cp: cp.start())

    fetch(t, lambda cp: cp.wait())

    pos = posc_ref[...]
    lane_r = lax.broadcasted_iota(jnp.int32, (t_tokens, n_rows), 1)
    onehot = jnp.where(lane_r == pos[:, 0:1], 1.0, jnp.where(lane_r == pos[:, 1:2], 1.0, 0.0)).astype(BF16)
    out = h_ref[...] + jnp.dot(onehot, ybuf[t % 2], preferred_element_type=F32)
    normed = _rms_norm_f32(out, gain_ref[...])
    if final:
        out_refs[0][...] = normed
    else:
        out_refs[0][...] = out
        out_refs[1][...] = normed.astype(out_refs[1].dtype)


def _combine(ys, posc, h, tables, ne, gain, final):
    n, d = h.shape
    tm = MOE_TOKEN_TILE
    tile = pl.BlockSpec((tm, d), lambda t, *_: (t, 0))
    grid_spec = pltpu.PrefetchScalarGridSpec(
        num_scalar_prefetch=3,
        grid=(n // tm,),
        in_specs=[pl.BlockSpec(memory_space=pl.ANY),
                  pl.BlockSpec((tm, posc.shape[1]), lambda t, *_: (t, 0)),
                  tile,
                  pl.BlockSpec((1, d), lambda t, *_: (0, 0))],
        out_specs=[tile] if final else [tile, tile],
        scratch_shapes=[pltpu.VMEM((2, _moe_compact_rows(ne), d), BF16), pltpu.SemaphoreType.DMA((2,))],
    )
    res = jax.ShapeDtypeStruct((n, d), F32)
    return pl.pallas_call(
        functools.partial(_combine_body, ne=ne, final=final),
        out_shape=[res] if final else [res, jax.ShapeDtypeStruct((n, d), BF16)],
        grid_spec=grid_spec,
        compiler_params=_cparams("arbitrary"),
        name="moe_combine",
    )(tables['seg_off'], tables['seg_len'], tables['dst'], ys, posc, h, gain.reshape(1, d))


def _moe(h, g_all, layer, wr_all, w_gu_all, w_down_all, widx, gain, final):
    n = h.shape[0]
    ne = wr_all.shape[-1]
    xn, posr, wrow, posc, cnt = _router(h, g_all, layer, wr_all, widx)
    n_sorted_rows = _moe_sorted_rows(n, ne)
    tables = _moe_tables(cnt, n_sorted_rows)
    xs, ws = _dispatch(xn, posr, wrow, tables, n_sorted_rows, ne)
    ys = _experts(xs, ws, tables, w_gu_all, w_down_all, widx)
    return _combine(ys, posc, h, tables, ne, gain, final)


def kernel(x, norm_mix_g, w_in, b_in, na_rpb, conv_w, conv_b, lru_wa, lru_ba, lru_wx, lru_bx, lru_lambda, w_branch, w_out, norm_ffn_g, ffn_w_gu, ffn_w_down, router_w, moe_w_gu, moe_w_down, final_g):
    batch, seq, d = x.shape
    depth = w_in.shape[0]
    n = batch * seq
    bw = w_branch.shape[2]
    h = x.reshape(n, d)

    tables = _na_bias_tables(na_rpb)
    twiddles = _fourier_twiddles(seq)
    w_gate = _lru_gate_weights(lru_wa, lru_wx)
    wb16 = w_branch.astype(BF16)
    wo16 = w_out.astype(BF16)
    tn = 3 * bw

    xn = _rms_norm(h, norm_mix_g[0], BF16)
    for l in range(depth):
        last = l == depth - 1
        q, k, v = _proj(xn, w_in, b_in, l, 0, tn, 3)
        u_f, u_x, u_g = _proj(xn, w_in, b_in, l, 1, tn, 3)
        y_a = _neighbourhood_attention(q, k, v, tables, l, batch)
        y_b = _fourier_mix(u_f, batch, twiddles)
        h_c = _recurrent_branch(u_x, conv_w, conv_b, w_gate, lru_ba, lru_bx, lru_lambda, l, batch)
        h = _merge(y_a, y_b, h_c, u_g, xn, h, w_in, b_in, wb16, wo16, l)
        next_gain = final_g if last else norm_mix_g[l + 1]
        if l % 2 == 0:
            h, xn = _ffn(h, norm_ffn_g[l], next_gain, ffn_w_gu, ffn_w_down, l // 2)
            if last:
                h = _rms_norm(h, final_g, F32)
        elif last:
            (h,) = _moe(h, norm_ffn_g, l, router_w, moe_w_gu, moe_w_down, l // 2, next_gain, final=True)
        else:
            h, xn = _moe(h, norm_ffn_g, l, router_w, moe_w_gu, moe_w_down, l // 2, next_gain, final=False)
    return h.reshape(batch, seq, d)
```

```python
import functools

import numpy as np
import jax
import jax.numpy as jnp
from jax import lax
from jax.experimental import pallas as pl
from jax.experimental.pallas import tpu as pltpu

F32 = jnp.float32
BF16 = jnp.bfloat16

RMS_EPS = 1e-6
GRID_W = 64
NA_HEADS = 8
NA_HEAD_DIM = 64
NA_KH = 8
NA_KW = 16
NA_ROWS_PER_BLOCK = 8
NA_ROWS_IN_FLIGHT = 4
LRU_C = 8.0
CONV_W = 4
CONV_PAD_LEFT = 2
N_EXPERTS = 8
MASK_VALUE = -1e30

VMEM_LIMIT_BYTES = 56 * 1024 * 1024


def _cparams(*sem):
    return pltpu.CompilerParams(dimension_semantics=sem, vmem_limit_bytes=VMEM_LIMIT_BYTES)


def _rms_norm_f32(x, g):
    ms = jnp.mean(x * x, axis=-1, keepdims=True)
    return x * lax.rsqrt(ms + RMS_EPS) * g


def _sigmoid(x):
    return 1.0 / (1.0 + jnp.exp(-x))


def _rms_norm_body(h_ref, g_ref, o_ref):
    o_ref[...] = _rms_norm_f32(h_ref[...], g_ref[...]).astype(o_ref.dtype)


def _rms_norm(h, g, out_dtype, tm=1024):
    n, d = h.shape
    return pl.pallas_call(
        _rms_norm_body,
        out_shape=jax.ShapeDtypeStruct((n, d), out_dtype),
        grid=(n // tm,),
        in_specs=[pl.BlockSpec((tm, d), lambda t: (t, 0)), pl.BlockSpec((1, d), lambda t: (0, 0))],
        out_specs=pl.BlockSpec((tm, d), lambda t: (t, 0)),
        compiler_params=_cparams("arbitrary"),
        name="rms_norm",
    )(h, g.reshape(1, d))


def _proj_body(x_ref, w_ref, b_ref, *out_refs):
    r = jnp.dot(x_ref[...], w_ref[...].astype(BF16), preferred_element_type=F32) + b_ref[...]
    width = r.shape[1] // len(out_refs)
    for i, o in enumerate(out_refs):
        o[...] = r[:, i * width:(i + 1) * width].astype(o.dtype)


def _proj(xn, w_all, b_all, layer, col_block, tn, n_out, tm=1024):
    n, d = xn.shape
    depth = w_all.shape[0]
    b3 = b_all.reshape(depth, 1, -1)
    width = tn // n_out
    return pl.pallas_call(
        _proj_body,
        out_shape=[jax.ShapeDtypeStruct((n, width), BF16) for _ in range(n_out)],
        grid=(n // tm,),
        in_specs=[
            pl.BlockSpec((tm, d), lambda t: (t, 0)),
            pl.BlockSpec((None, d, tn), lambda t: (layer, 0, col_block)),
            pl.BlockSpec((None, 1, tn), lambda t: (layer, 0, col_block)),
        ],
        out_specs=[pl.BlockSpec((tm, width), lambda t: (t, 0)) for _ in range(n_out)],
        compiler_params=_cparams("arbitrary"),
        name="in_proj",
    )(xn, w_all, b3)


def _na_bias_tables(rpb_all):
    cols = np.arange(GRID_W)
    col_start = np.clip(cols - NA_KW // 2, 0, GRID_W - NA_KW)
    cc = np.arange(GRID_W)[None, :]
    in_win = (cc >= col_start[:, None]) & (cc < col_start[:, None] + NA_KW)
    col_off = cc - cols[:, None] + (NA_KW - 1)
    onehot = np.zeros((2 * NA_KW - 1, GRID_W, GRID_W), np.float32)
    cq, ck = np.nonzero(in_win)
    onehot[col_off[cq, ck], cq, ck] = 1.0
    t = jnp.einsum('lhro,ocd->lhrcd', rpb_all.astype(F32), jnp.asarray(onehot),
                   precision=lax.Precision.HIGHEST)
    t = jnp.where(jnp.asarray(in_win)[None, None, None], t, MASK_VALUE)
    depth, heads, n_off = t.shape[:3]
    t = t.reshape(depth, heads // 2, 2, n_off, GRID_W, GRID_W)
    t = jnp.transpose(t, (0, 1, 3, 2, 4, 5)).reshape(depth, heads // 2, n_off, 2 * GRID_W, GRID_W)
    return jnp.concatenate([t[:, :, :-1], t[:, :, 1:]], axis=-1)


def _na_body(q_ref, kbuf, vbuf, tbl_ref, o_ref, s_scr, e_scr, *, n_blocks):
    j = pl.program_id(1)
    band = NA_KH * GRID_W
    half = NA_KH // 2
    lane = lax.broadcasted_iota(jnp.int32, (GRID_W, 2 * NA_HEAD_DIM), 1)
    lo = lane < NA_HEAD_DIM
    qscale = NA_HEAD_DIM ** -0.5
    head_mask = (jnp.where(lo, qscale, 0.0).astype(BF16), jnp.where(lo, 0.0, qscale).astype(BF16))

    n_pairs = NA_HEADS // 2
    cols = [slice(p * 2 * NA_HEAD_DIM, (p + 1) * 2 * NA_HEAD_DIM) for p in range(n_pairs)]

    def row_offsets(rl):
        start_first = jnp.maximum(rl - half, 0)
        start_last = jnp.minimum(rl + half, NA_ROWS_PER_BLOCK)
        start = jnp.where(j == 0, start_first, jnp.where(j == n_blocks - 1, start_last, rl))
        delta = jnp.where(j == 0, jnp.minimum(rl, half), jnp.where(j == n_blocks - 1, jnp.maximum(rl, half), half))
        return pl.multiple_of(rl * GRID_W, GRID_W), pl.multiple_of(start * GRID_W, GRID_W), delta

    def scores(rl, slot):
        qrow, krow, delta = row_offsets(rl)
        for p in range(n_pairs):
            q2 = q_ref[pl.ds(qrow, GRID_W), cols[p]]
            qs = jnp.concatenate([q2 * head_mask[0], q2 * head_mask[1]], axis=0)
            k2 = kbuf[pl.ds(krow, band), cols[p]]
            s = lax.dot_general(qs, k2, (((1,), (1,)), ((), ())), preferred_element_type=F32)
            bias = jnp.concatenate([tbl_ref[p, 2 * m - delta + (NA_KH - 1)] for m in range(NA_KH // 2)], axis=1)
            s_scr[slot, p] = s + bias

    def softmax(slot):
        inv_l = []
        for p in range(n_pairs):
            s = s_scr[slot, p]
            m = jnp.max(s, axis=-1, keepdims=True)
            e = jnp.exp(s - m)
            inv_l.append(1.0 / jnp.sum(e, axis=-1, keepdims=True))
            e_scr[slot, p] = e.astype(BF16)
        return inv_l

    def weighted_values(rl, slot, inv_l):
        qrow, krow, _ = row_offsets(rl)
        for p in range(n_pairs):
            v2 = vbuf[pl.ds(krow, band), cols[p]]
            o = jnp.dot(e_scr[slot, p], v2, preferred_element_type=F32) * inv_l[p]
            o_ref[pl.ds(qrow, GRID_W), cols[p]] = jnp.where(lo, o[:GRID_W], o[GRID_W:]).astype(o_ref.dtype)

    def rows_body(it, carry):
        rows = [it * NA_ROWS_IN_FLIGHT + r for r in range(NA_ROWS_IN_FLIGHT)]
        for slot, rl in enumerate(rows):
            scores(rl, slot)
        inv = [softmax(slot) for slot in range(NA_ROWS_IN_FLIGHT)]
        for slot, rl in enumerate(rows):
            weighted_values(rl, slot, inv[slot])
        return carry

    lax.fori_loop(0, NA_ROWS_PER_BLOCK // NA_ROWS_IN_FLIGHT, rows_body, 0)


def _neighbourhood_attention(q, k, v, tables, layer, batch):
    n, width = q.shape
    n_pairs = NA_HEADS // 2
    blk = NA_ROWS_PER_BLOCK * GRID_W
    n_blocks = n // batch // blk
    assert n_blocks >= 2 and NA_KH == NA_ROWS_PER_BLOCK
    s = n // batch
    half_blk = blk // 2

    def window_start(b, j):
        start = b * s + jnp.clip(j * blk - half_blk, 0, s - 2 * blk)
        return pl.multiple_of(start, half_blk), 0

    tile = pl.BlockSpec((blk, width), lambda b, j: (b * n_blocks + j, 0))
    window = pl.BlockSpec((pl.Element(2 * blk), pl.Element(width)), window_start)
    return pl.pallas_call(
        functools.partial(_na_body, n_blocks=n_blocks),
        out_shape=jax.ShapeDtypeStruct((n, width), BF16),
        grid=(batch, n_blocks),
        in_specs=[tile, window, window,
                  pl.BlockSpec((None,) + tables.shape[1:], lambda b, j: (layer, 0, 0, 0, 0))],
        out_specs=tile,
        scratch_shapes=[pltpu.VMEM((NA_ROWS_IN_FLIGHT, n_pairs, 2 * GRID_W, NA_KH * GRID_W), F32),
                        pltpu.VMEM((NA_ROWS_IN_FLIGHT, n_pairs, 2 * GRID_W, NA_KH * GRID_W), BF16)],
        compiler_params=_cparams("arbitrary", "arbitrary"),
        name="neigh_attn",
    )(q, k, v, tables)


FN_N2 = 128
FN_GROUP_DIM = 64
FN_GROUP = 16


def _dft_cos_sin(n):
    ang = 2.0 * np.pi * (np.outer(np.arange(n), np.arange(n)) % n) / n
    return np.cos(ang), np.sin(ang)


def _fourier_stage1_body(x_ref, f_ref, tc_ref, ts_ref, zr_ref, zi_ref):
    n1, n_b, width = x_ref.shape
    xs = jnp.swapaxes(x_ref[...], 0, 1)
    rep = width // tc_ref.shape[2]
    zr_all, zi_all = [], []
    for b in range(n_b):
        z = jnp.dot(f_ref[...], xs[b], preferred_element_type=F32)
        zr, zi = z[:n1], z[n1:]
        tc = jnp.concatenate([tc_ref[b]] * rep, axis=1)
        ts = jnp.concatenate([ts_ref[b]] * rep, axis=1)
        zr_all.append((zr * tc + zi * ts).astype(zr_ref.dtype))
        zi_all.append((zi * tc - zr * ts).astype(zi_ref.dtype))
    zr_ref[...] = jnp.swapaxes(jnp.stack(zr_all, axis=0), 0, 1)
    zi_ref[...] = jnp.swapaxes(jnp.stack(zi_all, axis=0), 0, 1)


def _fourier_stage2_body(zr_ref, zi_ref, f_ref, c_ref, o_ref, *, scale):
    n2 = zr_ref.shape[1]
    outs = []
    for i in range(zr_ref.shape[0]):
        z = jnp.concatenate([zr_ref[i], zi_ref[i]], axis=0)
        y = jnp.dot(f_ref[...], z, preferred_element_type=F32).astype(BF16)
        lanes = c_ref.shape[1]
        cols = [slice(cb * lanes, (cb + 1) * lanes) for cb in range(y.shape[1] // lanes)]
        out = jnp.concatenate(
            [jnp.dot(jnp.concatenate([y[:n2, cs], y[n2:, cs]], axis=1), c_ref[...], preferred_element_type=F32)
             for cs in cols], axis=1) * scale
        outs.append(out.astype(o_ref.dtype))
    o_ref[...] = jnp.swapaxes(jnp.stack(outs, axis=0), 0, 1)


def _fourier_mix(u, batch, twiddles):
    n, width = u.shape
    s = n // batch
    n2 = FN_N2
    n1 = s // n2
    tc, ts = twiddles
    c1, s1 = _dft_cos_sin(n1)
    f1 = jnp.asarray(np.concatenate([c1, -s1], axis=0), F32).astype(BF16)
    c2, s2 = _dft_cos_sin(n2)
    f2 = jnp.asarray(np.block([[c2, s2], [-s2, c2]]), F32).astype(BF16)
    cg, sg = _dft_cos_sin(FN_GROUP_DIM)
    eye = np.eye(128 // FN_GROUP_DIM)
    fc = jnp.asarray(np.concatenate([np.kron(eye, cg), np.kron(eye, sg)], axis=0), F32).astype(BF16)

    grp = FN_GROUP
    pos_blk = pl.BlockSpec((None, n1, grp, width), lambda b, j: (b, 0, j, 0))
    tw_blk = pl.BlockSpec((grp,) + tc.shape[1:], lambda b, j: (j, 0, 0))
    zr, zi = pl.pallas_call(
        _fourier_stage1_body,
        out_shape=[jax.ShapeDtypeStruct((batch, n1, n2, width), BF16)] * 2,
        grid=(batch, n2 // grp),
        in_specs=[pos_blk, pl.BlockSpec((2 * n1, n1), lambda b, j: (0, 0)), tw_blk, tw_blk],
        out_specs=[pos_blk, pos_blk],
        compiler_params=_cparams("arbitrary", "arbitrary"),
        name="fourier_stage1",
    )(u.reshape(batch, n1, n2, width), f1, tc, ts)

    freq_blk = pl.BlockSpec((None, grp, n2, width), lambda b, j: (b, j, 0, 0))
    out = pl.pallas_call(
        functools.partial(_fourier_stage2_body, scale=float(1.0 / np.sqrt(s * FN_GROUP_DIM))),
        out_shape=jax.ShapeDtypeStruct((batch, n2, n1, width), BF16),
        grid=(batch, n1 // grp),
        in_specs=[freq_blk, freq_blk,
                  pl.BlockSpec((2 * n2, 2 * n2), lambda b, j: (0, 0)),
                  pl.BlockSpec(fc.shape, lambda b, j: (0, 0))],
        out_specs=pl.BlockSpec((None, n2, grp, width), lambda b, j: (b, 0, j, 0)),
        compiler_params=_cparams("arbitrary", "arbitrary"),
        name="fourier_stage2",
    )(zr, zi, f2, fc)
    return out.reshape(n, width)


def _fourier_twiddles(s):
    n2 = FN_N2
    n1 = s // n2
    ang = (2.0 * np.pi / s) * (jnp.arange(n2, dtype=F32)[:, None] * jnp.arange(n1, dtype=F32)[None, :])
    rep = lambda t: jnp.broadcast_to(t[:, :, None], (n2, n1, 128))
    return rep(jnp.cos(ang)), rep(jnp.sin(ang))


LRU_LANES = 128
LRU_SEGMENTS = 16
LRU_JCHUNK = 32
SUBLANES = 8


def _lru_gate_weights(wa_all, wx_all):
    depth, _, nb, db, _ = wa_all.shape
    ncol = nb // 2

    def blockdiag(w):
        w = w.reshape(depth, ncol, 2, db, db)
        z = jnp.zeros_like(w[:, :, 0])
        top = jnp.concatenate([w[:, :, 0], z], axis=-1)
        bot = jnp.concatenate([z, w[:, :, 1]], axis=-1)
        return jnp.concatenate([top, bot], axis=-2)

    parts = [blockdiag(wa_all[:, 0]), blockdiag(wx_all[:, 0]), blockdiag(wa_all[:, 1]), blockdiag(wx_all[:, 1])]
    return jnp.concatenate(parts, axis=-1).astype(BF16)


def _gelu_tanh(x):
    return 0.5 * x * (1.0 + jnp.tanh(np.sqrt(2.0 / np.pi) * (x + 0.044715 * (x * x * x))))


def _lru_body(useq_ref, cw_ref, cb_ref, w_ref, ba_ref, bx_ref, lam_ref, oseq_ref, ux_ref, o_ref, h_scr, p_scr):
    n_j, n_g, lanes = ux_ref.shape
    ux_ref[...] = jnp.swapaxes(useq_ref[...].reshape(n_g, n_j, lanes), 0, 1)
    jc = LRU_JCHUNK
    n_chunks = n_j // jc
    seg = lax.broadcasted_iota(jnp.int32, (n_g, lanes), 0)

    def from_prev_segment(x):
        return jnp.where(seg >= 1, pltpu.roll(x, 1, axis=0), 0.0)

    def from_next_segment(x):
        return jnp.where(seg < n_g - 1, pltpu.roll(x, n_g - 1, axis=0), 0.0)

    def conv_chunk(j0):
        main = ux_ref[pl.ds(j0, jc)].astype(F32)
        lo_in = ux_ref[pl.ds(jnp.maximum(j0 - CONV_PAD_LEFT, 0), CONV_PAD_LEFT)].astype(F32)
        tail = ux_ref[n_j - CONV_PAD_LEFT:n_j].astype(F32)
        lo_wrap = jnp.stack([from_prev_segment(tail[r]) for r in range(CONV_PAD_LEFT)], axis=0)
        lo = jnp.where(j0 > 0, lo_in, lo_wrap)
        n_hi = CONV_W - 1 - CONV_PAD_LEFT
        hi_in = ux_ref[pl.ds(jnp.minimum(j0 + jc, n_j - n_hi), n_hi)].astype(F32)
        head = ux_ref[0:n_hi].astype(F32)
        hi_wrap = jnp.stack([from_next_segment(head[r]) for r in range(n_hi)], axis=0)
        hi = jnp.where(j0 + jc < n_j, hi_in, hi_wrap)
        ext = jnp.concatenate([lo, main, hi], axis=0)
        c = ext[0:jc] * cw_ref[0:1, :] + cb_ref[...]
        for tap in range(1, CONV_W):
            c = c + ext[tap:tap + jc] * cw_ref[tap:tap + 1, :]
        return c

    def gates(c, pre, d, j0):
        r = _sigmoid(pre[:, 2 * d * lanes:(2 * d + 1) * lanes] + ba_ref[d:d + 1, :])
        i = _sigmoid(pre[:, (2 * d + 1) * lanes:(2 * d + 2) * lanes] + bx_ref[d:d + 1, :])
        lam = lam_ref[d:d + 1, :]
        softplus = jnp.maximum(-lam, 0.0) + jnp.log(1.0 + jnp.exp(-jnp.abs(lam)))
        a = jnp.exp(-LRU_C * r * softplus)
        om = 1.0 - a * a
        mult = jnp.where(om > 0.0, om * lax.rsqrt(om), 0.0)
        a = a.reshape(jc, n_g, lanes)
        b = (mult * i).reshape(jc, n_g, lanes) * c
        edge = (jc - 1) * d
        i_edge = i.reshape(jc, n_g, lanes)[edge]
        is_first = (seg == (n_g - 1) * d) & (j0 + edge == (n_j - 1) * d)
        b_edge = jnp.where(is_first, i_edge * c[edge], b[edge])[None]
        b = jnp.concatenate([b_edge, b[1:]] if d == 0 else [b[:-1], b_edge], axis=0)
        return a, b

    def local_scan(a, b, carry, d):
        h, p = carry
        hs, ps = [None] * jc, [None] * jc
        for jj in (range(jc) if d == 0 else range(jc - 1, -1, -1)):
            h = a[jj] * h + b[jj]
            p = a[jj] * p
            hs[jj], ps[jj] = h, p
        return jnp.stack(hs, axis=0), jnp.stack(ps, axis=0), (h, p)

    scan_init = (jnp.zeros((n_g, lanes), F32), jnp.ones((n_g, lanes), F32))

    def gate_and_forward_body(ci, carry):
        j0 = ci * jc
        rows = pl.ds(j0, jc)
        c = conv_chunk(j0)
        pre = jnp.dot(c.reshape(jc * n_g, lanes).astype(BF16), w_ref[...], preferred_element_type=F32)
        a_b, b_b = gates(c, pre, 1, j0)
        p_scr[1, rows] = a_b
        h_scr[1, rows] = b_b
        a_f, b_f = gates(c, pre, 0, j0)
        h_scr[0, rows], p_scr[0, rows], carry = local_scan(a_f, b_f, carry, 0)
        return carry

    lax.fori_loop(0, n_chunks, gate_and_forward_body, scan_init)

    def backward_body(ci, carry):
        rows = pl.ds((n_chunks - 1 - ci) * jc, jc)
        h_scr[1, rows], p_scr[1, rows], carry = local_scan(p_scr[1, rows], h_scr[1, rows], carry, 1)
        return carry

    lax.fori_loop(0, n_chunks, backward_body, scan_init)

    def carry_in(d):
        edge = n_j - 1 if d == 0 else 0
        h_end, p_end = h_scr[d, edge], p_scr[d, edge]
        state = jnp.zeros((1, lanes), F32)
        out = jnp.zeros((n_g, lanes), F32)
        for g in (range(n_g) if d == 0 else range(n_g - 1, -1, -1)):
            out = jnp.where(seg == g, state, out)
            state = h_end[g:g + 1] + p_end[g:g + 1] * state
        return out

    e_fwd, e_bwd = carry_in(0), carry_in(1)

    def out_body(ci, carry):
        j0 = ci * jc
        rows = pl.ds(j0, jc)
        h = h_scr[0, rows] + p_scr[0, rows] * e_fwd + h_scr[1, rows] + p_scr[1, rows] * e_bwd
        o_ref[rows] = h.astype(o_ref.dtype)
        return carry

    lax.fori_loop(0, n_chunks, out_body, 0)
    oseq_ref[...] = jnp.swapaxes(o_ref[...], 0, 1).reshape(n_g * n_j, lanes)


def _recurrent_branch(u_x, conv_w, conv_b, w_gate, ba, bx, lam, layer, batch):
    n, width = u_x.shape
    s = n // batch
    depth = conv_w.shape[0]
    ncol = width // LRU_LANES
    n_g = LRU_SEGMENTS
    n_j = s // n_g
    assert n_j % LRU_JCHUNK == 0
    cb3 = conv_b.reshape(depth, 1, width)
    seq_spec = pl.BlockSpec((None, s, LRU_LANES), lambda b, c: (b, 0, c))
    par = lambda rows: pl.BlockSpec((None, rows, LRU_LANES), lambda b, c: (layer, 0, c))
    seg_copy = pltpu.VMEM((n_j, n_g, LRU_LANES), BF16)
    state = pltpu.VMEM((2, n_j, n_g, LRU_LANES), F32)
    out = pl.pallas_call(
        _lru_body,
        out_shape=jax.ShapeDtypeStruct((batch, s, width), BF16),
        grid=(batch, ncol),
        in_specs=[seq_spec, par(CONV_W), par(1),
                  pl.BlockSpec((None, None, LRU_LANES, 4 * LRU_LANES), lambda b, c: (layer, c, 0, 0)),
                  par(2), par(2), par(2)],
        out_specs=seq_spec,
        scratch_shapes=[seg_copy, seg_copy, state, state],
        compiler_params=_cparams("arbitrary", "arbitrary"),
        name="rg_lru",
    )(u_x.reshape(batch, s, width), conv_w, cb3, w_gate, ba, bx, lam)
    return out.reshape(n, width)


def _merge_body(ya_ref, yb_ref, hc_ref, ug_ref, xn_ref, h_ref, wgate_ref, bgate_ref, wb_ref, wo_ref, o_ref, wgate16):
    d = h_ref.shape[1]

    @pl.when(pl.program_id(0) == 0)
    def _():
        for kbr in range(wgate16.shape[1] // d):
            wgate16[:, kbr * d:(kbr + 1) * d] = wgate_ref[:, kbr * d:(kbr + 1) * d].astype(BF16)

    xn = xn_ref[...]
    yc = (hc_ref[...].astype(F32) * _gelu_tanh(ug_ref[...].astype(F32))).astype(BF16)
    merged = None
    for kbr, y in enumerate((ya_ref[...], yb_ref[...], yc)):
        cols = slice(kbr * d, (kbr + 1) * d)
        gate = jnp.dot(xn, wgate16[:, cols], preferred_element_type=F32) + bgate_ref[:, cols]
        ybr = jnp.dot(y, wb_ref[kbr], preferred_element_type=F32)
        term = _sigmoid(gate) * ybr
        merged = term if merged is None else merged + term
    o_ref[...] = h_ref[...] + jnp.dot(merged.astype(BF16), wo_ref[...], preferred_element_type=F32)


def _merge(ya, yb, hc, ug, xn, h, w_in_all, b_in_all, wb_all, wo_all, layer, tm=512):
    n, d = h.shape
    bw = ya.shape[1]
    depth = w_in_all.shape[0]
    gate_block = w_in_all.shape[2] // (3 * d) - 1
    assert (gate_block + 1) * 3 * d == w_in_all.shape[2]
    once = pl.Buffered(1)
    ytile = pl.BlockSpec((tm, bw), lambda t: (t, 0))
    return pl.pallas_call(
        _merge_body,
        out_shape=jax.ShapeDtypeStruct((n, d), F32),
        grid=(n // tm,),
        in_specs=[ytile, ytile, ytile, ytile,
                  pl.BlockSpec((tm, d), lambda t: (t, 0)),
                  pl.BlockSpec((tm, d), lambda t: (t, 0)),
                  pl.BlockSpec((None, d, 3 * d), lambda t: (layer, 0, gate_block), pipeline_mode=once),
                  pl.BlockSpec((None, 1, 3 * d), lambda t: (layer, 0, gate_block)),
                  pl.BlockSpec((None, 3, bw, d), lambda t: (layer, 0, 0, 0), pipeline_mode=once),
                  pl.BlockSpec((None, d, d), lambda t: (layer, 0, 0), pipeline_mode=once)],
        out_specs=pl.BlockSpec((tm, d), lambda t: (t, 0)),
        scratch_shapes=[pltpu.VMEM((d, 3 * d), BF16)],
        compiler_params=_cparams("arbitrary"),
        name="branch_merge",
    )(ya, yb, hc, ug, xn, h, w_in_all, b_in_all.reshape(depth, 1, -1), wb_all, wo_all)


def _ffn_body(h_ref, g_ref, gnext_ref, wg_ref, wu_ref, wd_ref, o_ref, xnext_ref, xn_ref):
    j = pl.program_id(1)

    @pl.when(j == 0)
    def _():
        h = h_ref[...]
        xn_ref[...] = _rms_norm_f32(h, g_ref[...]).astype(BF16)
        o_ref[...] = h

    xn = xn_ref[...]
    gate = jnp.dot(xn, wg_ref[...].astype(BF16), preferred_element_type=F32)
    up = jnp.dot(xn, wu_ref[...].astype(BF16), preferred_element_type=F32)
    act = gate * _sigmoid(gate) * up
    o_ref[...] += jnp.dot(act.astype(BF16), wd_ref[...].astype(BF16), preferred_element_type=F32)

    @pl.when(j == pl.num_programs(1) - 1)
    def _():
        xnext_ref[...] = _rms_norm_f32(o_ref[...], gnext_ref[...]).astype(xnext_ref.dtype)


def _ffn(h, g, g_next, w_gu_all, w_down_all, widx, tm=1024, tf=768):
    n, d = h.shape
    nf = w_down_all.shape[-2] // tf
    gain = pl.BlockSpec((1, d), lambda t, j: (0, 0))
    tile = pl.BlockSpec((tm, d), lambda t, j: (t, 0))
    return pl.pallas_call(
        _ffn_body,
        out_shape=[jax.ShapeDtypeStruct((n, d), F32), jax.ShapeDtypeStruct((n, d), BF16)],
        grid=(n // tm, nf),
        in_specs=[tile, gain, gain,
                  pl.BlockSpec((None, d, tf), lambda t, j: (widx, 0, j)),
                  pl.BlockSpec((None, d, tf), lambda t, j: (widx, 0, j + nf)),
                  pl.BlockSpec((None, tf, d), lambda t, j: (widx, j, 0))],
        out_specs=[tile, tile],
        scratch_shapes=[pltpu.VMEM((tm, d), BF16)],
        compiler_params=_cparams("arbitrary", "arbitrary"),
        name="dense_ffn",
    )(h, g.reshape(1, d), g_next.reshape(1, d), w_gu_all, w_gu_all, w_down_all)


MOE_TOKEN_TILE = 512
MOE_ROW_TILE = 512
MOE_SEG_ALIGN = 16
MOE_TOP_K = 2
MOE_CHUNK_SIZES = tuple(MOE_SEG_ALIGN << b for b in range(5, -1, -1))
assert MOE_CHUNK_SIZES[0] == MOE_TOKEN_TILE


def _moe_compact_rows(ne):
    rows = MOE_TOKEN_TILE * MOE_TOP_K + ne * (MOE_SEG_ALIGN - 1)
    return -(-rows // MOE_SEG_ALIGN) * MOE_SEG_ALIGN


def _moe_sorted_rows(n, ne):
    rows = n * MOE_TOP_K + (n // MOE_TOKEN_TILE) * ne * (MOE_SEG_ALIGN - 1) + ne * (MOE_ROW_TILE - MOE_SEG_ALIGN)
    return -(-rows // MOE_ROW_TILE) * MOE_ROW_TILE


def _router_body(h_ref, g_ref, wrt_ref, xn_ref, posr_ref, wrow_ref, posc_ref, cnt_ref, before_scr, eye_scr):
    t_tokens = h_ref.shape[0]

    @pl.when(pl.program_id(0) == 0)
    def _():
        r_i = lax.broadcasted_iota(jnp.int32, (t_tokens, t_tokens), 0)
        c_i = lax.broadcasted_iota(jnp.int32, (t_tokens, t_tokens), 1)
        before_scr[...] = jnp.where(r_i < c_i, 1.0, 0.0).astype(BF16)
        eye_scr[...] = jnp.where(r_i == c_i, 1.0, 0.0).astype(BF16)

    xn = _rms_norm_f32(h_ref[...], g_ref[...])
    xn_hi = xn.astype(BF16)
    xn_ref[...] = xn_hi
    nt_dims = (((1,), (1,)), ((), ()))
    xn_lo = (xn - xn_hi.astype(F32)).astype(BF16)
    w = wrt_ref[...]
    ne = w.shape[0]
    w_hi = w.astype(BF16)
    w_lo = (w - w_hi.astype(F32)).astype(BF16)
    by_hi = lax.dot_general(jnp.concatenate([w_hi, w_lo], axis=0), xn_hi, nt_dims, preferred_element_type=F32)
    logits = by_hi[:ne] + by_hi[ne:] + lax.dot_general(w_hi, xn_lo, nt_dims, preferred_element_type=F32)
    sub = lax.broadcasted_iota(jnp.int32, logits.shape, 0)
    m1 = jnp.max(logits, axis=0, keepdims=True)
    i1 = jnp.min(jnp.where(logits == m1, sub, ne), axis=0, keepdims=True)
    rest = jnp.where(sub == i1, -jnp.inf, logits)
    m2 = jnp.max(rest, axis=0, keepdims=True)
    i2 = jnp.min(jnp.where(rest == m2, sub, ne), axis=0, keepdims=True)
    e = jnp.exp(m2 - m1)
    wrow_ref[...] = jnp.concatenate([1.0 / (1.0 + e), e / (1.0 + e)], axis=0)

    sel1, sel2 = sub == i1, sub == i2
    memb = jnp.where(sel1, 1.0, jnp.where(sel2, 1.0, 0.0))
    rank = jnp.dot(memb.astype(BF16), before_scr[...], preferred_element_type=F32)
    sub_c = lax.broadcasted_iota(jnp.int32, cnt_ref.shape, 0)
    cnt_out = jnp.zeros(cnt_ref.shape, F32)
    base = rank
    running = jnp.zeros((1, 1), F32)
    for ex in range(ne):
        c = jnp.sum(memb[ex:ex + 1, :], axis=1, keepdims=True)
        cnt_out = jnp.where(sub_c == ex, c, cnt_out)
        base = jnp.where(sub == ex, base + running, base)
        running = running + jnp.floor((c + (MOE_SEG_ALIGN - 1)) * (1.0 / MOE_SEG_ALIGN)) * MOE_SEG_ALIGN
    cnt_ref[...] = cnt_out.astype(jnp.int32)
    pos1 = jnp.sum(jnp.where(sel1, base, 0.0), axis=0, keepdims=True)
    pos2 = jnp.sum(jnp.where(sel2, base, 0.0), axis=0, keepdims=True)
    posr_ref[...] = jnp.concatenate([pos1, pos2], axis=0).astype(jnp.int32)
    digits = []
    for pos in (pos1, pos2):
        hi = jnp.floor(pos * (1.0 / 128.0))
        digits += [hi, pos - 128.0 * hi]
    pad = jnp.zeros((2 * SUBLANES - len(digits), t_tokens), F32)
    dig_t = lax.dot_general(eye_scr[...], jnp.concatenate(digits + [pad], axis=0).astype(BF16), nt_dims,
                            preferred_element_type=F32)
    d_i = lax.broadcasted_iota(jnp.int32, (2 * SUBLANES, posc_ref.shape[1]), 0)
    c_i = lax.broadcasted_iota(jnp.int32, (2 * SUBLANES, posc_ref.shape[1]), 1)
    recombine = jnp.where(d_i == 2 * c_i, 128.0, jnp.where(d_i == 2 * c_i + 1, 1.0, 0.0)).astype(BF16)
    posc = jnp.dot(dig_t.astype(BF16), recombine, preferred_element_type=F32)
    posc_ref[...] = posc.astype(jnp.int32)


def _router(h, g_all, layer, wr_all, widx):
    n, d = h.shape
    depth = g_all.shape[0]
    ne = wr_all.shape[-1]
    tm = MOE_TOKEN_TILE
    nt = n // tm
    wrt = jnp.swapaxes(wr_all, 1, 2)
    return pl.pallas_call(
        _router_body,
        out_shape=[jax.ShapeDtypeStruct((n, d), BF16),
                   jax.ShapeDtypeStruct((MOE_TOP_K, n), jnp.int32),
                   jax.ShapeDtypeStruct((MOE_TOP_K, n), F32),
                   jax.ShapeDtypeStruct((n, SUBLANES), jnp.int32),
                   jax.ShapeDtypeStruct((nt, ne, 128), jnp.int32)],
        grid=(nt,),
        in_specs=[pl.BlockSpec((tm, d), lambda t: (t, 0)),
                  pl.BlockSpec((None, 1, d), lambda t: (layer, 0, 0)),
                  pl.BlockSpec((None, ne, d), lambda t: (widx, 0, 0))],
        out_specs=[pl.BlockSpec((tm, d), lambda t: (t, 0)),
                   pl.BlockSpec((MOE_TOP_K, tm), lambda t: (0, t)),
                   pl.BlockSpec((MOE_TOP_K, tm), lambda t: (0, t)),
                   pl.BlockSpec((tm, SUBLANES), lambda t: (t, 0)),
                   pl.BlockSpec((None, ne, 128), lambda t: (t, 0, 0))],
        scratch_shapes=[pltpu.VMEM((tm, tm), BF16), pltpu.VMEM((tm, tm), BF16)],
        compiler_params=_cparams("arbitrary"),
        name="router",
    )(h, g_all.reshape(depth, 1, d), wrt)


def _moe_tables(cnt, n_sorted_rows):
    cnt = cnt[:, :, 0]
    nt, ne = cnt.shape
    seg = (cnt + (MOE_SEG_ALIGN - 1)) // MOE_SEG_ALIGN * MOE_SEG_ALIGN
    seg_off = jnp.cumsum(seg, axis=1) - seg
    e_rows = jnp.sum(seg, axis=0)
    e_tiles = (e_rows + (MOE_ROW_TILE - 1)) // MOE_ROW_TILE
    e_cum = jnp.cumsum(e_tiles)
    e_base = (e_cum - e_tiles) * MOE_ROW_TILE
    dst = e_base[None, :] + jnp.cumsum(seg, axis=0) - seg
    n_used = e_cum[-1]
    tile_ids = jnp.minimum(jnp.arange(n_sorted_rows // MOE_ROW_TILE, dtype=jnp.int32), n_used - 1)
    tile_expert = jnp.sum(tile_ids[:, None] >= e_cum[None, :], axis=1)
    last_tile_row = e_base + (e_tiles - 1) * MOE_ROW_TILE
    i32 = lambda a: a.astype(jnp.int32)
    return dict(seg_off=i32(seg_off.reshape(-1)), seg_len=i32(seg.reshape(-1)), dst=i32(dst.reshape(-1)),
                n_used=i32(n_used.reshape(1)), tile_expert=i32(tile_expert),
                last_tile_row=i32(last_tile_row), has_rows=i32(e_tiles > 0))


def _moe_chunk_copies(t, ne, seg_off_ref, seg_len_ref, dst_ref, make_copy, act):
    for ex in range(ne):
        idx = t * ne + ex
        off = seg_off_ref[idx]
        ln = seg_len_ref[idx]
        row = dst_ref[idx]
        for size in MOE_CHUNK_SIZES:
            take = (ln & size) != 0

            @pl.when(take)
            def _(off=off, row=row, size=size):
                for cp in make_copy(pl.multiple_of(off, MOE_SEG_ALIGN), pl.multiple_of(row, MOE_SEG_ALIGN), size):
                    act(cp)

            step = jnp.where(take, size, 0)
            off = off + step
            row = row + step


def _dispatch_body(seg_off_ref, seg_len_ref, dst_ref, last_row_ref, has_rows_ref, n_used_ref,
                   xn_ref, posr_ref, wrow_ref, xs_ref, ws_ref, cbuf, wbuf, zx, zw, sems, zsem):
    t = pl.program_id(0)
    n_tiles = pl.num_programs(0)
    ne = last_row_ref.shape[0]
    n_rows, t_tokens = cbuf.shape[1], xn_ref.shape[0]

    def zero_copies(row):
        row = pl.multiple_of(row, MOE_ROW_TILE)
        return (pltpu.make_async_copy(zx, xs_ref.at[pl.ds(row, MOE_ROW_TILE)], zsem),
                pltpu.make_async_copy(zw, ws_ref.at[pl.ds(row, MOE_ROW_TILE)], zsem))

    @pl.when(t == 0)
    def _():
        zx[...] = jnp.zeros_like(zx)
        zw[...] = jnp.zeros_like(zw)
        for act in (lambda cp: cp.start(), lambda cp: cp.wait()):
            for ex in range(ne):
                @pl.when(has_rows_ref[ex] != 0)
                def _(ex=ex):
                    for cp in zero_copies(last_row_ref[ex]):
                        act(cp)

        def tail_body(i, carry):
            for cp in zero_copies(i * MOE_ROW_TILE):
                cp.start()
                cp.wait()
            return carry

        lax.fori_loop(n_used_ref[0], xs_ref.shape[0] // MOE_ROW_TILE, tail_body, 0)

    r_iota = lax.broadcasted_iota(jnp.int32, (n_rows, t_tokens), 0)
    hit1 = r_iota == posr_ref[0:1, :]
    hit2 = r_iota == posr_ref[1:2, :]
    onehot = jnp.where(hit1, 1.0, jnp.where(hit2, 1.0, 0.0)).astype(BF16)
    slot = t % 2
    cbuf[slot] = jnp.dot(onehot, xn_ref[...], preferred_element_type=F32).astype(BF16)
    wsel = jnp.where(hit1, wrow_ref[0:1, :], jnp.where(hit2, wrow_ref[1:2, :], 0.0))
    wbuf[slot] = jnp.broadcast_to(jnp.sum(wsel, axis=1, keepdims=True), wbuf.shape[1:])

    def drain(tile, act):
        s = tile % 2

        def make_copy(off, row, size):
            return (pltpu.make_async_copy(cbuf.at[s, pl.ds(off, size)], xs_ref.at[pl.ds(row, size)], sems.at[s]),
                    pltpu.make_async_copy(wbuf.at[s, pl.ds(off, size)], ws_ref.at[pl.ds(row, size)], sems.at[s]))

        _moe_chunk_copies(tile, ne, seg_off_ref, seg_len_ref, dst_ref, make_copy, act)

    drain(t, lambda cp: cp.start())

    @pl.when(t > 0)
    def _():
        drain(t - 1, lambda cp: cp.wait())

    @pl.when(t == n_tiles - 1)
    def _():
        drain(t, lambda cp: cp.wait())


def _dispatch(xn, posr, wrow, tables, n_sorted_rows, ne):
    n, d = xn.shape
    tm = MOE_TOKEN_TILE
    n_rows = _moe_compact_rows(ne)
    grid_spec = pltpu.PrefetchScalarGridSpec(
        num_scalar_prefetch=6,
        grid=(n // tm,),
        in_specs=[pl.BlockSpec((tm, d), lambda t, *_: (t, 0)),
                  pl.BlockSpec((MOE_TOP_K, tm), lambda t, *_: (0, t)),
                  pl.BlockSpec((MOE_TOP_K, tm), lambda t, *_: (0, t))],
        out_specs=[pl.BlockSpec(memory_space=pl.ANY), pl.BlockSpec(memory_space=pl.ANY)],
        scratch_shapes=[pltpu.VMEM((2, n_rows, d), BF16), pltpu.VMEM((2, n_rows, 128), F32),
                        pltpu.VMEM((MOE_ROW_TILE, d), BF16), pltpu.VMEM((MOE_ROW_TILE, 128), F32),
                        pltpu.SemaphoreType.DMA((2,)), pltpu.SemaphoreType.DMA],
    )
    return pl.pallas_call(
        _dispatch_body,
        out_shape=[jax.ShapeDtypeStruct((n_sorted_rows, d), BF16), jax.ShapeDtypeStruct((n_sorted_rows, 128), F32)],
        grid_spec=grid_spec,
        compiler_params=_cparams("arbitrary"),
        name="moe_dispatch",
    )(tables['seg_off'], tables['seg_len'], tables['dst'], tables['last_tile_row'], tables['has_rows'], tables['n_used'],
      xn, posr, wrow)


def _experts_body(tile_expert_ref, n_used_ref, xs_ref, ws_ref, wgu_ref, wd_ref, ys_ref, wgu16, wd16):
    i = pl.program_id(0)
    used = i < n_used_ref[0]
    prev = tile_expert_ref[jnp.maximum(i - 1, 0)]
    new_expert = jnp.logical_or(i == 0, tile_expert_ref[i] != prev)
    f = wd_ref.shape[0]
    fc = 512

    @pl.when(jnp.logical_and(used, new_expert))
    def _():
        for c in range(2 * f // fc):
            wgu16[:, c * fc:(c + 1) * fc] = wgu_ref[:, c * fc:(c + 1) * fc].astype(BF16)
        for c in range(f // fc):
            wd16[c * fc:(c + 1) * fc, :] = wd_ref[c * fc:(c + 1) * fc, :].astype(BF16)

    @pl.when(used)
    def _():
        x = xs_ref[...]
        w = ws_ref[...]
        wrep = jnp.concatenate([w] * (fc // w.shape[1]), axis=1)
        acc = None
        for c in range(f // fc):
            gate = jnp.dot(x, wgu16[:, c * fc:(c + 1) * fc], preferred_element_type=F32)
            up = jnp.dot(x, wgu16[:, f + c * fc:f + (c + 1) * fc], preferred_element_type=F32)
            act = (gate * _sigmoid(gate) * up * wrep).astype(BF16)
            part = jnp.dot(act, wd16[c * fc:(c + 1) * fc, :], preferred_element_type=F32)
            acc = part if acc is None else acc + part
        ys_ref[...] = acc.astype(ys_ref.dtype)

    @pl.when(jnp.logical_not(used))
    def _():
        ys_ref[...] = jnp.zeros_like(ys_ref)


def _experts(xs, ws, tables, w_gu_all, w_down_all, widx):
    rows, d = xs.shape
    f = w_down_all.shape[-2]
    tile = lambda w: pl.BlockSpec((MOE_ROW_TILE, w), lambda i, te, nu: (jnp.maximum(jnp.minimum(i, nu[0] - 1), 0), 0))
    grid_spec = pltpu.PrefetchScalarGridSpec(
        num_scalar_prefetch=2,
        grid=(rows // MOE_ROW_TILE,),
        in_specs=[tile(d), tile(ws.shape[1]),
                  pl.BlockSpec((None, None, d, 2 * f), lambda i, te, nu: (widx, te[i], 0, 0)),
                  pl.BlockSpec((None, None, f, d), lambda i, te, nu: (widx, te[i], 0, 0), pipeline_mode=pl.Buffered(1))],
        out_specs=pl.BlockSpec((MOE_ROW_TILE, d), lambda i, te, nu: (i, 0)),
        scratch_shapes=[pltpu.VMEM((d, 2 * f), BF16), pltpu.VMEM((f, d), BF16)],
    )
    return pl.pallas_call(
        _experts_body,
        out_shape=jax.ShapeDtypeStruct((rows, d), BF16),
        grid_spec=grid_spec,
        compiler_params=_cparams("arbitrary"),
        name="moe_experts",
    )(tables['tile_expert'], tables['n_used'], xs, ws, w_gu_all, w_down_all)


def _combine_body(seg_off_ref, seg_len_ref, dst_ref, ys_ref, posc_ref, h_ref, gain_ref, *rest, ne, final):
    *out_refs, ybuf, sems = rest
    t = pl.program_id(0)
    n_tiles = pl.num_programs(0)
    t_tokens, n_rows = h_ref.shape[0], ybuf.shape[1]

    def fetch(tile, act):
        slot = tile % 2

        def make_copy(off, row, size):
            return (pltpu.make_async_copy(ys_ref.at[pl.ds(row, size)], ybuf.at[slot, pl.ds(off, size)], sems.at[slot]),)

        _moe_chunk_copies(tile, ne, seg_off_ref, seg_len_ref, dst_ref, make_copy, act)

    @pl.when(t == 0)
    def _():
        ybuf[...] = jnp.zeros_like(ybuf)
        fetch(t, lambda cp: cp.start())

    @pl.when(t + 1 < n_tiles)
    def _():
        fetch(t + 1, lambda cp: cp.start())

    fetch(t, lambda cp: cp.wait())

    pos = posc_ref[...]
    lane_r = lax.broadcasted_iota(jnp.int32, (t_tokens, n_rows), 1)
    onehot = jnp.where(lane_r == pos[:, 0:1], 1.0, jnp.where(lane_r == pos[:, 1:2], 1.0, 0.0)).astype(BF16)
    out = h_ref[...] + jnp.dot(onehot, ybuf[t % 2], preferred_element_type=F32)
    normed = _rms_norm_f32(out, gain_ref[...])
    if final:
        out_refs[0][...] = normed
    else:
        out_refs[0][...] = out
        out_refs[1][...] = normed.astype(out_refs[1].dtype)


def _combine(ys, posc, h, tables, ne, gain, final):
    n, d = h.shape
    tm = MOE_TOKEN_TILE
    tile = pl.BlockSpec((tm, d), lambda t, *_: (t, 0))
    grid_spec = pltpu.PrefetchScalarGridSpec(
        num_scalar_prefetch=3,
        grid=(n // tm,),
        in_specs=[pl.BlockSpec(memory_space=pl.ANY),
                  pl.BlockSpec((tm, posc.shape[1]), lambda t, *_: (t, 0)),
                  tile,
                  pl.BlockSpec((1, d), lambda t, *_: (0, 0))],
        out_specs=[tile] if final else [tile, tile],
        scratch_shapes=[pltpu.VMEM((2, _moe_compact_rows(ne), d), BF16), pltpu.SemaphoreType.DMA((2,))],
    )
    res = jax.ShapeDtypeStruct((n, d), F32)
    return pl.pallas_call(
        functools.partial(_combine_body, ne=ne, final=final),
        out_shape=[res] if final else [res, jax.ShapeDtypeStruct((n, d), BF16)],
        grid_spec=grid_spec,
        compiler_params=_cparams("arbitrary"),
        name="moe_combine",
    )(tables['seg_off'], tables['seg_len'], tables['dst'], ys, posc, h, gain.reshape(1, d))


def _moe(h, g_all, layer, wr_all, w_gu_all, w_down_all, widx, gain, final):
    n = h.shape[0]
    ne = wr_all.shape[-1]
    xn, posr, wrow, posc, cnt = _router(h, g_all, layer, wr_all, widx)
    n_sorted_rows = _moe_sorted_rows(n, ne)
    tables = _moe_tables(cnt, n_sorted_rows)
    xs, ws = _dispatch(xn, posr, wrow, tables, n_sorted_rows, ne)
    ys = _experts(xs, ws, tables, w_gu_all, w_down_all, widx)
    return _combine(ys, posc, h, tables, ne, gain, final)


def kernel(x, norm_mix_g, w_in, b_in, na_rpb, conv_w, conv_b, lru_wa, lru_ba, lru_wx, lru_bx, lru_lambda, w_branch, w_out, norm_ffn_g, ffn_w_gu, ffn_w_down, router_w, moe_w_gu, moe_w_down, final_g):
    batch, seq, d = x.shape
    depth = w_in.shape[0]
    n = batch * seq
    bw = w_branch.shape[2]
    h = x.reshape(n, d)

    tables = _na_bias_tables(na_rpb)
    twiddles = _fourier_twiddles(seq)
    w_gate = _lru_gate_weights(lru_wa, lru_wx)
    wb16 = w_branch.astype(BF16)
    wo16 = w_out.astype(BF16)
    tn = 6 * bw

    xn = _rms_norm(h, norm_mix_g[0], BF16)
    for l in range(depth):
        last = l == depth - 1
        q, k, v, u_f, u_x, u_g = _proj(xn, w_in, b_in, l, 0, tn, 6)
        y_a = _neighbourhood_attention(q, k, v, tables, l, batch)
        y_b = _fourier_mix(u_f, batch, twiddles)
        h_c = _recurrent_branch(u_x, conv_w, conv_b, w_gate, lru_ba, lru_bx, lru_lambda, l, batch)
        h = _merge(y_a, y_b, h_c, u_g, xn, h, w_in, b_in, wb16, wo16, l)
        next_gain = final_g if last else norm_mix_g[l + 1]
        if l % 2 == 0:
            h, xn = _ffn(h, norm_ffn_g[l], next_gain, ffn_w_gu, ffn_w_down, l // 2)
            if last:
                h = _rms_norm(h, final_g, F32)
        elif last:
            (h,) = _moe(h, norm_ffn_g, l, router_w, moe_w_gu, moe_w_down, l // 2, next_gain, final=True)
        else:
            h, xn = _moe(h, norm_ffn_g, l, router_w, moe_w_gu, moe_w_down, l // 2, next_gain, final=False)
    return h.reshape(batch, seq, d)
```

```python
import functools

import numpy as np
import jax
import jax.numpy as jnp
from jax import lax
from jax.experimental import pallas as pl
from jax.experimental.pallas import tpu as pltpu

F32 = jnp.float32
BF16 = jnp.bfloat16

RMS_EPS = 1e-6
GRID_W = 64
NA_HEADS = 8
NA_HEAD_DIM = 64
NA_KH = 8
NA_KW = 16
NA_ROWS_PER_BLOCK = 8
NA_ROWS_IN_FLIGHT = 4
LRU_C = 8.0
CONV_W = 4
CONV_PAD_LEFT = 2
N_EXPERTS = 8
MASK_VALUE = -1e30

VMEM_LIMIT_BYTES = 56 * 1024 * 1024


def _cparams(*sem):
    return pltpu.CompilerParams(dimension_semantics=sem, vmem_limit_bytes=VMEM_LIMIT_BYTES)


def _rms_norm_f32(x, g):
    ms = jnp.mean(x * x, axis=-1, keepdims=True)
    return x * lax.rsqrt(ms + RMS_EPS) * g


def _sigmoid(x):
    return 1.0 / (1.0 + jnp.exp(-x))


def _rms_norm_body(h_ref, g_ref, o_ref):
    o_ref[...] = _rms_norm_f32(h_ref[...], g_ref[...]).astype(o_ref.dtype)


def _rms_norm(h, g, out_dtype, tm=1024):
    n, d = h.shape
    return pl.pallas_call(
        _rms_norm_body,
        out_shape=jax.ShapeDtypeStruct((n, d), out_dtype),
        grid=(n // tm,),
        in_specs=[pl.BlockSpec((tm, d), lambda t: (t, 0)), pl.BlockSpec((1, d), lambda t: (0, 0))],
        out_specs=pl.BlockSpec((tm, d), lambda t: (t, 0)),
        compiler_params=_cparams("arbitrary"),
        name="rms_norm",
    )(h, g.reshape(1, d))


def _proj_body(x_ref, w_ref, b_ref, *out_refs):
    r = jnp.dot(x_ref[...], w_ref[...].astype(BF16), preferred_element_type=F32) + b_ref[...]
    width = r.shape[1] // len(out_refs)
    for i, o in enumerate(out_refs):
        o[...] = r[:, i * width:(i + 1) * width].astype(o.dtype)


def _proj(xn, w_all, b_all, layer, col_block, tn, n_out, tm=1024):
    n, d = xn.shape
    depth = w_all.shape[0]
    b3 = b_all.reshape(depth, 1, -1)
    width = tn // n_out
    return pl.pallas_call(
        _proj_body,
        out_shape=[jax.ShapeDtypeStruct((n, width), BF16) for _ in range(n_out)],
        grid=(n // tm,),
        in_specs=[
            pl.BlockSpec((tm, d), lambda t: (t, 0)),
            pl.BlockSpec((None, d, tn), lambda t: (layer, 0, col_block)),
            pl.BlockSpec((None, 1, tn), lambda t: (layer, 0, col_block)),
        ],
        out_specs=[pl.BlockSpec((tm, width), lambda t: (t, 0)) for _ in range(n_out)],
        compiler_params=_cparams("arbitrary"),
        name="in_proj",
    )(xn, w_all, b3)


def _na_bias_tables(rpb_all):
    cols = np.arange(GRID_W)
    col_start = np.clip(cols - NA_KW // 2, 0, GRID_W - NA_KW)
    cc = np.arange(GRID_W)[None, :]
    in_win = (cc >= col_start[:, None]) & (cc < col_start[:, None] + NA_KW)
    col_off = cc - cols[:, None] + (NA_KW - 1)
    onehot = np.zeros((2 * NA_KW - 1, GRID_W, GRID_W), np.float32)
    cq, ck = np.nonzero(in_win)
    onehot[col_off[cq, ck], cq, ck] = 1.0
    t = jnp.einsum('lhro,ocd->lhrcd', rpb_all.astype(F32), jnp.asarray(onehot),
                   precision=lax.Precision.HIGHEST)
    t = jnp.where(jnp.asarray(in_win)[None, None, None], t, MASK_VALUE)
    depth, heads, n_off = t.shape[:3]
    t = t.reshape(depth, heads // 2, 2, n_off, GRID_W, GRID_W)
    t = jnp.transpose(t, (0, 1, 3, 2, 4, 5)).reshape(depth, heads // 2, n_off, 2 * GRID_W, GRID_W)
    return jnp.concatenate([t[:, :, :-1], t[:, :, 1:]], axis=-1)


def _na_body(q_ref, kbuf, vbuf, tbl_ref, o_ref, s_scr, e_scr, *, n_blocks):
    j = pl.program_id(1)
    band = NA_KH * GRID_W
    half = NA_KH // 2
    lane = lax.broadcasted_iota(jnp.int32, (GRID_W, 2 * NA_HEAD_DIM), 1)
    lo = lane < NA_HEAD_DIM
    qscale = NA_HEAD_DIM ** -0.5
    head_mask = (jnp.where(lo, qscale, 0.0).astype(BF16), jnp.where(lo, 0.0, qscale).astype(BF16))

    n_pairs = NA_HEADS // 2
    cols = [slice(p * 2 * NA_HEAD_DIM, (p + 1) * 2 * NA_HEAD_DIM) for p in range(n_pairs)]

    def row_offsets(rl):
        start_first = jnp.maximum(rl - half, 0)
        start_last = jnp.minimum(rl + half, NA_ROWS_PER_BLOCK)
        start = jnp.where(j == 0, start_first, jnp.where(j == n_blocks - 1, start_last, rl))
        delta = jnp.where(j == 0, jnp.minimum(rl, half), jnp.where(j == n_blocks - 1, jnp.maximum(rl, half), half))
        return pl.multiple_of(rl * GRID_W, GRID_W), pl.multiple_of(start * GRID_W, GRID_W), delta

    def scores(rl, slot):
        qrow, krow, delta = row_offsets(rl)
        for p in range(n_pairs):
            q2 = q_ref[pl.ds(qrow, GRID_W), cols[p]]
            qs = jnp.concatenate([q2 * head_mask[0], q2 * head_mask[1]], axis=0)
            k2 = kbuf[pl.ds(krow, band), cols[p]]
            s = lax.dot_general(qs, k2, (((1,), (1,)), ((), ())), preferred_element_type=F32)
            bias = jnp.concatenate([tbl_ref[p, 2 * m - delta + (NA_KH - 1)] for m in range(NA_KH // 2)], axis=1)
            s_scr[slot, p] = s + bias

    def softmax(slot):
        inv_l = []
        for p in range(n_pairs):
            s = s_scr[slot, p]
            m = jnp.max(s, axis=-1, keepdims=True)
            e = jnp.exp(s - m)
            inv_l.append(1.0 / jnp.sum(e, axis=-1, keepdims=True))
            e_scr[slot, p] = e.astype(BF16)
        return inv_l

    def weighted_values(rl, slot, inv_l):
        qrow, krow, _ = row_offsets(rl)
        for p in range(n_pairs):
            v2 = vbuf[pl.ds(krow, band), cols[p]]
            o = jnp.dot(e_scr[slot, p], v2, preferred_element_type=F32) * inv_l[p]
            o_ref[pl.ds(qrow, GRID_W), cols[p]] = jnp.where(lo, o[:GRID_W], o[GRID_W:]).astype(o_ref.dtype)

    def rows_body(it, carry):
        rows = [it * NA_ROWS_IN_FLIGHT + r for r in range(NA_ROWS_IN_FLIGHT)]
        for slot, rl in enumerate(rows):
            scores(rl, slot)
        inv = [softmax(slot) for slot in range(NA_ROWS_IN_FLIGHT)]
        for slot, rl in enumerate(rows):
            weighted_values(rl, slot, inv[slot])
        return carry

    lax.fori_loop(0, NA_ROWS_PER_BLOCK // NA_ROWS_IN_FLIGHT, rows_body, 0)


def _neighbourhood_attention(q, k, v, tables, layer, batch):
    n, width = q.shape
    n_pairs = NA_HEADS // 2
    blk = NA_ROWS_PER_BLOCK * GRID_W
    n_blocks = n // batch // blk
    assert n_blocks >= 2 and NA_KH == NA_ROWS_PER_BLOCK
    s = n // batch
    half_blk = blk // 2

    def window_start(b, j):
        start = b * s + jnp.clip(j * blk - half_blk, 0, s - 2 * blk)
        return pl.multiple_of(start, half_blk), 0

    tile = pl.BlockSpec((blk, width), lambda b, j: (b * n_blocks + j, 0))
    window = pl.BlockSpec((pl.Element(2 * blk), pl.Element(width)), window_start)
    return pl.pallas_call(
        functools.partial(_na_body, n_blocks=n_blocks),
        out_shape=jax.ShapeDtypeStruct((n, width), BF16),
        grid=(batch, n_blocks),
        in_specs=[tile, window, window,
                  pl.BlockSpec((None,) + tables.shape[1:], lambda b, j: (layer, 0, 0, 0, 0))],
        out_specs=tile,
        scratch_shapes=[pltpu.VMEM((NA_ROWS_IN_FLIGHT, n_pairs, 2 * GRID_W, NA_KH * GRID_W), F32),
                        pltpu.VMEM((NA_ROWS_IN_FLIGHT, n_pairs, 2 * GRID_W, NA_KH * GRID_W), BF16)],
        compiler_params=_cparams("arbitrary", "arbitrary"),
        name="neigh_attn",
    )(q, k, v, tables)


FN_N2 = 128
FN_GROUP_DIM = 64
FN_GROUP = 16


def _dft_cos_sin(n):
    ang = 2.0 * np.pi * (np.outer(np.arange(n), np.arange(n)) % n) / n
    return np.cos(ang), np.sin(ang)


def _fourier_stage1_body(x_ref, f_ref, tc_ref, ts_ref, zr_ref, zi_ref):
    n1, n_b, width = x_ref.shape
    xs = jnp.swapaxes(x_ref[...], 0, 1)
    rep = width // tc_ref.shape[2]
    zr_all, zi_all = [], []
    for b in range(n_b):
        z = jnp.dot(f_ref[...], xs[b], preferred_element_type=F32)
        zr, zi = z[:n1], z[n1:]
        tc = jnp.concatenate([tc_ref[b]] * rep, axis=1)
        ts = jnp.concatenate([ts_ref[b]] * rep, axis=1)
        zr_all.append((zr * tc + zi * ts).astype(zr_ref.dtype))
        zi_all.append((zi * tc - zr * ts).astype(zi_ref.dtype))
    zr_ref[...] = jnp.swapaxes(jnp.stack(zr_all, axis=0), 0, 1)
    zi_ref[...] = jnp.swapaxes(jnp.stack(zi_all, axis=0), 0, 1)


def _fourier_stage2_body(zr_ref, zi_ref, f_ref, c_ref, o_ref, *, scale):
    n2 = zr_ref.shape[1]
    outs = []
    for i in range(zr_ref.shape[0]):
        z = jnp.concatenate([zr_ref[i], zi_ref[i]], axis=0)
        y = jnp.dot(f_ref[...], z, preferred_element_type=F32).astype(BF16)
        lanes = c_ref.shape[1]
        cols = [slice(cb * lanes, (cb + 1) * lanes) for cb in range(y.shape[1] // lanes)]
        out = jnp.concatenate(
            [jnp.dot(jnp.concatenate([y[:n2, cs], y[n2:, cs]], axis=1), c_ref[...], preferred_element_type=F32)
             for cs in cols], axis=1) * scale
        outs.append(out.astype(o_ref.dtype))
    o_ref[...] = jnp.swapaxes(jnp.stack(outs, axis=0), 0, 1)


def _fourier_mix(u, batch, twiddles):
    n, width = u.shape
    s = n // batch
    n2 = FN_N2
    n1 = s // n2
    tc, ts = twiddles
    c1, s1 = _dft_cos_sin(n1)
    f1 = jnp.asarray(np.concatenate([c1, -s1], axis=0), F32).astype(BF16)
    c2, s2 = _dft_cos_sin(n2)
    f2 = jnp.asarray(np.block([[c2, s2], [-s2, c2]]), F32).astype(BF16)
    cg, sg = _dft_cos_sin(FN_GROUP_DIM)
    eye = np.eye(128 // FN_GROUP_DIM)
    fc = jnp.asarray(np.concatenate([np.kron(eye, cg), np.kron(eye, sg)], axis=0), F32).astype(BF16)

    grp = FN_GROUP
    pos_blk = pl.BlockSpec((None, n1, grp, width), lambda b, j: (b, 0, j, 0))
    tw_blk = pl.BlockSpec((grp,) + tc.shape[1:], lambda b, j: (j, 0, 0))
    zr, zi = pl.pallas_call(
        _fourier_stage1_body,
        out_shape=[jax.ShapeDtypeStruct((batch, n1, n2, width), BF16)] * 2,
        grid=(batch, n2 // grp),
        in_specs=[pos_blk, pl.BlockSpec((2 * n1, n1), lambda b, j: (0, 0)), tw_blk, tw_blk],
        out_specs=[pos_blk, pos_blk],
        compiler_params=_cparams("arbitrary", "arbitrary"),
        name="fourier_stage1",
    )(u.reshape(batch, n1, n2, width), f1, tc, ts)

    freq_blk = pl.BlockSpec((None, grp, n2, width), lambda b, j: (b, j, 0, 0))
    out = pl.pallas_call(
        functools.partial(_fourier_stage2_body, scale=float(1.0 / np.sqrt(s * FN_GROUP_DIM))),
        out_shape=jax.ShapeDtypeStruct((batch, n2, n1, width), BF16),
        grid=(batch, n1 // grp),
        in_specs=[freq_blk, freq_blk,
                  pl.BlockSpec((2 * n2, 2 * n2), lambda b, j: (0, 0)),
                  pl.BlockSpec(fc.shape, lambda b, j: (0, 0))],
        out_specs=pl.BlockSpec((None, n2, grp, width), lambda b, j: (b, 0, j, 0)),
        compiler_params=_cparams("arbitrary", "arbitrary"),
        name="fourier_stage2",
    )(zr, zi, f2, fc)
    return out.reshape(n, width)


def _fourier_twiddles(s):
    n2 = FN_N2
    n1 = s // n2
    ang = (2.0 * np.pi / s) * (jnp.arange(n2, dtype=F32)[:, None] * jnp.arange(n1, dtype=F32)[None, :])
    rep = lambda t: jnp.broadcast_to(t[:, :, None], (n2, n1, 128))
    return rep(jnp.cos(ang)), rep(jnp.sin(ang))


LRU_LANES = 128
LRU_SEGMENTS = 16
LRU_JCHUNK = 32
SUBLANES = 8


def _lru_gate_weights(wa_all, wx_all):
    depth, _, nb, db, _ = wa_all.shape
    ncol = nb // 2

    def blockdiag(w):
        w = w.reshape(depth, ncol, 2, db, db)
        z = jnp.zeros_like(w[:, :, 0])
        top = jnp.concatenate([w[:, :, 0], z], axis=-1)
        bot = jnp.concatenate([z, w[:, :, 1]], axis=-1)
        return jnp.concatenate([top, bot], axis=-2)

    parts = [blockdiag(wa_all[:, 0]), blockdiag(wx_all[:, 0]), blockdiag(wa_all[:, 1]), blockdiag(wx_all[:, 1])]
    return jnp.concatenate(parts, axis=-1).astype(BF16)


def _gelu_tanh(x):
    return 0.5 * x * (1.0 + jnp.tanh(np.sqrt(2.0 / np.pi) * (x + 0.044715 * (x * x * x))))


def _lru_body(useq_ref, cw_ref, cb_ref, w_ref, ba_ref, bx_ref, lam_ref, oseq_ref, ux_ref, o_ref, h_scr, p_scr):
    n_j, n_g, lanes = ux_ref.shape
    ux_ref[...] = jnp.swapaxes(useq_ref[...].reshape(n_g, n_j, lanes), 0, 1)
    jc = LRU_JCHUNK
    n_chunks = n_j // jc
    seg = lax.broadcasted_iota(jnp.int32, (n_g, lanes), 0)

    def from_prev_segment(x):
        return jnp.where(seg >= 1, pltpu.roll(x, 1, axis=0), 0.0)

    def from_next_segment(x):
        return jnp.where(seg < n_g - 1, pltpu.roll(x, n_g - 1, axis=0), 0.0)

    def conv_chunk(j0):
        main = ux_ref[pl.ds(j0, jc)].astype(F32)
        lo_in = ux_ref[pl.ds(jnp.maximum(j0 - CONV_PAD_LEFT, 0), CONV_PAD_LEFT)].astype(F32)
        tail = ux_ref[n_j - CONV_PAD_LEFT:n_j].astype(F32)
        lo_wrap = jnp.stack([from_prev_segment(tail[r]) for r in range(CONV_PAD_LEFT)], axis=0)
        lo = jnp.where(j0 > 0, lo_in, lo_wrap)
        n_hi = CONV_W - 1 - CONV_PAD_LEFT
        hi_in = ux_ref[pl.ds(jnp.minimum(j0 + jc, n_j - n_hi), n_hi)].astype(F32)
        head = ux_ref[0:n_hi].astype(F32)
        hi_wrap = jnp.stack([from_next_segment(head[r]) for r in range(n_hi)], axis=0)
        hi = jnp.where(j0 + jc < n_j, hi_in, hi_wrap)
        ext = jnp.concatenate([lo, main, hi], axis=0)
        c = ext[0:jc] * cw_ref[0:1, :] + cb_ref[...]
        for tap in range(1, CONV_W):
            c = c + ext[tap:tap + jc] * cw_ref[tap:tap + 1, :]
        return c

    def gates(c, pre, d, j0):
        r = _sigmoid(pre[:, 2 * d * lanes:(2 * d + 1) * lanes] + ba_ref[d:d + 1, :])
        i = _sigmoid(pre[:, (2 * d + 1) * lanes:(2 * d + 2) * lanes] + bx_ref[d:d + 1, :])
        lam = lam_ref[d:d + 1, :]
        softplus = jnp.maximum(-lam, 0.0) + jnp.log(1.0 + jnp.exp(-jnp.abs(lam)))
        a = jnp.exp(-LRU_C * r * softplus)
        om = 1.0 - a * a
        mult = jnp.where(om > 0.0, om * lax.rsqrt(om), 0.0)
        a = a.reshape(jc, n_g, lanes)
        b = (mult * i).reshape(jc, n_g, lanes) * c
        edge = (jc - 1) * d
        i_edge = i.reshape(jc, n_g, lanes)[edge]
        is_first = (seg == (n_g - 1) * d) & (j0 + edge == (n_j - 1) * d)
        b_edge = jnp.where(is_first, i_edge * c[edge], b[edge])[None]
        b = jnp.concatenate([b_edge, b[1:]] if d == 0 else [b[:-1], b_edge], axis=0)
        return a, b

    def local_scan(a, b, carry, d):
        h, p = carry
        hs, ps = [None] * jc, [None] * jc
        for jj in (range(jc) if d == 0 else range(jc - 1, -1, -1)):
            h = a[jj] * h + b[jj]
            p = a[jj] * p
            hs[jj], ps[jj] = h, p
        return jnp.stack(hs, axis=0), jnp.stack(ps, axis=0), (h, p)

    scan_init = (jnp.zeros((n_g, lanes), F32), jnp.ones((n_g, lanes), F32))

    def gate_and_forward_body(ci, carry):
        j0 = ci * jc
        rows = pl.ds(j0, jc)
        c = conv_chunk(j0)
        pre = jnp.dot(c.reshape(jc * n_g, lanes).astype(BF16), w_ref[...], preferred_element_type=F32)
        a_b, b_b = gates(c, pre, 1, j0)
        p_scr[1, rows] = a_b
        h_scr[1, rows] = b_b
        a_f, b_f = gates(c, pre, 0, j0)
        h_scr[0, rows], p_scr[0, rows], carry = local_scan(a_f, b_f, carry, 0)
        return carry

    lax.fori_loop(0, n_chunks, gate_and_forward_body, scan_init)

    def backward_body(ci, carry):
        rows = pl.ds((n_chunks - 1 - ci) * jc, jc)
        h_scr[1, rows], p_scr[1, rows], carry = local_scan(p_scr[1, rows], h_scr[1, rows], carry, 1)
        return carry

    lax.fori_loop(0, n_chunks, backward_body, scan_init)

    def carry_in(d):
        edge = n_j - 1 if d == 0 else 0
        h_end, p_end = h_scr[d, edge], p_scr[d, edge]
        state = jnp.zeros((1, lanes), F32)
        out = jnp.zeros((n_g, lanes), F32)
        for g in (range(n_g) if d == 0 else range(n_g - 1, -1, -1)):
            out = jnp.where(seg == g, state, out)
            state = h_end[g:g + 1] + p_end[g:g + 1] * state
        return out

    e_fwd, e_bwd = carry_in(0), carry_in(1)

    def out_body(ci, carry):
        j0 = ci * jc
        rows = pl.ds(j0, jc)
        h = h_scr[0, rows] + p_scr[0, rows] * e_fwd + h_scr[1, rows] + p_scr[1, rows] * e_bwd
        o_ref[rows] = h.astype(o_ref.dtype)
        return carry

    lax.fori_loop(0, n_chunks, out_body, 0)
    oseq_ref[...] = jnp.swapaxes(o_ref[...], 0, 1).reshape(n_g * n_j, lanes)


def _recurrent_branch(u_x, conv_w, conv_b, w_gate, ba, bx, lam, layer, batch):
    n, width = u_x.shape
    s = n // batch
    depth = conv_w.shape[0]
    ncol = width // LRU_LANES
    n_g = LRU_SEGMENTS
    n_j = s // n_g
    assert n_j % LRU_JCHUNK == 0
    cb3 = conv_b.reshape(depth, 1, width)
    seq_spec = pl.BlockSpec((None, s, LRU_LANES), lambda b, c: (b, 0, c))
    par = lambda rows: pl.BlockSpec((None, rows, LRU_LANES), lambda b, c: (layer, 0, c))
    seg_copy = pltpu.VMEM((n_j, n_g, LRU_LANES), BF16)
    state = pltpu.VMEM((2, n_j, n_g, LRU_LANES), F32)
    out = pl.pallas_call(
        _lru_body,
        out_shape=jax.ShapeDtypeStruct((batch, s, width), BF16),
        grid=(batch, ncol),
        in_specs=[seq_spec, par(CONV_W), par(1),
                  pl.BlockSpec((None, None, LRU_LANES, 4 * LRU_LANES), lambda b, c: (layer, c, 0, 0)),
                  par(2), par(2), par(2)],
        out_specs=seq_spec,
        scratch_shapes=[seg_copy, seg_copy, state, state],
        compiler_params=_cparams("arbitrary", "arbitrary"),
        name="rg_lru",
    )(u_x.reshape(batch, s, width), conv_w, cb3, w_gate, ba, bx, lam)
    return out.reshape(n, width)


def _merge_body(ya_ref, yb_ref, hc_ref, ug_ref, xn_ref, h_ref, wgate_ref, bgate_ref, wb_ref, wo_ref, o_ref,
                wgate16, wb16, wo16):
    d = h_ref.shape[1]

    @pl.when(pl.program_id(0) == 0)
    def _():
        for kbr in range(wgate16.shape[1] // d):
            wgate16[:, kbr * d:(kbr + 1) * d] = wgate_ref[:, kbr * d:(kbr + 1) * d].astype(BF16)
            wb16[kbr] = wb_ref[kbr].astype(BF16)
        wo16[...] = wo_ref[...].astype(BF16)

    xn = xn_ref[...]
    yc = (hc_ref[...].astype(F32) * _gelu_tanh(ug_ref[...].astype(F32))).astype(BF16)
    merged = None
    for kbr, y in enumerate((ya_ref[...], yb_ref[...], yc)):
        cols = slice(kbr * d, (kbr + 1) * d)
        gate = jnp.dot(xn, wgate16[:, cols], preferred_element_type=F32) + bgate_ref[:, cols]
        ybr = jnp.dot(y, wb16[kbr], preferred_element_type=F32)
        term = _sigmoid(gate) * ybr
        merged = term if merged is None else merged + term
    o_ref[...] = h_ref[...] + jnp.dot(merged.astype(BF16), wo16[...], preferred_element_type=F32)


def _merge(ya, yb, hc, ug, xn, h, w_in_all, b_in_all, wb_all, wo_all, layer, tm=512):
    n, d = h.shape
    bw = ya.shape[1]
    depth = w_in_all.shape[0]
    gate_block = w_in_all.shape[2] // (3 * d) - 1
    assert (gate_block + 1) * 3 * d == w_in_all.shape[2]
    once = pl.Buffered(1)
    ytile = pl.BlockSpec((tm, bw), lambda t: (t, 0))
    return pl.pallas_call(
        _merge_body,
        out_shape=jax.ShapeDtypeStruct((n, d), F32),
        grid=(n // tm,),
        in_specs=[ytile, ytile, ytile, ytile,
                  pl.BlockSpec((tm, d), lambda t: (t, 0)),
                  pl.BlockSpec((tm, d), lambda t: (t, 0)),
                  pl.BlockSpec((None, d, 3 * d), lambda t: (layer, 0, gate_block), pipeline_mode=once),
                  pl.BlockSpec((None, 1, 3 * d), lambda t: (layer, 0, gate_block)),
                  pl.BlockSpec((None, 3, bw, d), lambda t: (layer, 0, 0, 0), pipeline_mode=once),
                  pl.BlockSpec((None, d, d), lambda t: (layer, 0, 0), pipeline_mode=once)],
        out_specs=pl.BlockSpec((tm, d), lambda t: (t, 0)),
        scratch_shapes=[pltpu.VMEM((d, 3 * d), BF16), pltpu.VMEM((3, bw, d), BF16), pltpu.VMEM((d, d), BF16)],
        compiler_params=_cparams("arbitrary"),
        name="branch_merge",
    )(ya, yb, hc, ug, xn, h, w_in_all, b_in_all.reshape(depth, 1, -1), wb_all, wo_all)


def _ffn_body(h_ref, g_ref, gnext_ref, wg_ref, wu_ref, wd_ref, o_ref, xnext_ref, xn_ref):
    j = pl.program_id(1)

    @pl.when(j == 0)
    def _():
        h = h_ref[...]
        xn_ref[...] = _rms_norm_f32(h, g_ref[...]).astype(BF16)
        o_ref[...] = h

    xn = xn_ref[...]
    gate = jnp.dot(xn, wg_ref[...].astype(BF16), preferred_element_type=F32)
    up = jnp.dot(xn, wu_ref[...].astype(BF16), preferred_element_type=F32)
    act = gate * _sigmoid(gate) * up
    o_ref[...] += jnp.dot(act.astype(BF16), wd_ref[...].astype(BF16), preferred_element_type=F32)

    @pl.when(j == pl.num_programs(1) - 1)
    def _():
        xnext_ref[...] = _rms_norm_f32(o_ref[...], gnext_ref[...]).astype(xnext_ref.dtype)


def _ffn(h, g, g_next, w_gu_all, w_down_all, widx, tm=1024, tf=768):
    n, d = h.shape
    nf = w_down_all.shape[-2] // tf
    gain = pl.BlockSpec((1, d), lambda t, j: (0, 0))
    tile = pl.BlockSpec((tm, d), lambda t, j: (t, 0))
    return pl.pallas_call(
        _ffn_body,
        out_shape=[jax.ShapeDtypeStruct((n, d), F32), jax.ShapeDtypeStruct((n, d), BF16)],
        grid=(n // tm, nf),
        in_specs=[tile, gain, gain,
                  pl.BlockSpec((None, d, tf), lambda t, j: (widx, 0, j)),
                  pl.BlockSpec((None, d, tf), lambda t, j: (widx, 0, j + nf)),
                  pl.BlockSpec((None, tf, d), lambda t, j: (widx, j, 0))],
        out_specs=[tile, tile],
        scratch_shapes=[pltpu.VMEM((tm, d), BF16)],
        compiler_params=_cparams("arbitrary", "arbitrary"),
        name="dense_ffn",
    )(h, g.reshape(1, d), g_next.reshape(1, d), w_gu_all, w_gu_all, w_down_all)


MOE_TOKEN_TILE = 512
MOE_ROW_TILE = 512
MOE_SEG_ALIGN = 16
MOE_TOP_K = 2
MOE_CHUNK_SIZES = tuple(MOE_SEG_ALIGN << b for b in range(5, -1, -1))
assert MOE_CHUNK_SIZES[0] == MOE_TOKEN_TILE


def _moe_compact_rows(ne):
    rows = MOE_TOKEN_TILE * MOE_TOP_K + ne * (MOE_SEG_ALIGN - 1)
    return -(-rows // MOE_SEG_ALIGN) * MOE_SEG_ALIGN


def _moe_sorted_rows(n, ne):
    rows = n * MOE_TOP_K + (n // MOE_TOKEN_TILE) * ne * (MOE_SEG_ALIGN - 1) + ne * (MOE_ROW_TILE - MOE_SEG_ALIGN)
    return -(-rows // MOE_ROW_TILE) * MOE_ROW_TILE


def _router_body(h_ref, g_ref, wrt_ref, xn_ref, posr_ref, wrow_ref, posc_ref, cnt_ref, before_scr, eye_scr):
    t_tokens = h_ref.shape[0]

    @pl.when(pl.program_id(0) == 0)
    def _():
        r_i = lax.broadcasted_iota(jnp.int32, (t_tokens, t_tokens), 0)
        c_i = lax.broadcasted_iota(jnp.int32, (t_tokens, t_tokens), 1)
        before_scr[...] = jnp.where(r_i < c_i, 1.0, 0.0).astype(BF16)
        eye_scr[...] = jnp.where(r_i == c_i, 1.0, 0.0).astype(BF16)

    xn = _rms_norm_f32(h_ref[...], g_ref[...])
    xn_hi = xn.astype(BF16)
    xn_ref[...] = xn_hi
    nt_dims = (((1,), (1,)), ((), ()))
    xn_lo = (xn - xn_hi.astype(F32)).astype(BF16)
    w = wrt_ref[...]
    ne = w.shape[0]
    w_hi = w.astype(BF16)
    w_lo = (w - w_hi.astype(F32)).astype(BF16)
    by_hi = lax.dot_general(jnp.concatenate([w_hi, w_lo], axis=0), xn_hi, nt_dims, preferred_element_type=F32)
    logits = by_hi[:ne] + by_hi[ne:] + lax.dot_general(w_hi, xn_lo, nt_dims, preferred_element_type=F32)
    sub = lax.broadcasted_iota(jnp.int32, logits.shape, 0)
    m1 = jnp.max(logits, axis=0, keepdims=True)
    i1 = jnp.min(jnp.where(logits == m1, sub, ne), axis=0, keepdims=True)
    rest = jnp.where(sub == i1, -jnp.inf, logits)
    m2 = jnp.max(rest, axis=0, keepdims=True)
    i2 = jnp.min(jnp.where(rest == m2, sub, ne), axis=0, keepdims=True)
    e = jnp.exp(m2 - m1)
    wrow_ref[...] = jnp.concatenate([1.0 / (1.0 + e), e / (1.0 + e)], axis=0)

    sel1, sel2 = sub == i1, sub == i2
    memb = jnp.where(sel1, 1.0, jnp.where(sel2, 1.0, 0.0))
    rank = jnp.dot(memb.astype(BF16), before_scr[...], preferred_element_type=F32)
    sub_c = lax.broadcasted_iota(jnp.int32, cnt_ref.shape, 0)
    cnt_out = jnp.zeros(cnt_ref.shape, F32)
    base = rank
    running = jnp.zeros((1, 1), F32)
    for ex in range(ne):
        c = jnp.sum(memb[ex:ex + 1, :], axis=1, keepdims=True)
        cnt_out = jnp.where(sub_c == ex, c, cnt_out)
        base = jnp.where(sub == ex, base + running, base)
        running = running + jnp.floor((c + (MOE_SEG_ALIGN - 1)) * (1.0 / MOE_SEG_ALIGN)) * MOE_SEG_ALIGN
    cnt_ref[...] = cnt_out.astype(jnp.int32)
    pos1 = jnp.sum(jnp.where(sel1, base, 0.0), axis=0, keepdims=True)
    pos2 = jnp.sum(jnp.where(sel2, base, 0.0), axis=0, keepdims=True)
    posr_ref[...] = jnp.concatenate([pos1, pos2], axis=0).astype(jnp.int32)
    digits = []
    for pos in (pos1, pos2):
        hi = jnp.floor(pos * (1.0 / 128.0))
        digits += [hi, pos - 128.0 * hi]
    pad = jnp.zeros((2 * SUBLANES - len(digits), t_tokens), F32)
    dig_t = lax.dot_general(eye_scr[...], jnp.concatenate(digits + [pad], axis=0).astype(BF16), nt_dims,
                            preferred_element_type=F32)
    d_i = lax.broadcasted_iota(jnp.int32, (2 * SUBLANES, posc_ref.shape[1]), 0)
    c_i = lax.broadcasted_iota(jnp.int32, (2 * SUBLANES, posc_ref.shape[1]), 1)
    recombine = jnp.where(d_i == 2 * c_i, 128.0, jnp.where(d_i == 2 * c_i + 1, 1.0, 0.0)).astype(BF16)
    posc = jnp.dot(dig_t.astype(BF16), recombine, preferred_element_type=F32)
    posc_ref[...] = posc.astype(jnp.int32)


def _router(h, g_all, layer, wr_all, widx):
    n, d = h.shape
    depth = g_all.shape[0]
    ne = wr_all.shape[-1]
    tm = MOE_TOKEN_TILE
    nt = n // tm
    wrt = jnp.swapaxes(wr_all, 1, 2)
    return pl.pallas_call(
        _router_body,
        out_shape=[jax.ShapeDtypeStruct((n, d), BF16),
                   jax.ShapeDtypeStruct((MOE_TOP_K, n), jnp.int32),
                   jax.ShapeDtypeStruct((MOE_TOP_K, n), F32),
                   jax.ShapeDtypeStruct((n, SUBLANES), jnp.int32),
                   jax.ShapeDtypeStruct((nt, ne, 128), jnp.int32)],
        grid=(nt,),
        in_specs=[pl.BlockSpec((tm, d), lambda t: (t, 0)),
                  pl.BlockSpec((None, 1, d), lambda t: (layer, 0, 0)),
                  pl.BlockSpec((None, ne, d), lambda t: (widx, 0, 0))],
        out_specs=[pl.BlockSpec((tm, d), lambda t: (t, 0)),
                   pl.BlockSpec((MOE_TOP_K, tm), lambda t: (0, t)),
                   pl.BlockSpec((MOE_TOP_K, tm), lambda t: (0, t)),
                   pl.BlockSpec((tm, SUBLANES), lambda t: (t, 0)),
                   pl.BlockSpec((None, ne, 128), lambda t: (t, 0, 0))],
        scratch_shapes=[pltpu.VMEM((tm, tm), BF16), pltpu.VMEM((tm, tm), BF16)],
        compiler_params=_cparams("arbitrary"),
        name="router",
    )(h, g_all.reshape(depth, 1, d), wrt)


def _moe_tables(cnt, n_sorted_rows):
    cnt = cnt[:, :, 0]
    nt, ne = cnt.shape
    seg = (cnt + (MOE_SEG_ALIGN - 1)) // MOE_SEG_ALIGN * MOE_SEG_ALIGN
    seg_off = jnp.cumsum(seg, axis=1) - seg
    e_rows = jnp.sum(seg, axis=0)
    e_tiles = (e_rows + (MOE_ROW_TILE - 1)) // MOE_ROW_TILE
    e_cum = jnp.cumsum(e_tiles)
    e_base = (e_cum - e_tiles) * MOE_ROW_TILE
    dst = e_base[None, :] + jnp.cumsum(seg, axis=0) - seg
    n_used = e_cum[-1]
    tile_ids = jnp.minimum(jnp.arange(n_sorted_rows // MOE_ROW_TILE, dtype=jnp.int32), n_used - 1)
    tile_expert = jnp.sum(tile_ids[:, None] >= e_cum[None, :], axis=1)
    last_tile_row = e_base + (e_tiles - 1) * MOE_ROW_TILE
    i32 = lambda a: a.astype(jnp.int32)
    return dict(seg_off=i32(seg_off.reshape(-1)), seg_len=i32(seg.reshape(-1)), dst=i32(dst.reshape(-1)),
                n_used=i32(n_used.reshape(1)), tile_expert=i32(tile_expert),
                last_tile_row=i32(last_tile_row), has_rows=i32(e_tiles > 0))


def _moe_chunk_copies(t, ne, seg_off_ref, seg_len_ref, dst_ref, make_copy, act):
    for ex in range(ne):
        idx = t * ne + ex
        off = seg_off_ref[idx]
        ln = seg_len_ref[idx]
        row = dst_ref[idx]
        for size in MOE_CHUNK_SIZES:
            take = (ln & size) != 0

            @pl.when(take)
            def _(off=off, row=row, size=size):
                for cp in make_copy(pl.multiple_of(off, MOE_SEG_ALIGN), pl.multiple_of(row, MOE_SEG_ALIGN), size):
                    act(cp)

            step = jnp.where(take, size, 0)
            off = off + step
            row = row + step


def _dispatch_body(seg_off_ref, seg_len_ref, dst_ref, last_row_ref, has_rows_ref, n_used_ref,
                   xn_ref, posr_ref, wrow_ref, xs_ref, ws_ref, cbuf, wbuf, zx, zw, sems, zsem):
    t = pl.program_id(0)
    n_tiles = pl.num_programs(0)
    ne = last_row_ref.shape[0]
    n_rows, t_tokens = cbuf.shape[1], xn_ref.shape[0]

    def zero_copies(row):
        row = pl.multiple_of(row, MOE_ROW_TILE)
        return (pltpu.make_async_copy(zx, xs_ref.at[pl.ds(row, MOE_ROW_TILE)], zsem),
                pltpu.make_async_copy(zw, ws_ref.at[pl.ds(row, MOE_ROW_TILE)], zsem))

    @pl.when(t == 0)
    def _():
        zx[...] = jnp.zeros_like(zx)
        zw[...] = jnp.zeros_like(zw)
        for act in (lambda cp: cp.start(), lambda cp: cp.wait()):
            for ex in range(ne):
                @pl.when(has_rows_ref[ex] != 0)
                def _(ex=ex):
                    for cp in zero_copies(last_row_ref[ex]):
                        act(cp)

        def tail_body(i, carry):
            for cp in zero_copies(i * MOE_ROW_TILE):
                cp.start()
                cp.wait()
            return carry

        lax.fori_loop(n_used_ref[0], xs_ref.shape[0] // MOE_ROW_TILE, tail_body, 0)

    r_iota = lax.broadcasted_iota(jnp.int32, (n_rows, t_tokens), 0)
    hit1 = r_iota == posr_ref[0:1, :]
    hit2 = r_iota == posr_ref[1:2, :]
    onehot = jnp.where(hit1, 1.0, jnp.where(hit2, 1.0, 0.0)).astype(BF16)
    slot = t % 2
    cbuf[slot] = jnp.dot(onehot, xn_ref[...], preferred_element_type=F32).astype(BF16)
    wsel = jnp.where(hit1, wrow_ref[0:1, :], jnp.where(hit2, wrow_ref[1:2, :], 0.0))
    wbuf[slot] = jnp.broadcast_to(jnp.sum(wsel, axis=1, keepdims=True), wbuf.shape[1:])

    def drain(tile, act):
        s = tile % 2

        def make_copy(off, row, size):
            return (pltpu.make_async_copy(cbuf.at[s, pl.ds(off, size)], xs_ref.at[pl.ds(row, size)], sems.at[s]),
                    pltpu.make_async_copy(wbuf.at[s, pl.ds(off, size)], ws_ref.at[pl.ds(row, size)], sems.at[s]))

        _moe_chunk_copies(tile, ne, seg_off_ref, seg_len_ref, dst_ref, make_copy, act)

    drain(t, lambda cp: cp.start())

    @pl.when(t > 0)
    def _():
        drain(t - 1, lambda cp: cp.wait())

    @pl.when(t == n_tiles - 1)
    def _():
        drain(t, lambda cp: cp.wait())


def _dispatch(xn, posr, wrow, tables, n_sorted_rows, ne):
    n, d = xn.shape
    tm = MOE_TOKEN_TILE
    n_rows = _moe_compact_rows(ne)
    grid_spec = pltpu.PrefetchScalarGridSpec(
        num_scalar_prefetch=6,
        grid=(n // tm,),
        in_specs=[pl.BlockSpec((tm, d), lambda t, *_: (t, 0)),
                  pl.BlockSpec((MOE_TOP_K, tm), lambda t, *_: (0, t)),
                  pl.BlockSpec((MOE_TOP_K, tm), lambda t, *_: (0, t))],
        out_specs=[pl.BlockSpec(memory_space=pl.ANY), pl.BlockSpec(memory_space=pl.ANY)],
        scratch_shapes=[pltpu.VMEM((2, n_rows, d), BF16), pltpu.VMEM((2, n_rows, 128), F32),
                        pltpu.VMEM((MOE_ROW_TILE, d), BF16), pltpu.VMEM((MOE_ROW_TILE, 128), F32),
                        pltpu.SemaphoreType.DMA((2,)), pltpu.SemaphoreType.DMA],
    )
    return pl.pallas_call(
        _dispatch_body,
        out_shape=[jax.ShapeDtypeStruct((n_sorted_rows, d), BF16), jax.ShapeDtypeStruct((n_sorted_rows, 128), F32)],
        grid_spec=grid_spec,
        compiler_params=_cparams("arbitrary"),
        name="moe_dispatch",
    )(tables['seg_off'], tables['seg_len'], tables['dst'], tables['last_tile_row'], tables['has_rows'], tables['n_used'],
      xn, posr, wrow)


def _experts_body(tile_expert_ref, n_used_ref, xs_ref, ws_ref, wgu_ref, wd_ref, ys_ref, wgu16, wd16):
    i = pl.program_id(0)
    used = i < n_used_ref[0]
    prev = tile_expert_ref[jnp.maximum(i - 1, 0)]
    new_expert = jnp.logical_or(i == 0, tile_expert_ref[i] != prev)
    f = wd_ref.shape[0]
    fc = 512

    @pl.when(jnp.logical_and(used, new_expert))
    def _():
        for c in range(2 * f // fc):
            wgu16[:, c * fc:(c + 1) * fc] = wgu_ref[:, c * fc:(c + 1) * fc].astype(BF16)
        for c in range(f // fc):
            wd16[c * fc:(c + 1) * fc, :] = wd_ref[c * fc:(c + 1) * fc, :].astype(BF16)

    @pl.when(used)
    def _():
        x = xs_ref[...]
        w = ws_ref[...]
        wrep = jnp.concatenate([w] * (fc // w.shape[1]), axis=1)
        acc = None
        for c in range(f // fc):
            gate = jnp.dot(x, wgu16[:, c * fc:(c + 1) * fc], preferred_element_type=F32)
            up = jnp.dot(x, wgu16[:, f + c * fc:f + (c + 1) * fc], preferred_element_type=F32)
            act = (gate * _sigmoid(gate) * up * wrep).astype(BF16)
            part = jnp.dot(act, wd16[c * fc:(c + 1) * fc, :], preferred_element_type=F32)
            acc = part if acc is None else acc + part
        ys_ref[...] = acc.astype(ys_ref.dtype)

    @pl.when(jnp.logical_not(used))
    def _():
        ys_ref[...] = jnp.zeros_like(ys_ref)


def _experts(xs, ws, tables, w_gu_all, w_down_all, widx):
    rows, d = xs.shape
    f = w_down_all.shape[-2]
    tile = lambda w: pl.BlockSpec((MOE_ROW_TILE, w), lambda i, te, nu: (jnp.maximum(jnp.minimum(i, nu[0] - 1), 0), 0))
    grid_spec = pltpu.PrefetchScalarGridSpec(
        num_scalar_prefetch=2,
        grid=(rows // MOE_ROW_TILE,),
        in_specs=[tile(d), tile(ws.shape[1]),
                  pl.BlockSpec((None, None, d, 2 * f), lambda i, te, nu: (widx, te[i], 0, 0)),
                  pl.BlockSpec((None, None, f, d), lambda i, te, nu: (widx, te[i], 0, 0))],
        out_specs=pl.BlockSpec((MOE_ROW_TILE, d), lambda i, te, nu: (i, 0)),
        scratch_shapes=[pltpu.VMEM((d, 2 * f), BF16), pltpu.VMEM((f, d), BF16)],
    )
    return pl.pallas_call(
        _experts_body,
        out_shape=jax.ShapeDtypeStruct((rows, d), BF16),
        grid_spec=grid_spec,
        compiler_params=_cparams("arbitrary"),
        name="moe_experts",
    )(tables['tile_expert'], tables['n_used'], xs, ws, w_gu_all, w_down_all)


def _combine_body(seg_off_ref, seg_len_ref, dst_ref, ys_ref, posc_ref, h_ref, gain_ref, *rest, ne, final):
    *out_refs, ybuf, sems = rest
    t = pl.program_id(0)
    n_tiles = pl.num_programs(0)
    t_tokens, n_rows = h_ref.shape[0], ybuf.shape[1]

    def fetch(tile, act):
        slot = tile % 2

        def make_copy(off, row, size):
            return (pltpu.make_async_copy(ys_ref.at[pl.ds(row, size)], ybuf.at[slot, pl.ds(off, size)], sems.at[slot]),)

        _moe_chunk_copies(tile, ne, seg_off_ref, seg_len_ref, dst_ref, make_copy, act)

    @pl.when(t == 0)
    def _():
        ybuf[...] = jnp.zeros_like(ybuf)
        fetch(t, lambda cp: cp.start())

    @pl.when(t + 1 < n_tiles)
    def _():
        fetch(t + 1, lambda cp: cp.start())

    fetch(t, lambda cp: cp.wait())

    pos = posc_ref[...]
    lane_r = lax.broadcasted_iota(jnp.int32, (t_tokens, n_rows), 1)
    onehot = jnp.where(lane_r == pos[:, 0:1], 1.0, jnp.where(lane_r == pos[:, 1:2], 1.0, 0.0)).astype(BF16)
    out = h_ref[...] + jnp.dot(onehot, ybuf[t % 2], preferred_element_type=F32)
    normed = _rms_norm_f32(out, gain_ref[...])
    if final:
        out_refs[0][...] = normed
    else:
        out_refs[0][...] = out
        out_refs[1][...] = normed.astype(out_refs[1].dtype)


def _combine(ys, posc, h, tables, ne, gain, final):
    n, d = h.shape
    tm = MOE_TOKEN_TILE
    tile = pl.BlockSpec((tm, d), lambda t, *_: (t, 0))
    grid_spec = pltpu.PrefetchScalarGridSpec(
        num_scalar_prefetch=3,
        grid=(n // tm,),
        in_specs=[pl.BlockSpec(memory_space=pl.ANY),
                  pl.BlockSpec((tm, posc.shape[1]), lambda t, *_: (t, 0)),
                  tile,
                  pl.BlockSpec((1, d), lambda t, *_: (0, 0))],
        out_specs=[tile] if final else [tile, tile],
        scratch_shapes=[pltpu.VMEM((2, _moe_compact_rows(ne), d), BF16), pltpu.SemaphoreType.DMA((2,))],
    )
    res = jax.ShapeDtypeStruct((n, d), F32)
    return pl.pallas_call(
        functools.partial(_combine_body, ne=ne, final=final),
        out_shape=[res] if final else [res, jax.ShapeDtypeStruct((n, d), BF16)],
        grid_spec=grid_spec,
        compiler_params=_cparams("arbitrary"),
        name="moe_combine",
    )(tables['seg_off'], tables['seg_len'], tables['dst'], ys, posc, h, gain.reshape(1, d))


def _moe(h, g_all, layer, wr_all, w_gu_all, w_down_all, widx, gain, final):
    n = h.shape[0]
    ne = wr_all.shape[-1]
    xn, posr, wrow, posc, cnt = _router(h, g_all, layer, wr_all, widx)
    n_sorted_rows = _moe_sorted_rows(n, ne)
    tables = _moe_tables(cnt, n_sorted_rows)
    xs, ws = _dispatch(xn, posr, wrow, tables, n_sorted_rows, ne)
    ys = _experts(xs, ws, tables, w_gu_all, w_down_all, widx)
    return _combine(ys, posc, h, tables, ne, gain, final)


def kernel(x, norm_mix_g, w_in, b_in, na_rpb, conv_w, conv_b, lru_wa, lru_ba, lru_wx, lru_bx, lru_lambda, w_branch, w_out, norm_ffn_g, ffn_w_gu, ffn_w_down, router_w, moe_w_gu, moe_w_down, final_g):
    batch, seq, d = x.shape
    depth = w_in.shape[0]
    n = batch * seq
    bw = w_branch.shape[2]
    h = x.reshape(n, d)

    tables = _na_bias_tables(na_rpb)
    twiddles = _fourier_twiddles(seq)
    w_gate = _lru_gate_weights(lru_wa, lru_wx)
    tn = 6 * bw

    xn = _rms_norm(h, norm_mix_g[0], BF16)
    for l in range(depth):
        last = l == depth - 1
        q, k, v, u_f, u_x, u_g = _proj(xn, w_in, b_in, l, 0, tn, 6)
        y_a = _neighbourhood_attention(q, k, v, tables, l, batch)
        y_b = _fourier_mix(u_f, batch, twiddles)
        h_c = _recurrent_branch(u_x, conv_w, conv_b, w_gate, lru_ba, lru_bx, lru_lambda, l, batch)
        h = _merge(y_a, y_b, h_c, u_g, xn, h, w_in, b_in, w_branch, w_out, l)
        next_gain = final_g if last else norm_mix_g[l + 1]
        if l % 2 == 0:
            h, xn = _ffn(h, norm_ffn_g[l], next_gain, ffn_w_gu, ffn_w_down, l // 2)
            if last:
                h = _rms_norm(h, final_g, F32)
        elif last:
            (h,) = _moe(h, norm_ffn_g, l, router_w, moe_w_gu, moe_w_down, l // 2, next_gain, final=True)
        else:
            h, xn = _moe(h, norm_ffn_g, l, router_w, moe_w_gu, moe_w_down, l // 2, next_gain, final=False)
    return h.reshape(batch, seq, d)
```

```python
import functools

import numpy as np
import jax
import jax.numpy as jnp
from jax import lax
from jax.experimental import pallas as pl
from jax.experimental.pallas import tpu as pltpu

F32 = jnp.float32
BF16 = jnp.bfloat16

RMS_EPS = 1e-6
GRID_W = 64
NA_HEADS = 8
NA_HEAD_DIM = 64
NA_KH = 8
NA_KW = 16
NA_ROWS_PER_BLOCK = 8
NA_ROWS_IN_FLIGHT = 4
LRU_C = 8.0
CONV_W = 4
CONV_PAD_LEFT = 2
MASK_VALUE = -1e30

VMEM_LIMIT_BYTES = 56 * 1024 * 1024


def _cparams(*sem):
    return pltpu.CompilerParams(dimension_semantics=sem, vmem_limit_bytes=VMEM_LIMIT_BYTES)


def _rms_norm_f32(x, g):
    ms = jnp.mean(x * x, axis=-1, keepdims=True)
    return x * lax.rsqrt(ms + RMS_EPS) * g


def _sigmoid(x):
    return 1.0 / (1.0 + jnp.exp(-x))


def _rms_norm_body(h_ref, g_ref, o_ref):
    o_ref[...] = _rms_norm_f32(h_ref[...], g_ref[...]).astype(o_ref.dtype)


def _rms_norm(h, g, out_dtype, tm=1024):
    n, d = h.shape
    return pl.pallas_call(
        _rms_norm_body,
        out_shape=jax.ShapeDtypeStruct((n, d), out_dtype),
        grid=(n // tm,),
        in_specs=[pl.BlockSpec((tm, d), lambda t: (t, 0)), pl.BlockSpec((1, d), lambda t: (0, 0))],
        out_specs=pl.BlockSpec((tm, d), lambda t: (t, 0)),
        compiler_params=_cparams("arbitrary"),
        name="rms_norm",
    )(h, g.reshape(1, d))


def _proj_body(x_ref, w_ref, b_ref, *out_refs):
    r = jnp.dot(x_ref[...], w_ref[...].astype(BF16), preferred_element_type=F32) + b_ref[...]
    width = r.shape[1] // len(out_refs)
    for i, o in enumerate(out_refs):
        o[...] = r[:, i * width:(i + 1) * width].astype(o.dtype)


def _proj(xn, w_all, b_all, layer, col_block, tn, n_out, tm=1024):
    n, d = xn.shape
    depth = w_all.shape[0]
    b3 = b_all.reshape(depth, 1, -1)
    width = tn // n_out
    return pl.pallas_call(
        _proj_body,
        out_shape=[jax.ShapeDtypeStruct((n, width), BF16) for _ in range(n_out)],
        grid=(n // tm,),
        in_specs=[
            pl.BlockSpec((tm, d), lambda t: (t, 0)),
            pl.BlockSpec((None, d, tn), lambda t: (layer, 0, col_block)),
            pl.BlockSpec((None, 1, tn), lambda t: (layer, 0, col_block)),
        ],
        out_specs=[pl.BlockSpec((tm, width), lambda t: (t, 0)) for _ in range(n_out)],
        compiler_params=_cparams("arbitrary"),
        name="in_proj",
    )(xn, w_all, b3)


def _na_bias_tables(rpb_all):
    cols = np.arange(GRID_W)
    col_start = np.clip(cols - NA_KW // 2, 0, GRID_W - NA_KW)
    cc = np.arange(GRID_W)[None, :]
    in_win = (cc >= col_start[:, None]) & (cc < col_start[:, None] + NA_KW)
    col_off = cc - cols[:, None] + (NA_KW - 1)
    onehot = np.zeros((2 * NA_KW - 1, GRID_W, GRID_W), np.float32)
    cq, ck = np.nonzero(in_win)
    onehot[col_off[cq, ck], cq, ck] = 1.0
    t = jnp.einsum('lhro,ocd->lhrcd', rpb_all.astype(F32), jnp.asarray(onehot),
                   precision=lax.Precision.HIGHEST)
    t = jnp.where(jnp.asarray(in_win)[None, None, None], t, MASK_VALUE)
    depth, heads, n_off = t.shape[:3]
    t = t.reshape(depth, heads // 2, 2, n_off, GRID_W, GRID_W)
    t = jnp.transpose(t, (0, 1, 3, 2, 4, 5)).reshape(depth, heads // 2, n_off, 2 * GRID_W, GRID_W)
    return jnp.concatenate([t[:, :, :-1], t[:, :, 1:]], axis=-1)


def _na_body(q_ref, kbuf, vbuf, tbl_ref, o_ref, s_scr, e_scr, *, n_blocks):
    j = pl.program_id(1)
    band = NA_KH * GRID_W
    half = NA_KH // 2
    lane = lax.broadcasted_iota(jnp.int32, (GRID_W, 2 * NA_HEAD_DIM), 1)
    lo = lane < NA_HEAD_DIM
    qscale = NA_HEAD_DIM ** -0.5
    head_mask = (jnp.where(lo, qscale, 0.0).astype(BF16), jnp.where(lo, 0.0, qscale).astype(BF16))

    n_pairs = NA_HEADS // 2
    cols = [slice(p * 2 * NA_HEAD_DIM, (p + 1) * 2 * NA_HEAD_DIM) for p in range(n_pairs)]

    def row_offsets(rl):
        start_first = jnp.maximum(rl - half, 0)
        start_last = jnp.minimum(rl + half, NA_ROWS_PER_BLOCK)
        start = jnp.where(j == 0, start_first, jnp.where(j == n_blocks - 1, start_last, rl))
        delta = jnp.where(j == 0, jnp.minimum(rl, half), jnp.where(j == n_blocks - 1, jnp.maximum(rl, half), half))
        return pl.multiple_of(rl * GRID_W, GRID_W), pl.multiple_of(start * GRID_W, GRID_W), delta

    def scores(rl, slot):
        qrow, krow, delta = row_offsets(rl)
        for p in range(n_pairs):
            q2 = q_ref[pl.ds(qrow, GRID_W), cols[p]]
            qs = jnp.concatenate([q2 * head_mask[0], q2 * head_mask[1]], axis=0)
            k2 = kbuf[pl.ds(krow, band), cols[p]]
            s = lax.dot_general(qs, k2, (((1,), (1,)), ((), ())), preferred_element_type=F32)
            bias = jnp.concatenate([tbl_ref[p, 2 * m - delta + (NA_KH - 1)] for m in range(NA_KH // 2)], axis=1)
            s_scr[slot, p] = s + bias

    def softmax(slot):
        inv_l = []
        for p in range(n_pairs):
            s = s_scr[slot, p]
            m = jnp.max(s, axis=-1, keepdims=True)
            e = jnp.exp(s - m)
            inv_l.append(1.0 / jnp.sum(e, axis=-1, keepdims=True))
            e_scr[slot, p] = e.astype(BF16)
        return inv_l

    def weighted_values(rl, slot, inv_l):
        qrow, krow, _ = row_offsets(rl)
        for p in range(n_pairs):
            v2 = vbuf[pl.ds(krow, band), cols[p]]
            o = jnp.dot(e_scr[slot, p], v2, preferred_element_type=F32) * inv_l[p]
            o_ref[pl.ds(qrow, GRID_W), cols[p]] = jnp.where(lo, o[:GRID_W], o[GRID_W:]).astype(o_ref.dtype)

    def rows_body(it, carry):
        rows = [it * NA_ROWS_IN_FLIGHT + r for r in range(NA_ROWS_IN_FLIGHT)]
        for slot, rl in enumerate(rows):
            scores(rl, slot)
        inv = [softmax(slot) for slot in range(NA_ROWS_IN_FLIGHT)]
        for slot, rl in enumerate(rows):
            weighted_values(rl, slot, inv[slot])
        return carry

    lax.fori_loop(0, NA_ROWS_PER_BLOCK // NA_ROWS_IN_FLIGHT, rows_body, 0)


def _neighbourhood_attention(q, k, v, tables, layer, batch):
    n, width = q.shape
    n_pairs = NA_HEADS // 2
    blk = NA_ROWS_PER_BLOCK * GRID_W
    n_blocks = n // batch // blk
    assert n_blocks >= 2 and NA_KH == NA_ROWS_PER_BLOCK
    s = n // batch
    half_blk = blk // 2

    def window_start(b, j):
        start = b * s + jnp.clip(j * blk - half_blk, 0, s - 2 * blk)
        return pl.multiple_of(start, half_blk), 0

    tile = pl.BlockSpec((blk, width), lambda b, j: (b * n_blocks + j, 0))
    window = pl.BlockSpec((pl.Element(2 * blk), pl.Element(width)), window_start)
    return pl.pallas_call(
        functools.partial(_na_body, n_blocks=n_blocks),
        out_shape=jax.ShapeDtypeStruct((n, width), BF16),
        grid=(batch, n_blocks),
        in_specs=[tile, window, window,
                  pl.BlockSpec((None,) + tables.shape[1:], lambda b, j: (layer, 0, 0, 0, 0))],
        out_specs=tile,
        scratch_shapes=[pltpu.VMEM((NA_ROWS_IN_FLIGHT, n_pairs, 2 * GRID_W, NA_KH * GRID_W), F32),
                        pltpu.VMEM((NA_ROWS_IN_FLIGHT, n_pairs, 2 * GRID_W, NA_KH * GRID_W), BF16)],
        compiler_params=_cparams("arbitrary", "arbitrary"),
        name="neigh_attn",
    )(q, k, v, tables)


FN_N2 = 128
FN_GROUP_DIM = 64
FN_GROUP = 16


def _dft_cos_sin(n):
    ang = 2.0 * np.pi * (np.outer(np.arange(n), np.arange(n)) % n) / n
    return np.cos(ang), np.sin(ang)


def _fourier_stage1_body(x_ref, f_ref, tc_ref, ts_ref, zr_ref, zi_ref):
    n1, n_b, width = x_ref.shape
    xs = jnp.swapaxes(x_ref[...], 0, 1)
    rep = width // tc_ref.shape[2]
    zr_all, zi_all = [], []
    for b in range(n_b):
        z = jnp.dot(f_ref[...], xs[b], preferred_element_type=F32)
        zr, zi = z[:n1], z[n1:]
        tc = jnp.concatenate([tc_ref[b]] * rep, axis=1)
        ts = jnp.concatenate([ts_ref[b]] * rep, axis=1)
        zr_all.append((zr * tc + zi * ts).astype(zr_ref.dtype))
        zi_all.append((zi * tc - zr * ts).astype(zi_ref.dtype))
    zr_ref[...] = jnp.swapaxes(jnp.stack(zr_all, axis=0), 0, 1)
    zi_ref[...] = jnp.swapaxes(jnp.stack(zi_all, axis=0), 0, 1)


def _fourier_stage2_body(zr_ref, zi_ref, f_ref, c_ref, o_ref, *, scale):
    n2 = zr_ref.shape[1]
    outs = []
    for i in range(zr_ref.shape[0]):
        z = jnp.concatenate([zr_ref[i], zi_ref[i]], axis=0)
        y = jnp.dot(f_ref[...], z, preferred_element_type=F32).astype(BF16)
        lanes = c_ref.shape[1]
        cols = [slice(cb * lanes, (cb + 1) * lanes) for cb in range(y.shape[1] // lanes)]
        out = jnp.concatenate(
            [jnp.dot(jnp.concatenate([y[:n2, cs], y[n2:, cs]], axis=1), c_ref[...], preferred_element_type=F32)
             for cs in cols], axis=1) * scale
        outs.append(out.astype(o_ref.dtype))
    o_ref[...] = jnp.swapaxes(jnp.stack(outs, axis=0), 0, 1)


def _fourier_mix(u, batch, twiddles):
    n, width = u.shape
    s = n // batch
    n2 = FN_N2
    n1 = s // n2
    tc, ts = twiddles
    c1, s1 = _dft_cos_sin(n1)
    f1 = jnp.asarray(np.concatenate([c1, -s1], axis=0), F32).astype(BF16)
    c2, s2 = _dft_cos_sin(n2)
    f2 = jnp.asarray(np.block([[c2, s2], [-s2, c2]]), F32).astype(BF16)
    cg, sg = _dft_cos_sin(FN_GROUP_DIM)
    eye = np.eye(128 // FN_GROUP_DIM)
    fc = jnp.asarray(np.concatenate([np.kron(eye, cg), np.kron(eye, sg)], axis=0), F32).astype(BF16)

    grp = FN_GROUP
    pos_blk = pl.BlockSpec((None, n1, grp, width), lambda b, j: (b, 0, j, 0))
    tw_blk = pl.BlockSpec((grp,) + tc.shape[1:], lambda b, j: (j, 0, 0))
    zr, zi = pl.pallas_call(
        _fourier_stage1_body,
        out_shape=[jax.ShapeDtypeStruct((batch, n1, n2, width), BF16)] * 2,
        grid=(batch, n2 // grp),
        in_specs=[pos_blk, pl.BlockSpec((2 * n1, n1), lambda b, j: (0, 0)), tw_blk, tw_blk],
        out_specs=[pos_blk, pos_blk],
        compiler_params=_cparams("arbitrary", "arbitrary"),
        name="fourier_stage1",
    )(u.reshape(batch, n1, n2, width), f1, tc, ts)

    freq_blk = pl.BlockSpec((None, grp, n2, width), lambda b, j: (b, j, 0, 0))
    out = pl.pallas_call(
        functools.partial(_fourier_stage2_body, scale=float(1.0 / np.sqrt(s * FN_GROUP_DIM))),
        out_shape=jax.ShapeDtypeStruct((batch, n2, n1, width), BF16),
        grid=(batch, n1 // grp),
        in_specs=[freq_blk, freq_blk,
                  pl.BlockSpec((2 * n2, 2 * n2), lambda b, j: (0, 0)),
                  pl.BlockSpec(fc.shape, lambda b, j: (0, 0))],
        out_specs=pl.BlockSpec((None, n2, grp, width), lambda b, j: (b, 0, j, 0)),
        compiler_params=_cparams("arbitrary", "arbitrary"),
        name="fourier_stage2",
    )(zr, zi, f2, fc)
    return out.reshape(n, width)


def _fourier_twiddles(s):
    n2 = FN_N2
    n1 = s // n2
    ang = (2.0 * np.pi / s) * (jnp.arange(n2, dtype=F32)[:, None] * jnp.arange(n1, dtype=F32)[None, :])
    rep = lambda t: jnp.broadcast_to(t[:, :, None], (n2, n1, 128))
    return rep(jnp.cos(ang)), rep(jnp.sin(ang))


LRU_LANES = 128
LRU_SEGMENTS = 16
LRU_JCHUNK = 32
SUBLANES = 8


def _lru_gate_weights(wa_all, wx_all):
    depth, _, nb, db, _ = wa_all.shape
    ncol = nb // 2

    def blockdiag(w):
        w = w.reshape(depth, ncol, 2, db, db)
        z = jnp.zeros_like(w[:, :, 0])
        top = jnp.concatenate([w[:, :, 0], z], axis=-1)
        bot = jnp.concatenate([z, w[:, :, 1]], axis=-1)
        return jnp.concatenate([top, bot], axis=-2)

    parts = [blockdiag(wa_all[:, 0]), blockdiag(wx_all[:, 0]), blockdiag(wa_all[:, 1]), blockdiag(wx_all[:, 1])]
    return jnp.concatenate(parts, axis=-1).astype(BF16)


def _gelu_tanh(x):
    return 0.5 * x * (1.0 + jnp.tanh(np.sqrt(2.0 / np.pi) * (x + 0.044715 * (x * x * x))))


def _lru_body(useq_ref, cw_ref, cb_ref, w_ref, ba_ref, bx_ref, lam_ref, oseq_ref, ux_ref, o_ref, h_scr, p_scr):
    n_j, n_g, lanes = ux_ref.shape
    ux_ref[...] = jnp.swapaxes(useq_ref[...].reshape(n_g, n_j, lanes), 0, 1)
    jc = LRU_JCHUNK
    n_chunks = n_j // jc
    seg = lax.broadcasted_iota(jnp.int32, (n_g, lanes), 0)

    def from_prev_segment(x):
        return jnp.where(seg >= 1, pltpu.roll(x, 1, axis=0), 0.0)

    def from_next_segment(x):
        return jnp.where(seg < n_g - 1, pltpu.roll(x, n_g - 1, axis=0), 0.0)

    def conv_chunk(j0):
        main = ux_ref[pl.ds(j0, jc)].astype(F32)
        lo_in = ux_ref[pl.ds(jnp.maximum(j0 - CONV_PAD_LEFT, 0), CONV_PAD_LEFT)].astype(F32)
        tail = ux_ref[n_j - CONV_PAD_LEFT:n_j].astype(F32)
        lo_wrap = jnp.stack([from_prev_segment(tail[r]) for r in range(CONV_PAD_LEFT)], axis=0)
        lo = jnp.where(j0 > 0, lo_in, lo_wrap)
        n_hi = CONV_W - 1 - CONV_PAD_LEFT
        hi_in = ux_ref[pl.ds(jnp.minimum(j0 + jc, n_j - n_hi), n_hi)].astype(F32)
        head = ux_ref[0:n_hi].astype(F32)
        hi_wrap = jnp.stack([from_next_segment(head[r]) for r in range(n_hi)], axis=0)
        hi = jnp.where(j0 + jc < n_j, hi_in, hi_wrap)
        ext = jnp.concatenate([lo, main, hi], axis=0)
        c = ext[0:jc] * cw_ref[0:1, :] + cb_ref[...]
        for tap in range(1, CONV_W):
            c = c + ext[tap:tap + jc] * cw_ref[tap:tap + 1, :]
        return c

    def gates(c, pre, d, j0):
        r = _sigmoid(pre[:, 2 * d * lanes:(2 * d + 1) * lanes] + ba_ref[d:d + 1, :])
        i = _sigmoid(pre[:, (2 * d + 1) * lanes:(2 * d + 2) * lanes] + bx_ref[d:d + 1, :])
        lam = lam_ref[d:d + 1, :]
        softplus = jnp.maximum(-lam, 0.0) + jnp.log(1.0 + jnp.exp(-jnp.abs(lam)))
        a = jnp.exp(-LRU_C * r * softplus)
        om = 1.0 - a * a
        mult = jnp.where(om > 0.0, om * lax.rsqrt(om), 0.0)
        a = a.reshape(jc, n_g, lanes)
        b = (mult * i).reshape(jc, n_g, lanes) * c
        edge = (jc - 1) * d
        i_edge = i.reshape(jc, n_g, lanes)[edge]
        is_first = (seg == (n_g - 1) * d) & (j0 + edge == (n_j - 1) * d)
        b_edge = jnp.where(is_first, i_edge * c[edge], b[edge])[None]
        b = jnp.concatenate([b_edge, b[1:]] if d == 0 else [b[:-1], b_edge], axis=0)
        return a, b

    def local_scan(a, b, carry, d):
        h, p = carry
        hs, ps = [None] * jc, [None] * jc
        for jj in (range(jc) if d == 0 else range(jc - 1, -1, -1)):
            h = a[jj] * h + b[jj]
            p = a[jj] * p
            hs[jj], ps[jj] = h, p
        return jnp.stack(hs, axis=0), jnp.stack(ps, axis=0), (h, p)

    scan_init = (jnp.zeros((n_g, lanes), F32), jnp.ones((n_g, lanes), F32))

    def gate_and_forward_body(ci, carry):
        j0 = ci * jc
        rows = pl.ds(j0, jc)
        c = conv_chunk(j0)
        pre = jnp.dot(c.reshape(jc * n_g, lanes).astype(BF16), w_ref[...], preferred_element_type=F32)
        a_b, b_b = gates(c, pre, 1, j0)
        p_scr[1, rows] = a_b
        h_scr[1, rows] = b_b
        a_f, b_f = gates(c, pre, 0, j0)
        h_scr[0, rows], p_scr[0, rows], carry = local_scan(a_f, b_f, carry, 0)
        return carry

    lax.fori_loop(0, n_chunks, gate_and_forward_body, scan_init)

    def backward_body(ci, carry):
        rows = pl.ds((n_chunks - 1 - ci) * jc, jc)
        h_scr[1, rows], p_scr[1, rows], carry = local_scan(p_scr[1, rows], h_scr[1, rows], carry, 1)
        return carry

    lax.fori_loop(0, n_chunks, backward_body, scan_init)

    def carry_in(d):
        edge = n_j - 1 if d == 0 else 0
        h_end, p_end = h_scr[d, edge], p_scr[d, edge]
        state = jnp.zeros((1, lanes), F32)
        out = jnp.zeros((n_g, lanes), F32)
        for g in (range(n_g) if d == 0 else range(n_g - 1, -1, -1)):
            out = jnp.where(seg == g, state, out)
            state = h_end[g:g + 1] + p_end[g:g + 1] * state
        return out

    e_fwd, e_bwd = carry_in(0), carry_in(1)

    def out_body(ci, carry):
        j0 = ci * jc
        rows = pl.ds(j0, jc)
        h = h_scr[0, rows] + p_scr[0, rows] * e_fwd + h_scr[1, rows] + p_scr[1, rows] * e_bwd
        o_ref[rows] = h.astype(o_ref.dtype)
        return carry

    lax.fori_loop(0, n_chunks, out_body, 0)
    oseq_ref[...] = jnp.swapaxes(o_ref[...], 0, 1).reshape(n_g * n_j, lanes)


def _recurrent_branch(u_x, conv_w, conv_b, w_gate, ba, bx, lam, layer, batch):
    n, width = u_x.shape
    s = n // batch
    depth = conv_w.shape[0]
    ncol = width // LRU_LANES
    n_g = LRU_SEGMENTS
    n_j = s // n_g
    assert n_j % LRU_JCHUNK == 0
    cb3 = conv_b.reshape(depth, 1, width)
    seq_spec = pl.BlockSpec((None, s, LRU_LANES), lambda b, c: (b, 0, c))
    par = lambda rows: pl.BlockSpec((None, rows, LRU_LANES), lambda b, c: (layer, 0, c))
    seg_copy = pltpu.VMEM((n_j, n_g, LRU_LANES), BF16)
    state = pltpu.VMEM((2, n_j, n_g, LRU_LANES), F32)
    out = pl.pallas_call(
        _lru_body,
        out_shape=jax.ShapeDtypeStruct((batch, s, width), BF16),
        grid=(batch, ncol),
        in_specs=[seq_spec, par(CONV_W), par(1),
                  pl.BlockSpec((None, None, LRU_LANES, 4 * LRU_LANES), lambda b, c: (layer, c, 0, 0)),
                  par(2), par(2), par(2)],
        out_specs=seq_spec,
        scratch_shapes=[seg_copy, seg_copy, state, state],
        compiler_params=_cparams("arbitrary", "arbitrary"),
        name="rg_lru",
    )(u_x.reshape(batch, s, width), conv_w, cb3, w_gate, ba, bx, lam)
    return out.reshape(n, width)


def _merge_body(ya_ref, yb_ref, hc_ref, ug_ref, xn_ref, h_ref, wgate_ref, bgate_ref, wb_ref, wo_ref, o_ref,
                wgate16, wb16, wo16):
    d = h_ref.shape[1]

    @pl.when(pl.program_id(0) == 0)
    def _():
        for kbr in range(wgate16.shape[1] // d):
            wgate16[:, kbr * d:(kbr + 1) * d] = wgate_ref[:, kbr * d:(kbr + 1) * d].astype(BF16)
            wb16[kbr] = wb_ref[kbr].astype(BF16)
        wo16[...] = wo_ref[...].astype(BF16)

    xn = xn_ref[...]
    yc = (hc_ref[...].astype(F32) * _gelu_tanh(ug_ref[...].astype(F32))).astype(BF16)
    merged = None
    for kbr, y in enumerate((ya_ref[...], yb_ref[...], yc)):
        cols = slice(kbr * d, (kbr + 1) * d)
        gate = jnp.dot(xn, wgate16[:, cols], preferred_element_type=F32) + bgate_ref[:, cols]
        ybr = jnp.dot(y, wb16[kbr], preferred_element_type=F32)
        term = _sigmoid(gate) * ybr
        merged = term if merged is None else merged + term
    o_ref[...] = h_ref[...] + jnp.dot(merged.astype(BF16), wo16[...], preferred_element_type=F32)


def _merge(ya, yb, hc, ug, xn, h, w_in_all, b_in_all, wb_all, wo_all, layer, tm=512):
    n, d = h.shape
    bw = ya.shape[1]
    depth = w_in_all.shape[0]
    gate_block = w_in_all.shape[2] // (3 * d) - 1
    assert (gate_block + 1) * 3 * d == w_in_all.shape[2]
    once = pl.Buffered(1)
    ytile = pl.BlockSpec((tm, bw), lambda t: (t, 0))
    return pl.pallas_call(
        _merge_body,
        out_shape=jax.ShapeDtypeStruct((n, d), F32),
        grid=(n // tm,),
        in_specs=[ytile, ytile, ytile, ytile,
                  pl.BlockSpec((tm, d), lambda t: (t, 0)),
                  pl.BlockSpec((tm, d), lambda t: (t, 0)),
                  pl.BlockSpec((None, d, 3 * d), lambda t: (layer, 0, gate_block), pipeline_mode=once),
                  pl.BlockSpec((None, 1, 3 * d), lambda t: (layer, 0, gate_block)),
                  pl.BlockSpec((None, 3, bw, d), lambda t: (layer, 0, 0, 0), pipeline_mode=once),
                  pl.BlockSpec((None, d, d), lambda t: (layer, 0, 0), pipeline_mode=once)],
        out_specs=pl.BlockSpec((tm, d), lambda t: (t, 0)),
        scratch_shapes=[pltpu.VMEM((d, 3 * d), BF16), pltpu.VMEM((3, bw, d), BF16), pltpu.VMEM((d, d), BF16)],
        compiler_params=_cparams("arbitrary"),
        name="branch_merge",
    )(ya, yb, hc, ug, xn, h, w_in_all, b_in_all.reshape(depth, 1, -1), wb_all, wo_all)


def _ffn_body(h_ref, g_ref, gnext_ref, wg_ref, wu_ref, wd_ref, o_ref, xnext_ref, xn_ref):
    j = pl.program_id(1)

    @pl.when(j == 0)
    def _():
        h = h_ref[...]
        xn_ref[...] = _rms_norm_f32(h, g_ref[...]).astype(BF16)
        o_ref[...] = h

    xn = xn_ref[...]
    gate = jnp.dot(xn, wg_ref[...].astype(BF16), preferred_element_type=F32)
    up = jnp.dot(xn, wu_ref[...].astype(BF16), preferred_element_type=F32)
    act = gate * _sigmoid(gate) * up
    o_ref[...] += jnp.dot(act.astype(BF16), wd_ref[...].astype(BF16), preferred_element_type=F32)

    @pl.when(j == pl.num_programs(1) - 1)
    def _():
        xnext_ref[...] = _rms_norm_f32(o_ref[...], gnext_ref[...]).astype(xnext_ref.dtype)


def _ffn(h, g, g_next, w_gu_all, w_down_all, widx, tm=1024, tf=1024):
    n, d = h.shape
    nf = w_down_all.shape[-2] // tf
    gain = pl.BlockSpec((1, d), lambda t, j: (0, 0))
    tile = pl.BlockSpec((tm, d), lambda t, j: (t, 0))
    return pl.pallas_call(
        _ffn_body,
        out_shape=[jax.ShapeDtypeStruct((n, d), F32), jax.ShapeDtypeStruct((n, d), BF16)],
        grid=(n // tm, nf),
        in_specs=[tile, gain, gain,
                  pl.BlockSpec((None, d, tf), lambda t, j: (widx, 0, j)),
                  pl.BlockSpec((None, d, tf), lambda t, j: (widx, 0, j + nf)),
                  pl.BlockSpec((None, tf, d), lambda t, j: (widx, j, 0))],
        out_specs=[tile, tile],
        scratch_shapes=[pltpu.VMEM((tm, d), BF16)],
        compiler_params=_cparams("arbitrary", "arbitrary"),
        name="dense_ffn",
    )(h, g.reshape(1, d), g_next.reshape(1, d), w_gu_all, w_gu_all, w_down_all)


MOE_TOKEN_TILE = 512
MOE_ROW_TILE = 512
MOE_SEG_ALIGN = 16
MOE_TOP_K = 2
MOE_CHUNK_SIZES = tuple(MOE_SEG_ALIGN << b for b in range(5, -1, -1))
assert MOE_CHUNK_SIZES[0] == MOE_TOKEN_TILE


def _moe_compact_rows(ne):
    rows = MOE_TOKEN_TILE * MOE_TOP_K + ne * (MOE_SEG_ALIGN - 1)
    return -(-rows // MOE_SEG_ALIGN) * MOE_SEG_ALIGN


def _moe_sorted_rows(n, ne):
    rows = n * MOE_TOP_K + (n // MOE_TOKEN_TILE) * ne * (MOE_SEG_ALIGN - 1) + ne * (MOE_ROW_TILE - MOE_SEG_ALIGN)
    return -(-rows // MOE_ROW_TILE) * MOE_ROW_TILE


def _router_body(h_ref, g_ref, wrt_ref, xn_ref, posr_ref, wrow_ref, posc_ref, cnt_ref, before_scr, eye_scr):
    t_tokens = h_ref.shape[0]

    @pl.when(pl.program_id(0) == 0)
    def _():
        r_i = lax.broadcasted_iota(jnp.int32, (t_tokens, t_tokens), 0)
        c_i = lax.broadcasted_iota(jnp.int32, (t_tokens, t_tokens), 1)
        before_scr[...] = jnp.where(r_i < c_i, 1.0, 0.0).astype(BF16)
        eye_scr[...] = jnp.where(r_i == c_i, 1.0, 0.0).astype(BF16)

    xn = _rms_norm_f32(h_ref[...], g_ref[...])
    xn_hi = xn.astype(BF16)
    xn_ref[...] = xn_hi
    nt_dims = (((1,), (1,)), ((), ()))
    xn_lo = (xn - xn_hi.astype(F32)).astype(BF16)
    w = wrt_ref[...]
    ne = w.shape[0]
    w_hi = w.astype(BF16)
    w_lo = (w - w_hi.astype(F32)).astype(BF16)
    by_hi = lax.dot_general(jnp.concatenate([w_hi, w_lo], axis=0), xn_hi, nt_dims, preferred_element_type=F32)
    logits = by_hi[:ne] + by_hi[ne:] + lax.dot_general(w_hi, xn_lo, nt_dims, preferred_element_type=F32)
    sub = lax.broadcasted_iota(jnp.int32, logits.shape, 0)
    m1 = jnp.max(logits, axis=0, keepdims=True)
    i1 = jnp.min(jnp.where(logits == m1, sub, ne), axis=0, keepdims=True)
    rest = jnp.where(sub == i1, -jnp.inf, logits)
    m2 = jnp.max(rest, axis=0, keepdims=True)
    i2 = jnp.min(jnp.where(rest == m2, sub, ne), axis=0, keepdims=True)
    e = jnp.exp(m2 - m1)
    wrow_ref[...] = jnp.concatenate([1.0 / (1.0 + e), e / (1.0 + e)], axis=0)

    sel1, sel2 = sub == i1, sub == i2
    memb = jnp.where(sel1, 1.0, jnp.where(sel2, 1.0, 0.0))
    rank = jnp.dot(memb.astype(BF16), before_scr[...], preferred_element_type=F32)
    sub_c = lax.broadcasted_iota(jnp.int32, cnt_ref.shape, 0)
    cnt_out = jnp.zeros(cnt_ref.shape, F32)
    base = rank
    running = jnp.zeros((1, 1), F32)
    for ex in range(ne):
        c = jnp.sum(memb[ex:ex + 1, :], axis=1, keepdims=True)
        cnt_out = jnp.where(sub_c == ex, c, cnt_out)
        base = jnp.where(sub == ex, base + running, base)
        running = running + jnp.floor((c + (MOE_SEG_ALIGN - 1)) * (1.0 / MOE_SEG_ALIGN)) * MOE_SEG_ALIGN
    cnt_ref[...] = cnt_out.astype(jnp.int32)
    pos1 = jnp.sum(jnp.where(sel1, base, 0.0), axis=0, keepdims=True)
    pos2 = jnp.sum(jnp.where(sel2, base, 0.0), axis=0, keepdims=True)
    posr_ref[...] = jnp.concatenate([pos1, pos2], axis=0).astype(jnp.int32)
    digits = []
    for pos in (pos1, pos2):
        hi = jnp.floor(pos * (1.0 / 128.0))
        digits += [hi, pos - 128.0 * hi]
    pad = jnp.zeros((2 * SUBLANES - len(digits), t_tokens), F32)
    dig_t = lax.dot_general(eye_scr[...], jnp.concatenate(digits + [pad], axis=0).astype(BF16), nt_dims,
                            preferred_element_type=F32)
    d_i = lax.broadcasted_iota(jnp.int32, (2 * SUBLANES, posc_ref.shape[1]), 0)
    c_i = lax.broadcasted_iota(jnp.int32, (2 * SUBLANES, posc_ref.shape[1]), 1)
    recombine = jnp.where(d_i == 2 * c_i, 128.0, jnp.where(d_i == 2 * c_i + 1, 1.0, 0.0)).astype(BF16)
    posc = jnp.dot(dig_t.astype(BF16), recombine, preferred_element_type=F32)
    posc_ref[...] = posc.astype(jnp.int32)


def _router(h, g_all, layer, wr_all, widx):
    n, d = h.shape
    depth = g_all.shape[0]
    ne = wr_all.shape[-1]
    tm = MOE_TOKEN_TILE
    nt = n // tm
    wrt = jnp.swapaxes(wr_all, 1, 2)
    return pl.pallas_call(
        _router_body,
        out_shape=[jax.ShapeDtypeStruct((n, d), BF16),
                   jax.ShapeDtypeStruct((MOE_TOP_K, n), jnp.int32),
                   jax.ShapeDtypeStruct((MOE_TOP_K, n), F32),
                   jax.ShapeDtypeStruct((n, SUBLANES), jnp.int32),
                   jax.ShapeDtypeStruct((nt, ne, 128), jnp.int32)],
        grid=(nt,),
        in_specs=[pl.BlockSpec((tm, d), lambda t: (t, 0)),
                  pl.BlockSpec((None, 1, d), lambda t: (layer, 0, 0)),
                  pl.BlockSpec((None, ne, d), lambda t: (widx, 0, 0))],
        out_specs=[pl.BlockSpec((tm, d), lambda t: (t, 0)),
                   pl.BlockSpec((MOE_TOP_K, tm), lambda t: (0, t)),
                   pl.BlockSpec((MOE_TOP_K, tm), lambda t: (0, t)),
                   pl.BlockSpec((tm, SUBLANES), lambda t: (t, 0)),
                   pl.BlockSpec((None, ne, 128), lambda t: (t, 0, 0))],
        scratch_shapes=[pltpu.VMEM((tm, tm), BF16), pltpu.VMEM((tm, tm), BF16)],
        compiler_params=_cparams("arbitrary"),
        name="router",
    )(h, g_all.reshape(depth, 1, d), wrt)


def _moe_tables(cnt, n_sorted_rows):
    cnt = cnt[:, :, 0]
    nt, ne = cnt.shape
    seg = (cnt + (MOE_SEG_ALIGN - 1)) // MOE_SEG_ALIGN * MOE_SEG_ALIGN
    seg_off = jnp.cumsum(seg, axis=1) - seg
    e_rows = jnp.sum(seg, axis=0)
    e_tiles = (e_rows + (MOE_ROW_TILE - 1)) // MOE_ROW_TILE
    e_cum = jnp.cumsum(e_tiles)
    e_base = (e_cum - e_tiles) * MOE_ROW_TILE
    dst = e_base[None, :] + jnp.cumsum(seg, axis=0) - seg
    n_used = e_cum[-1]
    tile_ids = jnp.minimum(jnp.arange(n_sorted_rows // MOE_ROW_TILE, dtype=jnp.int32), n_used - 1)
    tile_expert = jnp.sum(tile_ids[:, None] >= e_cum[None, :], axis=1)
    last_tile_row = e_base + (e_tiles - 1) * MOE_ROW_TILE
    i32 = lambda a: a.astype(jnp.int32)
    return dict(seg_off=i32(seg_off.reshape(-1)), seg_len=i32(seg.reshape(-1)), dst=i32(dst.reshape(-1)),
                n_used=i32(n_used.reshape(1)), tile_expert=i32(tile_expert),
                last_tile_row=i32(last_tile_row), has_rows=i32(e_tiles > 0))


def _moe_chunk_copies(t, ne, seg_off_ref, seg_len_ref, dst_ref, make_copy, act):
    for ex in range(ne):
        idx = t * ne + ex
        off = seg_off_ref[idx]
        ln = seg_len_ref[idx]
        row = dst_ref[idx]
        for size in MOE_CHUNK_SIZES:
            take = (ln & size) != 0

            @pl.when(take)
            def _(off=off, row=row, size=size):
                for cp in make_copy(pl.multiple_of(off, MOE_SEG_ALIGN), pl.multiple_of(row, MOE_SEG_ALIGN), size):
                    act(cp)

            step = jnp.where(take, size, 0)
            off = off + step
            row = row + step


def _dispatch_body(seg_off_ref, seg_len_ref, dst_ref, last_row_ref, has_rows_ref, n_used_ref,
                   xn_ref, posr_ref, wrow_ref, xs_ref, ws_ref, cbuf, wbuf, zx, zw, sems, zsem):
    t = pl.program_id(0)
    n_tiles = pl.num_programs(0)
    ne = last_row_ref.shape[0]
    n_rows, t_tokens = cbuf.shape[1], xn_ref.shape[0]

    def zero_copies(row):
        row = pl.multiple_of(row, MOE_ROW_TILE)
        return (pltpu.make_async_copy(zx, xs_ref.at[pl.ds(row, MOE_ROW_TILE)], zsem),
                pltpu.make_async_copy(zw, ws_ref.at[pl.ds(row, MOE_ROW_TILE)], zsem))

    @pl.when(t == 0)
    def _():
        zx[...] = jnp.zeros_like(zx)
        zw[...] = jnp.zeros_like(zw)
        for act in (lambda cp: cp.start(), lambda cp: cp.wait()):
            for ex in range(ne):
                @pl.when(has_rows_ref[ex] != 0)
                def _(ex=ex):
                    for cp in zero_copies(last_row_ref[ex]):
                        act(cp)

        def tail_body(i, carry):
            for cp in zero_copies(i * MOE_ROW_TILE):
                cp.start()
                cp.wait()
            return carry

        lax.fori_loop(n_used_ref[0], xs_ref.shape[0] // MOE_ROW_TILE, tail_body, 0)

    r_iota = lax.broadcasted_iota(jnp.int32, (n_rows, t_tokens), 0)
    hit1 = r_iota == posr_ref[0:1, :]
    hit2 = r_iota == posr_ref[1:2, :]
    onehot = jnp.where(hit1, 1.0, jnp.where(hit2, 1.0, 0.0)).astype(BF16)
    slot = t % 2
    cbuf[slot] = jnp.dot(onehot, xn_ref[...], preferred_element_type=F32).astype(BF16)
    wsel = jnp.where(hit1, wrow_ref[0:1, :], jnp.where(hit2, wrow_ref[1:2, :], 0.0))
    wbuf[slot] = jnp.broadcast_to(jnp.sum(wsel, axis=1, keepdims=True), wbuf.shape[1:])

    def drain(tile, act):
        s = tile % 2

        def make_copy(off, row, size):
            return (pltpu.make_async_copy(cbuf.at[s, pl.ds(off, size)], xs_ref.at[pl.ds(row, size)], sems.at[s]),
                    pltpu.make_async_copy(wbuf.at[s, pl.ds(off, size)], ws_ref.at[pl.ds(row, size)], sems.at[s]))

        _moe_chunk_copies(tile, ne, seg_off_ref, seg_len_ref, dst_ref, make_copy, act)

    drain(t, lambda cp: cp.start())

    @pl.when(t > 0)
    def _():
        drain(t - 1, lambda cp: cp.wait())

    @pl.when(t == n_tiles - 1)
    def _():
        drain(t, lambda cp: cp.wait())


def _dispatch(xn, posr, wrow, tables, n_sorted_rows, ne):
    n, d = xn.shape
    tm = MOE_TOKEN_TILE
    n_rows = _moe_compact_rows(ne)
    grid_spec = pltpu.PrefetchScalarGridSpec(
        num_scalar_prefetch=6,
        grid=(n // tm,),
        in_specs=[pl.BlockSpec((tm, d), lambda t, *_: (t, 0)),
                  pl.BlockSpec((MOE_TOP_K, tm), lambda t, *_: (0, t)),
                  pl.BlockSpec((MOE_TOP_K, tm), lambda t, *_: (0, t))],
        out_specs=[pl.BlockSpec(memory_space=pl.ANY), pl.BlockSpec(memory_space=pl.ANY)],
        scratch_shapes=[pltpu.VMEM((2, n_rows, d), BF16), pltpu.VMEM((2, n_rows, 128), F32),
                        pltpu.VMEM((MOE_ROW_TILE, d), BF16), pltpu.VMEM((MOE_ROW_TILE, 128), F32),
                        pltpu.SemaphoreType.DMA((2,)), pltpu.SemaphoreType.DMA],
    )
    return pl.pallas_call(
        _dispatch_body,
        out_shape=[jax.ShapeDtypeStruct((n_sorted_rows, d), BF16), jax.ShapeDtypeStruct((n_sorted_rows, 128), F32)],
        grid_spec=grid_spec,
        compiler_params=_cparams("arbitrary"),
        name="moe_dispatch",
    )(tables['seg_off'], tables['seg_len'], tables['dst'], tables['last_tile_row'], tables['has_rows'], tables['n_used'],
      xn, posr, wrow)


def _experts_body(tile_expert_ref, n_used_ref, xs_ref, ws_ref, wgu_ref, wd_ref, ys_ref, wgu16, wd16):
    i = pl.program_id(0)
    used = i < n_used_ref[0]
    prev = tile_expert_ref[jnp.maximum(i - 1, 0)]
    new_expert = jnp.logical_or(i == 0, tile_expert_ref[i] != prev)
    f = wd_ref.shape[0]
    fc = 512

    @pl.when(jnp.logical_and(used, new_expert))
    def _():
        for c in range(2 * f // fc):
            wgu16[:, c * fc:(c + 1) * fc] = wgu_ref[:, c * fc:(c + 1) * fc].astype(BF16)
        for c in range(f // fc):
            wd16[c * fc:(c + 1) * fc, :] = wd_ref[c * fc:(c + 1) * fc, :].astype(BF16)

    @pl.when(used)
    def _():
        x = xs_ref[...]
        w = ws_ref[...]
        wrep = jnp.concatenate([w] * (fc // w.shape[1]), axis=1)
        acc = None
        for c in range(f // fc):
            gate = jnp.dot(x, wgu16[:, c * fc:(c + 1) * fc], preferred_element_type=F32)
            up = jnp.dot(x, wgu16[:, f + c * fc:f + (c + 1) * fc], preferred_element_type=F32)
            act = (gate * _sigmoid(gate) * up * wrep).astype(BF16)
            part = jnp.dot(act, wd16[c * fc:(c + 1) * fc, :], preferred_element_type=F32)
            acc = part if acc is None else acc + part
        ys_ref[...] = acc.astype(ys_ref.dtype)

    @pl.when(jnp.logical_not(used))
    def _():
        ys_ref[...] = jnp.zeros_like(ys_ref)


def _experts(xs, ws, tables, w_gu_all, w_down_all, widx):
    rows, d = xs.shape
    f = w_down_all.shape[-2]
    tile = lambda w: pl.BlockSpec((MOE_ROW_TILE, w), lambda i, te, nu: (jnp.maximum(jnp.minimum(i, nu[0] - 1), 0), 0))
    grid_spec = pltpu.PrefetchScalarGridSpec(
        num_scalar_prefetch=2,
        grid=(rows // MOE_ROW_TILE,),
        in_specs=[tile(d), tile(ws.shape[1]),
                  pl.BlockSpec((None, None, d, 2 * f), lambda i, te, nu: (widx, te[i], 0, 0)),
                  pl.BlockSpec((None, None, f, d), lambda i, te, nu: (widx, te[i], 0, 0))],
        out_specs=pl.BlockSpec((MOE_ROW_TILE, d), lambda i, te, nu: (i, 0)),
        scratch_shapes=[pltpu.VMEM((d, 2 * f), BF16), pltpu.VMEM((f, d), BF16)],
    )
    return pl.pallas_call(
        _experts_body,
        out_shape=jax.ShapeDtypeStruct((rows, d), BF16),
        grid_spec=grid_spec,
        compiler_params=_cparams("arbitrary"),
        name="moe_experts",
    )(tables['tile_expert'], tables['n_used'], xs, ws, w_gu_all, w_down_all)


def _combine_body(seg_off_ref, seg_len_ref, dst_ref, ys_ref, posc_ref, h_ref, gain_ref, *rest, ne, final):
    *out_refs, ybuf, sems = rest
    t = pl.program_id(0)
    n_tiles = pl.num_programs(0)
    t_tokens, n_rows = h_ref.shape[0], ybuf.shape[1]

    def fetch(tile, act):
        slot = tile % 2

        def make_copy(off, row, size):
            return (pltpu.make_async_copy(ys_ref.at[pl.ds(row, size)], ybuf.at[slot, pl.ds(off, size)], sems.at[slot]),)

        _moe_chunk_copies(tile, ne, seg_off_ref, seg_len_ref, dst_ref, make_copy, act)

    @pl.when(t == 0)
    def _():
        ybuf[...] = jnp.zeros_like(ybuf)
        fetch(t, lambda cp: cp.start())

    @pl.when(t + 1 < n_tiles)
    def _():
        fetch(t + 1, lambda cp: cp.start())

    fetch(t, lambda cp: cp.wait())

    pos = posc_ref[...]
    lane_r = lax.broadcasted_iota(jnp.int32, (t_tokens, n_rows), 1)
    onehot = jnp.where(lane_r == pos[:, 0:1], 1.0, jnp.where(lane_r == pos[:, 1:2], 1.0, 0.0)).astype(BF16)
    out = h_ref[...] + jnp.dot(onehot, ybuf[t % 2], preferred_element_type=F32)
    normed = _rms_norm_f32(out, gain_ref[...])
    if final:
        out_refs[0][...] = normed
    else:
        out_refs[0][...] = out
        out_refs[1][...] = normed.astype(out_refs[1].dtype)


def _combine(ys, posc, h, tables, ne, gain, final):
    n, d = h.shape
    tm = MOE_TOKEN_TILE
    tile = pl.BlockSpec((tm, d), lambda t, *_: (t, 0))
    grid_spec = pltpu.PrefetchScalarGridSpec(
        num_scalar_prefetch=3,
        grid=(n // tm,),
        in_specs=[pl.BlockSpec(memory_space=pl.ANY),
                  pl.BlockSpec((tm, posc.shape[1]), lambda t, *_: (t, 0)),
                  tile,
                  pl.BlockSpec((1, d), lambda t, *_: (0, 0))],
        out_specs=[tile] if final else [tile, tile],
        scratch_shapes=[pltpu.VMEM((2, _moe_compact_rows(ne), d), BF16), pltpu.SemaphoreType.DMA((2,))],
    )
    res = jax.ShapeDtypeStruct((n, d), F32)
    return pl.pallas_call(
        functools.partial(_combine_body, ne=ne, final=final),
        out_shape=[res] if final else [res, jax.ShapeDtypeStruct((n, d), BF16)],
        grid_spec=grid_spec,
        compiler_params=_cparams("arbitrary"),
        name="moe_combine",
    )(tables['seg_off'], tables['seg_len'], tables['dst'], ys, posc, h, gain.reshape(1, d))


def _moe(h, g_all, layer, wr_all, w_gu_all, w_down_all, widx, gain, final):
    n = h.shape[0]
    ne = wr_all.shape[-1]
    xn, posr, wrow, posc, cnt = _router(h, g_all, layer, wr_all, widx)
    n_sorted_rows = _moe_sorted_rows(n, ne)
    tables = _moe_tables(cnt, n_sorted_rows)
    xs, ws = _dispatch(xn, posr, wrow, tables, n_sorted_rows, ne)
    ys = _experts(xs, ws, tables, w_gu_all, w_down_all, widx)
    return _combine(ys, posc, h, tables, ne, gain, final)


def kernel(x, norm_mix_g, w_in, b_in, na_rpb, conv_w, conv_b, lru_wa, lru_ba, lru_wx, lru_bx, lru_lambda, w_branch, w_out, norm_ffn_g, ffn_w_gu, ffn_w_down, router_w, moe_w_gu, moe_w_down, final_g):
    batch, seq, d = x.shape
    depth = w_in.shape[0]
    n = batch * seq
    bw = w_branch.shape[2]
    h = x.reshape(n, d)

    tables = _na_bias_tables(na_rpb)
    twiddles = _fourier_twiddles(seq)
    w_gate = _lru_gate_weights(lru_wa, lru_wx)
    tn = 6 * bw

    xn = _rms_norm(h, norm_mix_g[0], BF16)
    for l in range(depth):
        last = l == depth - 1
        q, k, v, u_f, u_x, u_g = _proj(xn, w_in, b_in, l, 0, tn, 6)
        y_a = _neighbourhood_attention(q, k, v, tables, l, batch)
        y_b = _fourier_mix(u_f, batch, twiddles)
        h_c = _recurrent_branch(u_x, conv_w, conv_b, w_gate, lru_ba, lru_bx, lru_lambda, l, batch)
        h = _merge(y_a, y_b, h_c, u_g, xn, h, w_in, b_in, w_branch, w_out, l)
        next_gain = final_g if last else norm_mix_g[l + 1]
        if l % 2 == 0:
            h, xn = _ffn(h, norm_ffn_g[l], next_gain, ffn_w_gu, ffn_w_down, l // 2)
            if last:
                h = _rms_norm(h, final_g, F32)
        elif last:
            (h,) = _moe(h, norm_ffn_g, l, router_w, moe_w_gu, moe_w_down, l // 2, next_gain, final=True)
        else:
            h, xn = _moe(h, norm_ffn_g, l, router_w, moe_w_gu, moe_w_down, l // 2, next_gain, final=False)
    return h.reshape(batch, seq, d)
```

```python
import functools

import numpy as np
import jax
import jax.numpy as jnp
from jax import lax
from jax.experimental import pallas as pl
from jax.experimental.pallas import tpu as pltpu

F32 = jnp.float32
BF16 = jnp.bfloat16

RMS_EPS = 1e-6
GRID_W = 64
NA_HEADS = 8
NA_HEAD_DIM = 64
NA_KH = 8
NA_KW = 16
NA_ROWS_PER_BLOCK = 8
NA_ROWS_IN_FLIGHT = 4
LRU_C = 8.0
CONV_W = 4
CONV_PAD_LEFT = 2
MASK_VALUE = -1e30

VMEM_LIMIT_BYTES = 56 * 1024 * 1024


def _cparams(*sem):
    return pltpu.CompilerParams(dimension_semantics=sem, vmem_limit_bytes=VMEM_LIMIT_BYTES)


def _rms_norm_f32(x, g):
    ms = jnp.mean(x * x, axis=-1, keepdims=True)
    return x * lax.rsqrt(ms + RMS_EPS) * g


def _sigmoid(x):
    return 1.0 / (1.0 + jnp.exp(-x))


def _rms_norm_body(h_ref, g_ref, o_ref):
    o_ref[...] = _rms_norm_f32(h_ref[...], g_ref[...]).astype(o_ref.dtype)


def _rms_norm(h, g, out_dtype, tm=1024):
    n, d = h.shape
    return pl.pallas_call(
        _rms_norm_body,
        out_shape=jax.ShapeDtypeStruct((n, d), out_dtype),
        grid=(n // tm,),
        in_specs=[pl.BlockSpec((tm, d), lambda t: (t, 0)), pl.BlockSpec((1, d), lambda t: (0, 0))],
        out_specs=pl.BlockSpec((tm, d), lambda t: (t, 0)),
        compiler_params=_cparams("arbitrary"),
        name="rms_norm",
    )(h, g.reshape(1, d))


def _proj_body(x_ref, w_ref, b_ref, *out_refs):
    r = jnp.dot(x_ref[...], w_ref[...].astype(BF16), preferred_element_type=F32) + b_ref[...]
    width = r.shape[1] // len(out_refs)
    for i, o in enumerate(out_refs):
        o[...] = r[:, i * width:(i + 1) * width].astype(o.dtype)


def _proj(xn, w_all, b_all, layer, col_block, tn, n_out, tm=1024):
    n, d = xn.shape
    depth = w_all.shape[0]
    b3 = b_all.reshape(depth, 1, -1)
    width = tn // n_out
    return pl.pallas_call(
        _proj_body,
        out_shape=[jax.ShapeDtypeStruct((n, width), BF16) for _ in range(n_out)],
        grid=(n // tm,),
        in_specs=[
            pl.BlockSpec((tm, d), lambda t: (t, 0)),
            pl.BlockSpec((None, d, tn), lambda t: (layer, 0, col_block)),
            pl.BlockSpec((None, 1, tn), lambda t: (layer, 0, col_block)),
        ],
        out_specs=[pl.BlockSpec((tm, width), lambda t: (t, 0)) for _ in range(n_out)],
        compiler_params=_cparams("arbitrary"),
        name="in_proj",
    )(xn, w_all, b3)


def _na_bias_tables(rpb_all):
    cols = np.arange(GRID_W)
    col_start = np.clip(cols - NA_KW // 2, 0, GRID_W - NA_KW)
    cc = np.arange(GRID_W)[None, :]
    in_win = (cc >= col_start[:, None]) & (cc < col_start[:, None] + NA_KW)
    col_off = cc - cols[:, None] + (NA_KW - 1)
    onehot = np.zeros((2 * NA_KW - 1, GRID_W, GRID_W), np.float32)
    cq, ck = np.nonzero(in_win)
    onehot[col_off[cq, ck], cq, ck] = 1.0
    t = jnp.einsum('lhro,ocd->lhrcd', rpb_all.astype(F32), jnp.asarray(onehot),
                   precision=lax.Precision.HIGHEST)
    t = jnp.where(jnp.asarray(in_win)[None, None, None], t, MASK_VALUE)
    depth, heads, n_off = t.shape[:3]
    t = t.reshape(depth, heads // 2, 2, n_off, GRID_W, GRID_W)
    t = jnp.transpose(t, (0, 1, 3, 2, 4, 5)).reshape(depth, heads // 2, n_off, 2 * GRID_W, GRID_W)
    return jnp.concatenate([t[:, :, :-1], t[:, :, 1:]], axis=-1)


def _na_body(q_ref, kbuf, vbuf, tbl_ref, o_ref, s_scr, e_scr, *, n_blocks):
    j = pl.program_id(1)
    band = NA_KH * GRID_W
    half = NA_KH // 2
    lane = lax.broadcasted_iota(jnp.int32, (GRID_W, 2 * NA_HEAD_DIM), 1)
    lo = lane < NA_HEAD_DIM
    qscale = NA_HEAD_DIM ** -0.5
    head_mask = (jnp.where(lo, qscale, 0.0).astype(BF16), jnp.where(lo, 0.0, qscale).astype(BF16))

    n_pairs = NA_HEADS // 2
    cols = [slice(p * 2 * NA_HEAD_DIM, (p + 1) * 2 * NA_HEAD_DIM) for p in range(n_pairs)]

    def row_offsets(rl):
        start_first = jnp.maximum(rl - half, 0)
        start_last = jnp.minimum(rl + half, NA_ROWS_PER_BLOCK)
        start = jnp.where(j == 0, start_first, jnp.where(j == n_blocks - 1, start_last, rl))
        delta = jnp.where(j == 0, jnp.minimum(rl, half), jnp.where(j == n_blocks - 1, jnp.maximum(rl, half), half))
        return pl.multiple_of(rl * GRID_W, GRID_W), pl.multiple_of(start * GRID_W, GRID_W), delta

    def scores(rl, slot):
        qrow, krow, delta = row_offsets(rl)
        for p in range(n_pairs):
            q2 = q_ref[pl.ds(qrow, GRID_W), cols[p]]
            qs = jnp.concatenate([q2 * head_mask[0], q2 * head_mask[1]], axis=0)
            k2 = kbuf[pl.ds(krow, band), cols[p]]
            s = lax.dot_general(qs, k2, (((1,), (1,)), ((), ())), preferred_element_type=F32)
            bias = jnp.concatenate([tbl_ref[p, 2 * m - delta + (NA_KH - 1)] for m in range(NA_KH // 2)], axis=1)
            s_scr[slot, p] = s + bias

    def softmax(slot):
        inv_l = []
        for p in range(n_pairs):
            s = s_scr[slot, p]
            m = jnp.max(s, axis=-1, keepdims=True)
            e = jnp.exp(s - m)
            inv_l.append(1.0 / jnp.sum(e, axis=-1, keepdims=True))
            e_scr[slot, p] = e.astype(BF16)
        return inv_l

    def weighted_values(rl, slot, inv_l):
        qrow, krow, _ = row_offsets(rl)
        for p in range(n_pairs):
            v2 = vbuf[pl.ds(krow, band), cols[p]]
            o = jnp.dot(e_scr[slot, p], v2, preferred_element_type=F32) * inv_l[p]
            o_ref[pl.ds(qrow, GRID_W), cols[p]] = jnp.where(lo, o[:GRID_W], o[GRID_W:]).astype(o_ref.dtype)

    def rows_body(it, carry):
        rows = [it * NA_ROWS_IN_FLIGHT + r for r in range(NA_ROWS_IN_FLIGHT)]
        for slot, rl in enumerate(rows):
            scores(rl, slot)
        inv = [softmax(slot) for slot in range(NA_ROWS_IN_FLIGHT)]
        for slot, rl in enumerate(rows):
            weighted_values(rl, slot, inv[slot])
        return carry

    lax.fori_loop(0, NA_ROWS_PER_BLOCK // NA_ROWS_IN_FLIGHT, rows_body, 0)


def _neighbourhood_attention(q, k, v, tables, layer, batch):
    n, width = q.shape
    n_pairs = NA_HEADS // 2
    blk = NA_ROWS_PER_BLOCK * GRID_W
    n_blocks = n // batch // blk
    assert n_blocks >= 2 and NA_KH == NA_ROWS_PER_BLOCK
    s = n // batch
    half_blk = blk // 2

    def window_start(b, j):
        start = b * s + jnp.clip(j * blk - half_blk, 0, s - 2 * blk)
        return pl.multiple_of(start, half_blk), 0

    tile = pl.BlockSpec((blk, width), lambda b, j: (b * n_blocks + j, 0))
    window = pl.BlockSpec((pl.Element(2 * blk), pl.Element(width)), window_start)
    return pl.pallas_call(
        functools.partial(_na_body, n_blocks=n_blocks),
        out_shape=jax.ShapeDtypeStruct((n, width), BF16),
        grid=(batch, n_blocks),
        in_specs=[tile, window, window,
                  pl.BlockSpec((None,) + tables.shape[1:], lambda b, j: (layer, 0, 0, 0, 0))],
        out_specs=tile,
        scratch_shapes=[pltpu.VMEM((NA_ROWS_IN_FLIGHT, n_pairs, 2 * GRID_W, NA_KH * GRID_W), F32),
                        pltpu.VMEM((NA_ROWS_IN_FLIGHT, n_pairs, 2 * GRID_W, NA_KH * GRID_W), BF16)],
        compiler_params=_cparams("arbitrary", "arbitrary"),
        name="neigh_attn",
    )(q, k, v, tables)


FN_N2 = 128
FN_GROUP_DIM = 64
FN_GROUP = 16


def _dft_cos_sin(n):
    ang = 2.0 * np.pi * (np.outer(np.arange(n), np.arange(n)) % n) / n
    return np.cos(ang), np.sin(ang)


def _fourier_stage1_body(x_ref, f_ref, tc_ref, ts_ref, zr_ref, zi_ref):
    n1, n_b, width = x_ref.shape
    xs = jnp.swapaxes(x_ref[...], 0, 1)
    rep = width // tc_ref.shape[2]
    zr_all, zi_all = [], []
    for b in range(n_b):
        z = jnp.dot(f_ref[...], xs[b], preferred_element_type=F32)
        zr, zi = z[:n1], z[n1:]
        tc = jnp.concatenate([tc_ref[b]] * rep, axis=1)
        ts = jnp.concatenate([ts_ref[b]] * rep, axis=1)
        zr_all.append((zr * tc + zi * ts).astype(zr_ref.dtype))
        zi_all.append((zi * tc - zr * ts).astype(zi_ref.dtype))
    zr_ref[...] = jnp.swapaxes(jnp.stack(zr_all, axis=0), 0, 1)
    zi_ref[...] = jnp.swapaxes(jnp.stack(zi_all, axis=0), 0, 1)


def _fourier_stage2_body(zr_ref, zi_ref, f_ref, c_ref, o_ref, *, scale):
    n2 = zr_ref.shape[1]
    outs = []
    for i in range(zr_ref.shape[0]):
        z = jnp.concatenate([zr_ref[i], zi_ref[i]], axis=0)
        y = jnp.dot(f_ref[...], z, preferred_element_type=F32).astype(BF16)
        lanes = c_ref.shape[1]
        cols = [slice(cb * lanes, (cb + 1) * lanes) for cb in range(y.shape[1] // lanes)]
        out = jnp.concatenate(
            [jnp.dot(jnp.concatenate([y[:n2, cs], y[n2:, cs]], axis=1), c_ref[...], preferred_element_type=F32)
             for cs in cols], axis=1) * scale
        outs.append(out.astype(o_ref.dtype))
    o_ref[...] = jnp.swapaxes(jnp.stack(outs, axis=0), 0, 1)


def _fourier_mix(u, batch, twiddles):
    n, width = u.shape
    s = n // batch
    n2 = FN_N2
    n1 = s // n2
    tc, ts = twiddles
    c1, s1 = _dft_cos_sin(n1)
    f1 = jnp.asarray(np.concatenate([c1, -s1], axis=0), F32).astype(BF16)
    c2, s2 = _dft_cos_sin(n2)
    f2 = jnp.asarray(np.block([[c2, s2], [-s2, c2]]), F32).astype(BF16)
    cg, sg = _dft_cos_sin(FN_GROUP_DIM)
    eye = np.eye(128 // FN_GROUP_DIM)
    fc = jnp.asarray(np.concatenate([np.kron(eye, cg), np.kron(eye, sg)], axis=0), F32).astype(BF16)

    grp = FN_GROUP
    pos_blk = pl.BlockSpec((None, n1, grp, width), lambda b, j: (b, 0, j, 0))
    tw_blk = pl.BlockSpec((grp,) + tc.shape[1:], lambda b, j: (j, 0, 0))
    zr, zi = pl.pallas_call(
        _fourier_stage1_body,
        out_shape=[jax.ShapeDtypeStruct((batch, n1, n2, width), BF16)] * 2,
        grid=(batch, n2 // grp),
        in_specs=[pos_blk, pl.BlockSpec((2 * n1, n1), lambda b, j: (0, 0)), tw_blk, tw_blk],
        out_specs=[pos_blk, pos_blk],
        compiler_params=_cparams("arbitrary", "arbitrary"),
        name="fourier_stage1",
    )(u.reshape(batch, n1, n2, width), f1, tc, ts)

    freq_blk = pl.BlockSpec((None, grp, n2, width), lambda b, j: (b, j, 0, 0))
    out = pl.pallas_call(
        functools.partial(_fourier_stage2_body, scale=float(1.0 / np.sqrt(s * FN_GROUP_DIM))),
        out_shape=jax.ShapeDtypeStruct((batch, n2, n1, width), BF16),
        grid=(batch, n1 // grp),
        in_specs=[freq_blk, freq_blk,
                  pl.BlockSpec((2 * n2, 2 * n2), lambda b, j: (0, 0)),
                  pl.BlockSpec(fc.shape, lambda b, j: (0, 0))],
        out_specs=pl.BlockSpec((None, n2, grp, width), lambda b, j: (b, 0, j, 0)),
        compiler_params=_cparams("arbitrary", "arbitrary"),
        name="fourier_stage2",
    )(zr, zi, f2, fc)
    return out.reshape(n, width)


def _fourier_twiddles(s):
    n2 = FN_N2
    n1 = s // n2
    ang = (2.0 * np.pi / s) * (jnp.arange(n2, dtype=F32)[:, None] * jnp.arange(n1, dtype=F32)[None, :])
    rep = lambda t: jnp.broadcast_to(t[:, :, None], (n2, n1, 128))
    return rep(jnp.cos(ang)), rep(jnp.sin(ang))


LRU_LANES = 128
LRU_SEGMENTS = 16
LRU_JCHUNK = 32
SUBLANES = 8


def _lru_gate_weights(wa_all, wx_all):
    depth, _, nb, db, _ = wa_all.shape
    ncol = nb // 2

    def blockdiag(w):
        w = w.reshape(depth, ncol, 2, db, db)
        z = jnp.zeros_like(w[:, :, 0])
        top = jnp.concatenate([w[:, :, 0], z], axis=-1)
        bot = jnp.concatenate([z, w[:, :, 1]], axis=-1)
        return jnp.concatenate([top, bot], axis=-2)

    parts = [blockdiag(wa_all[:, 0]), blockdiag(wx_all[:, 0]), blockdiag(wa_all[:, 1]), blockdiag(wx_all[:, 1])]
    return jnp.concatenate(parts, axis=-1).astype(BF16)


def _gelu_tanh(x):
    return 0.5 * x * (1.0 + jnp.tanh(np.sqrt(2.0 / np.pi) * (x + 0.044715 * (x * x * x))))


def _lru_body(useq_ref, cw_ref, cb_ref, w_ref, ba_ref, bx_ref, lam_ref, oseq_ref, ux_ref, o_ref, h_scr, p_scr):
    n_j, n_g, lanes = ux_ref.shape
    ux_ref[...] = jnp.swapaxes(useq_ref[...].reshape(n_g, n_j, lanes), 0, 1)
    jc = LRU_JCHUNK
    n_chunks = n_j // jc
    seg = lax.broadcasted_iota(jnp.int32, (n_g, lanes), 0)

    def from_prev_segment(x):
        return jnp.where(seg >= 1, pltpu.roll(x, 1, axis=0), 0.0)

    def from_next_segment(x):
        return jnp.where(seg < n_g - 1, pltpu.roll(x, n_g - 1, axis=0), 0.0)

    def conv_chunk(j0):
        main = ux_ref[pl.ds(j0, jc)].astype(F32)
        lo_in = ux_ref[pl.ds(jnp.maximum(j0 - CONV_PAD_LEFT, 0), CONV_PAD_LEFT)].astype(F32)
        tail = ux_ref[n_j - CONV_PAD_LEFT:n_j].astype(F32)
        lo_wrap = jnp.stack([from_prev_segment(tail[r]) for r in range(CONV_PAD_LEFT)], axis=0)
        lo = jnp.where(j0 > 0, lo_in, lo_wrap)
        n_hi = CONV_W - 1 - CONV_PAD_LEFT
        hi_in = ux_ref[pl.ds(jnp.minimum(j0 + jc, n_j - n_hi), n_hi)].astype(F32)
        head = ux_ref[0:n_hi].astype(F32)
        hi_wrap = jnp.stack([from_next_segment(head[r]) for r in range(n_hi)], axis=0)
        hi = jnp.where(j0 + jc < n_j, hi_in, hi_wrap)
        ext = jnp.concatenate([lo, main, hi], axis=0)
        c = ext[0:jc] * cw_ref[0:1, :] + cb_ref[...]
        for tap in range(1, CONV_W):
            c = c + ext[tap:tap + jc] * cw_ref[tap:tap + 1, :]
        return c

    def gates(c, pre, d, j0):
        r = _sigmoid(pre[:, 2 * d * lanes:(2 * d + 1) * lanes] + ba_ref[d:d + 1, :])
        i = _sigmoid(pre[:, (2 * d + 1) * lanes:(2 * d + 2) * lanes] + bx_ref[d:d + 1, :])
        lam = lam_ref[d:d + 1, :]
        softplus = jnp.maximum(-lam, 0.0) + jnp.log(1.0 + jnp.exp(-jnp.abs(lam)))
        a = jnp.exp(-LRU_C * r * softplus)
        om = 1.0 - a * a
        mult = jnp.where(om > 0.0, om * lax.rsqrt(om), 0.0)
        a = a.reshape(jc, n_g, lanes)
        b = (mult * i).reshape(jc, n_g, lanes) * c
        edge = (jc - 1) * d
        i_edge = i.reshape(jc, n_g, lanes)[edge]
        is_first = (seg == (n_g - 1) * d) & (j0 + edge == (n_j - 1) * d)
        b_edge = jnp.where(is_first, i_edge * c[edge], b[edge])[None]
        b = jnp.concatenate([b_edge, b[1:]] if d == 0 else [b[:-1], b_edge], axis=0)
        return a, b

    def local_scan(a, b, carry, d):
        h, p = carry
        hs, ps = [None] * jc, [None] * jc
        for jj in (range(jc) if d == 0 else range(jc - 1, -1, -1)):
            h = a[jj] * h + b[jj]
            p = a[jj] * p
            hs[jj], ps[jj] = h, p
        return jnp.stack(hs, axis=0), jnp.stack(ps, axis=0), (h, p)

    scan_init = (jnp.zeros((n_g, lanes), F32), jnp.ones((n_g, lanes), F32))

    def gate_and_forward_body(ci, carry):
        j0 = ci * jc
        rows = pl.ds(j0, jc)
        c = conv_chunk(j0)
        pre = jnp.dot(c.reshape(jc * n_g, lanes).astype(BF16), w_ref[...], preferred_element_type=F32)
        a_b, b_b = gates(c, pre, 1, j0)
        p_scr[1, rows] = a_b
        h_scr[1, rows] = b_b
        a_f, b_f = gates(c, pre, 0, j0)
        h_scr[0, rows], p_scr[0, rows], carry = local_scan(a_f, b_f, carry, 0)
        return carry

    lax.fori_loop(0, n_chunks, gate_and_forward_body, scan_init)

    def backward_body(ci, carry):
        rows = pl.ds((n_chunks - 1 - ci) * jc, jc)
        h_scr[1, rows], p_scr[1, rows], carry = local_scan(p_scr[1, rows], h_scr[1, rows], carry, 1)
        return carry

    lax.fori_loop(0, n_chunks, backward_body, scan_init)

    def carry_in(d):
        edge = n_j - 1 if d == 0 else 0
        h_end, p_end = h_scr[d, edge], p_scr[d, edge]
        state = jnp.zeros((1, lanes), F32)
        out = jnp.zeros((n_g, lanes), F32)
        for g in (range(n_g) if d == 0 else range(n_g - 1, -1, -1)):
            out = jnp.where(seg == g, state, out)
            state = h_end[g:g + 1] + p_end[g:g + 1] * state
        return out

    e_fwd, e_bwd = carry_in(0), carry_in(1)

    def out_body(ci, carry):
        j0 = ci * jc
        rows = pl.ds(j0, jc)
        h = h_scr[0, rows] + p_scr[0, rows] * e_fwd + h_scr[1, rows] + p_scr[1, rows] * e_bwd
        o_ref[rows] = h.astype(o_ref.dtype)
        return carry

    lax.fori_loop(0, n_chunks, out_body, 0)
    oseq_ref[...] = jnp.swapaxes(o_ref[...], 0, 1).reshape(n_g * n_j, lanes)


def _recurrent_branch(u_x, conv_w, conv_b, w_gate, ba, bx, lam, layer, batch):
    n, width = u_x.shape
    s = n // batch
    depth = conv_w.shape[0]
    ncol = width // LRU_LANES
    n_g = LRU_SEGMENTS
    n_j = s // n_g
    assert n_j % LRU_JCHUNK == 0
    cb3 = conv_b.reshape(depth, 1, width)
    seq_spec = pl.BlockSpec((None, s, LRU_LANES), lambda b, c: (b, 0, c))
    par = lambda rows: pl.BlockSpec((None, rows, LRU_LANES), lambda b, c: (layer, 0, c))
    seg_copy = pltpu.VMEM((n_j, n_g, LRU_LANES), BF16)
    state = pltpu.VMEM((2, n_j, n_g, LRU_LANES), F32)
    out = pl.pallas_call(
        _lru_body,
        out_shape=jax.ShapeDtypeStruct((batch, s, width), BF16),
        grid=(batch, ncol),
        in_specs=[seq_spec, par(CONV_W), par(1),
                  pl.BlockSpec((None, None, LRU_LANES, 4 * LRU_LANES), lambda b, c: (layer, c, 0, 0)),
                  par(2), par(2), par(2)],
        out_specs=seq_spec,
        scratch_shapes=[seg_copy, seg_copy, state, state],
        compiler_params=_cparams("arbitrary", "arbitrary"),
        name="rg_lru",
    )(u_x.reshape(batch, s, width), conv_w, cb3, w_gate, ba, bx, lam)
    return out.reshape(n, width)


def _merge_body(ya_ref, yb_ref, hc_ref, ug_ref, xn_ref, h_ref, wgate_ref, bgate_ref, wb_ref, wo_ref, o_ref,
                wgate16, wb16, wo16):
    d = h_ref.shape[1]

    @pl.when(pl.program_id(0) == 0)
    def _():
        for kbr in range(wgate16.shape[1] // d):
            wgate16[:, kbr * d:(kbr + 1) * d] = wgate_ref[:, kbr * d:(kbr + 1) * d].astype(BF16)
            wb16[kbr] = wb_ref[kbr].astype(BF16)
        wo16[...] = wo_ref[...].astype(BF16)

    xn = xn_ref[...]
    yc = (hc_ref[...].astype(F32) * _gelu_tanh(ug_ref[...].astype(F32))).astype(BF16)
    merged = None
    for kbr, y in enumerate((ya_ref[...], yb_ref[...], yc)):
        cols = slice(kbr * d, (kbr + 1) * d)
        gate = jnp.dot(xn, wgate16[:, cols], preferred_element_type=F32) + bgate_ref[:, cols]
        ybr = jnp.dot(y, wb16[kbr], preferred_element_type=F32)
        term = _sigmoid(gate) * ybr
        merged = term if merged is None else merged + term
    o_ref[...] = h_ref[...] + jnp.dot(merged.astype(BF16), wo16[...], preferred_element_type=F32)


def _merge(ya, yb, hc, ug, xn, h, w_in_all, b_in_all, wb_all, wo_all, layer, tm=512):
    n, d = h.shape
    bw = ya.shape[1]
    depth = w_in_all.shape[0]
    gate_block = w_in_all.shape[2] // (3 * d) - 1
    assert (gate_block + 1) * 3 * d == w_in_all.shape[2]
    once = pl.Buffered(1)
    ytile = pl.BlockSpec((tm, bw), lambda t: (t, 0))
    return pl.pallas_call(
        _merge_body,
        out_shape=jax.ShapeDtypeStruct((n, d), F32),
        grid=(n // tm,),
        in_specs=[ytile, ytile, ytile, ytile,
                  pl.BlockSpec((tm, d), lambda t: (t, 0)),
                  pl.BlockSpec((tm, d), lambda t: (t, 0)),
                  pl.BlockSpec((None, d, 3 * d), lambda t: (layer, 0, gate_block), pipeline_mode=once),
                  pl.BlockSpec((None, 1, 3 * d), lambda t: (layer, 0, gate_block)),
                  pl.BlockSpec((None, 3, bw, d), lambda t: (layer, 0, 0, 0), pipeline_mode=once),
                  pl.BlockSpec((None, d, d), lambda t: (layer, 0, 0), pipeline_mode=once)],
        out_specs=pl.BlockSpec((tm, d), lambda t: (t, 0)),
        scratch_shapes=[pltpu.VMEM((d, 3 * d), BF16), pltpu.VMEM((3, bw, d), BF16), pltpu.VMEM((d, d), BF16)],
        compiler_params=_cparams("arbitrary"),
        name="branch_merge",
    )(ya, yb, hc, ug, xn, h, w_in_all, b_in_all.reshape(depth, 1, -1), wb_all, wo_all)


def _ffn_body(h_ref, g_ref, gnext_ref, wg_ref, wu_ref, wd_ref, o_ref, xnext_ref, xn_ref):
    j = pl.program_id(1)

    @pl.when(j == 0)
    def _():
        h = h_ref[...]
        xn_ref[...] = _rms_norm_f32(h, g_ref[...]).astype(BF16)
        o_ref[...] = h

    xn = xn_ref[...]
    gate = jnp.dot(xn, wg_ref[...].astype(BF16), preferred_element_type=F32)
    up = jnp.dot(xn, wu_ref[...].astype(BF16), preferred_element_type=F32)
    act = gate * _sigmoid(gate) * up
    o_ref[...] += jnp.dot(act.astype(BF16), wd_ref[...].astype(BF16), preferred_element_type=F32)

    @pl.when(j == pl.num_programs(1) - 1)
    def _():
        xnext_ref[...] = _rms_norm_f32(o_ref[...], gnext_ref[...]).astype(xnext_ref.dtype)


def _ffn(h, g, g_next, w_gu_all, w_down_all, widx, tm=1024, tf=768):
    n, d = h.shape
    nf = w_down_all.shape[-2] // tf
    gain = pl.BlockSpec((1, d), lambda t, j: (0, 0))
    tile = pl.BlockSpec((tm, d), lambda t, j: (t, 0))
    return pl.pallas_call(
        _ffn_body,
        out_shape=[jax.ShapeDtypeStruct((n, d), F32), jax.ShapeDtypeStruct((n, d), BF16)],
        grid=(n // tm, nf),
        in_specs=[tile, gain, gain,
                  pl.BlockSpec((None, d, tf), lambda t, j: (widx, 0, j)),
                  pl.BlockSpec((None, d, tf), lambda t, j: (widx, 0, j + nf)),
                  pl.BlockSpec((None, tf, d), lambda t, j: (widx, j, 0))],
        out_specs=[tile, tile],
        scratch_shapes=[pltpu.VMEM((tm, d), BF16)],
        compiler_params=_cparams("arbitrary", "arbitrary"),
        name="dense_ffn",
    )(h, g.reshape(1, d), g_next.reshape(1, d), w_gu_all, w_gu_all, w_down_all)


MOE_TOKEN_TILE = 512
MOE_ROW_TILE = 512
MOE_SEG_ALIGN = 16
MOE_TOP_K = 2
MOE_CHUNK_SIZES = tuple(MOE_SEG_ALIGN << b for b in range(5, -1, -1))
assert MOE_CHUNK_SIZES[0] == MOE_TOKEN_TILE


def _moe_compact_rows(ne):
    rows = MOE_TOKEN_TILE * MOE_TOP_K + ne * (MOE_SEG_ALIGN - 1)
    return -(-rows // MOE_SEG_ALIGN) * MOE_SEG_ALIGN


def _moe_sorted_rows(n, ne):
    rows = n * MOE_TOP_K + (n // MOE_TOKEN_TILE) * ne * (MOE_SEG_ALIGN - 1) + ne * (MOE_ROW_TILE - MOE_SEG_ALIGN)
    return -(-rows // MOE_ROW_TILE) * MOE_ROW_TILE


def _router_body(h_ref, g_ref, wrt_ref, xn_ref, posr_ref, wrow_ref, posc_ref, cnt_ref, before_scr, eye_scr):
    t_tokens = h_ref.shape[0]

    @pl.when(pl.program_id(0) == 0)
    def _():
        r_i = lax.broadcasted_iota(jnp.int32, (t_tokens, t_tokens), 0)
        c_i = lax.broadcasted_iota(jnp.int32, (t_tokens, t_tokens), 1)
        before_scr[...] = jnp.where(r_i < c_i, 1.0, 0.0).astype(BF16)
        eye_scr[...] = jnp.where(r_i == c_i, 1.0, 0.0).astype(BF16)

    xn = _rms_norm_f32(h_ref[...], g_ref[...])
    xn_hi = xn.astype(BF16)
    xn_ref[...] = xn_hi
    nt_dims = (((1,), (1,)), ((), ()))
    xn_lo = (xn - xn_hi.astype(F32)).astype(BF16)
    w = wrt_ref[...]
    ne = w.shape[0]
    w_hi = w.astype(BF16)
    w_lo = (w - w_hi.astype(F32)).astype(BF16)
    by_hi = lax.dot_general(jnp.concatenate([w_hi, w_lo], axis=0), xn_hi, nt_dims, preferred_element_type=F32)
    logits = by_hi[:ne] + by_hi[ne:] + lax.dot_general(w_hi, xn_lo, nt_dims, preferred_element_type=F32)
    sub = lax.broadcasted_iota(jnp.int32, logits.shape, 0)
    m1 = jnp.max(logits, axis=0, keepdims=True)
    i1 = jnp.min(jnp.where(logits == m1, sub, ne), axis=0, keepdims=True)
    rest = jnp.where(sub == i1, -jnp.inf, logits)
    m2 = jnp.max(rest, axis=0, keepdims=True)
    i2 = jnp.min(jnp.where(rest == m2, sub, ne), axis=0, keepdims=True)
    e = jnp.exp(m2 - m1)
    wrow_ref[...] = jnp.concatenate([1.0 / (1.0 + e), e / (1.0 + e)], axis=0)

    sel1, sel2 = sub == i1, sub == i2
    memb = jnp.where(sel1, 1.0, jnp.where(sel2, 1.0, 0.0))
    rank = jnp.dot(memb.astype(BF16), before_scr[...], preferred_element_type=F32)
    sub_c = lax.broadcasted_iota(jnp.int32, cnt_ref.shape, 0)
    cnt_out = jnp.zeros(cnt_ref.shape, F32)
    base = rank
    running = jnp.zeros((1, 1), F32)
    for ex in range(ne):
        c = jnp.sum(memb[ex:ex + 1, :], axis=1, keepdims=True)
        cnt_out = jnp.where(sub_c == ex, c, cnt_out)
        base = jnp.where(sub == ex, base + running, base)
        running = running + jnp.floor((c + (MOE_SEG_ALIGN - 1)) * (1.0 / MOE_SEG_ALIGN)) * MOE_SEG_ALIGN
    cnt_ref[...] = cnt_out.astype(jnp.int32)
    pos1 = jnp.sum(jnp.where(sel1, base, 0.0), axis=0, keepdims=True)
    pos2 = jnp.sum(jnp.where(sel2, base, 0.0), axis=0, keepdims=True)
    posr_ref[...] = jnp.concatenate([pos1, pos2], axis=0).astype(jnp.int32)
    digits = []
    for pos in (pos1, pos2):
        hi = jnp.floor(pos * (1.0 / 128.0))
        digits += [hi, pos - 128.0 * hi]
    pad = jnp.zeros((2 * SUBLANES - len(digits), t_tokens), F32)
    dig_t = lax.dot_general(eye_scr[...], jnp.concatenate(digits + [pad], axis=0).astype(BF16), nt_dims,
                            preferred_element_type=F32)
    d_i = lax.broadcasted_iota(jnp.int32, (2 * SUBLANES, posc_ref.shape[1]), 0)
    c_i = lax.broadcasted_iota(jnp.int32, (2 * SUBLANES, posc_ref.shape[1]), 1)
    recombine = jnp.where(d_i == 2 * c_i, 128.0, jnp.where(d_i == 2 * c_i + 1, 1.0, 0.0)).astype(BF16)
    posc = jnp.dot(dig_t.astype(BF16), recombine, preferred_element_type=F32)
    posc_ref[...] = posc.astype(jnp.int32)


def _router(h, g_all, layer, wr_all, widx):
    n, d = h.shape
    depth = g_all.shape[0]
    ne = wr_all.shape[-1]
    tm = MOE_TOKEN_TILE
    nt = n // tm
    wrt = jnp.swapaxes(wr_all, 1, 2)
    return pl.pallas_call(
        _router_body,
        out_shape=[jax.ShapeDtypeStruct((n, d), BF16),
                   jax.ShapeDtypeStruct((MOE_TOP_K, n), jnp.int32),
                   jax.ShapeDtypeStruct((MOE_TOP_K, n), F32),
                   jax.ShapeDtypeStruct((n, SUBLANES), jnp.int32),
                   jax.ShapeDtypeStruct((nt, ne, 128), jnp.int32)],
        grid=(nt,),
        in_specs=[pl.BlockSpec((tm, d), lambda t: (t, 0)),
                  pl.BlockSpec((None, 1, d), lambda t: (layer, 0, 0)),
                  pl.BlockSpec((None, ne, d), lambda t: (widx, 0, 0))],
        out_specs=[pl.BlockSpec((tm, d), lambda t: (t, 0)),
                   pl.BlockSpec((MOE_TOP_K, tm), lambda t: (0, t)),
                   pl.BlockSpec((MOE_TOP_K, tm), lambda t: (0, t)),
                   pl.BlockSpec((tm, SUBLANES), lambda t: (t, 0)),
                   pl.BlockSpec((None, ne, 128), lambda t: (t, 0, 0))],
        scratch_shapes=[pltpu.VMEM((tm, tm), BF16), pltpu.VMEM((tm, tm), BF16)],
        compiler_params=_cparams("arbitrary"),
        name="router",
    )(h, g_all.reshape(depth, 1, d), wrt)


def _moe_tables(cnt, n_sorted_rows):
    cnt = cnt[:, :, 0]
    nt, ne = cnt.shape
    seg = (cnt + (MOE_SEG_ALIGN - 1)) // MOE_SEG_ALIGN * MOE_SEG_ALIGN
    seg_off = jnp.cumsum(seg, axis=1) - seg
    e_rows = jnp.sum(seg, axis=0)
    e_tiles = (e_rows + (MOE_ROW_TILE - 1)) // MOE_ROW_TILE
    e_cum = jnp.cumsum(e_tiles)
    e_base = (e_cum - e_tiles) * MOE_ROW_TILE
    dst = e_base[None, :] + jnp.cumsum(seg, axis=0) - seg
    n_used = e_cum[-1]
    tile_ids = jnp.minimum(jnp.arange(n_sorted_rows // MOE_ROW_TILE, dtype=jnp.int32), n_used - 1)
    tile_expert = jnp.sum(tile_ids[:, None] >= e_cum[None, :], axis=1)
    last_tile_row = e_base + (e_tiles - 1) * MOE_ROW_TILE
    i32 = lambda a: a.astype(jnp.int32)
    return dict(seg_off=i32(seg_off.reshape(-1)), seg_len=i32(seg.reshape(-1)), dst=i32(dst.reshape(-1)),
                n_used=i32(n_used.reshape(1)), tile_expert=i32(tile_expert),
                last_tile_row=i32(last_tile_row), has_rows=i32(e_tiles > 0))


def _moe_chunk_copies(t, ne, seg_off_ref, seg_len_ref, dst_ref, make_copy, act):
    for ex in range(ne):
        idx = t * ne + ex
        off = seg_off_ref[idx]
        ln = seg_len_ref[idx]
        row = dst_ref[idx]
        for size in MOE_CHUNK_SIZES:
            take = (ln & size) != 0

            @pl.when(take)
            def _(off=off, row=row, size=size):
                for cp in make_copy(pl.multiple_of(off, MOE_SEG_ALIGN), pl.multiple_of(row, MOE_SEG_ALIGN), size):
                    act(cp)

            step = jnp.where(take, size, 0)
            off = off + step
            row = row + step


def _dispatch_body(seg_off_ref, seg_len_ref, dst_ref, last_row_ref, has_rows_ref, n_used_ref,
                   xn_ref, posr_ref, wrow_ref, xs_ref, ws_ref, cbuf, wbuf, zx, zw, sems, zsem):
    t = pl.program_id(0)
    n_tiles = pl.num_programs(0)
    ne = last_row_ref.shape[0]
    n_rows, t_tokens = cbuf.shape[1], xn_ref.shape[0]

    def zero_copies(row):
        row = pl.multiple_of(row, MOE_ROW_TILE)
        return (pltpu.make_async_copy(zx, xs_ref.at[pl.ds(row, MOE_ROW_TILE)], zsem),
                pltpu.make_async_copy(zw, ws_ref.at[pl.ds(row, MOE_ROW_TILE)], zsem))

    @pl.when(t == 0)
    def _():
        zx[...] = jnp.zeros_like(zx)
        zw[...] = jnp.zeros_like(zw)
        for act in (lambda cp: cp.start(), lambda cp: cp.wait()):
            for ex in range(ne):
                @pl.when(has_rows_ref[ex] != 0)
                def _(ex=ex):
                    for cp in zero_copies(last_row_ref[ex]):
                        act(cp)

        def tail_body(i, carry):
            for cp in zero_copies(i * MOE_ROW_TILE):
                cp.start()
                cp.wait()
            return carry

        lax.fori_loop(n_used_ref[0], xs_ref.shape[0] // MOE_ROW_TILE, tail_body, 0)

    r_iota = lax.broadcasted_iota(jnp.int32, (n_rows, t_tokens), 0)
    hit1 = r_iota == posr_ref[0:1, :]
    hit2 = r_iota == posr_ref[1:2, :]
    onehot = jnp.where(hit1, 1.0, jnp.where(hit2, 1.0, 0.0)).astype(BF16)
    slot = t % 2
    cbuf[slot] = jnp.dot(onehot, xn_ref[...], preferred_element_type=F32).astype(BF16)
    wsel = jnp.where(hit1, wrow_ref[0:1, :], jnp.where(hit2, wrow_ref[1:2, :], 0.0))
    wbuf[slot] = jnp.broadcast_to(jnp.sum(wsel, axis=1, keepdims=True), wbuf.shape[1:])

    def drain(tile, act):
        s = tile % 2

        def make_copy(off, row, size):
            return (pltpu.make_async_copy(cbuf.at[s, pl.ds(off, size)], xs_ref.at[pl.ds(row, size)], sems.at[s]),
                    pltpu.make_async_copy(wbuf.at[s, pl.ds(off, size)], ws_ref.at[pl.ds(row, size)], sems.at[s]))

        _moe_chunk_copies(tile, ne, seg_off_ref, seg_len_ref, dst_ref, make_copy, act)

    drain(t, lambda cp: cp.start())

    @pl.when(t > 0)
    def _():
        drain(t - 1, lambda cp: cp.wait())

    @pl.when(t == n_tiles - 1)
    def _():
        drain(t, lambda cp: cp.wait())


def _dispatch(xn, posr, wrow, tables, n_sorted_rows, ne):
    n, d = xn.shape
    tm = MOE_TOKEN_TILE
    n_rows = _moe_compact_rows(ne)
    grid_spec = pltpu.PrefetchScalarGridSpec(
        num_scalar_prefetch=6,
        grid=(n // tm,),
        in_specs=[pl.BlockSpec((tm, d), lambda t, *_: (t, 0)),
                  pl.BlockSpec((MOE_TOP_K, tm), lambda t, *_: (0, t)),
                  pl.BlockSpec((MOE_TOP_K, tm), lambda t, *_: (0, t))],
        out_specs=[pl.BlockSpec(memory_space=pl.ANY), pl.BlockSpec(memory_space=pl.ANY)],
        scratch_shapes=[pltpu.VMEM((2, n_rows, d), BF16), pltpu.VMEM((2, n_rows, 128), F32),
                        pltpu.VMEM((MOE_ROW_TILE, d), BF16), pltpu.VMEM((MOE_ROW_TILE, 128), F32),
                        pltpu.SemaphoreType.DMA((2,)), pltpu.SemaphoreType.DMA],
    )
    return pl.pallas_call(
        _dispatch_body,
        out_shape=[jax.ShapeDtypeStruct((n_sorted_rows, d), BF16), jax.ShapeDtypeStruct((n_sorted_rows, 128), F32)],
        grid_spec=grid_spec,
        compiler_params=_cparams("arbitrary"),
        name="moe_dispatch",
    )(tables['seg_off'], tables['seg_len'], tables['dst'], tables['last_tile_row'], tables['has_rows'], tables['n_used'],
      xn, posr, wrow)


def _experts_body(tile_expert_ref, n_used_ref, xs_ref, ws_ref, wgu_ref, wd_ref, ys_ref, wgu16, wd16):
    i = pl.program_id(0)
    used = i < n_used_ref[0]
    prev = tile_expert_ref[jnp.maximum(i - 1, 0)]
    new_expert = jnp.logical_or(i == 0, tile_expert_ref[i] != prev)
    f = wd_ref.shape[0]
    fc = 512

    @pl.when(jnp.logical_and(used, new_expert))
    def _():
        for c in range(2 * f // fc):
            wgu16[:, c * fc:(c + 1) * fc] = wgu_ref[:, c * fc:(c + 1) * fc].astype(BF16)
        for c in range(f // fc):
            wd16[c * fc:(c + 1) * fc, :] = wd_ref[c * fc:(c + 1) * fc, :].astype(BF16)

    @pl.when(used)
    def _():
        x = xs_ref[...]
        w = ws_ref[...]
        wrep = jnp.concatenate([w] * (fc // w.shape[1]), axis=1)
        acc = None
        for c in range(f // fc):
            gate = jnp.dot(x, wgu16[:, c * fc:(c + 1) * fc], preferred_element_type=F32)
            up = jnp.dot(x, wgu16[:, f + c * fc:f + (c + 1) * fc], preferred_element_type=F32)
            act = (gate * _sigmoid(gate) * up * wrep).astype(BF16)
            part = jnp.dot(act, wd16[c * fc:(c + 1) * fc, :], preferred_element_type=F32)
            acc = part if acc is None else acc + part
        ys_ref[...] = acc.astype(ys_ref.dtype)

    @pl.when(jnp.logical_not(used))
    def _():
        ys_ref[...] = jnp.zeros_like(ys_ref)


def _experts(xs, ws, tables, w_gu_all, w_down_all, widx):
    rows, d = xs.shape
    f = w_down_all.shape[-2]
    tile = lambda w: pl.BlockSpec((MOE_ROW_TILE, w), lambda i, te, nu: (jnp.maximum(jnp.minimum(i, nu[0] - 1), 0), 0))
    grid_spec = pltpu.PrefetchScalarGridSpec(
        num_scalar_prefetch=2,
        grid=(rows // MOE_ROW_TILE,),
        in_specs=[tile(d), tile(ws.shape[1]),
                  pl.BlockSpec((None, None, d, 2 * f), lambda i, te, nu: (widx, te[i], 0, 0)),
                  pl.BlockSpec((None, None, f, d), lambda i, te, nu: (widx, te[i], 0, 0))],
        out_specs=pl.BlockSpec((MOE_ROW_TILE, d), lambda i, te, nu: (i, 0)),
        scratch_shapes=[pltpu.VMEM((d, 2 * f), BF16), pltpu.VMEM((f, d), BF16)],
    )
    return pl.pallas_call(
        _experts_body,
        out_shape=jax.ShapeDtypeStruct((rows, d), BF16),
        grid_spec=grid_spec,
        compiler_params=_cparams("arbitrary"),
        name="moe_experts",
    )(tables['tile_expert'], tables['n_used'], xs, ws, w_gu_all, w_down_all)


def _combine_body(seg_off_ref, seg_len_ref, dst_ref, ys_ref, posc_ref, h_ref, gain_ref, *rest, ne, final):
    *out_refs, ybuf, sems = rest
    t = pl.program_id(0)
    n_tiles = pl.num_programs(0)
    t_tokens, n_rows = h_ref.shape[0], ybuf.shape[1]

    def fetch(tile, act):
        slot = tile % 2

        def make_copy(off, row, size):
            return (pltpu.make_async_copy(ys_ref.at[pl.ds(row, size)], ybuf.at[slot, pl.ds(off, size)], sems.at[slot]),)

        _moe_chunk_copies(tile, ne, seg_off_ref, seg_len_ref, dst_ref, make_copy, act)

    @pl.when(t == 0)
    def _():
        ybuf[...] = jnp.zeros_like(ybuf)
        fetch(t, lambda cp: cp.start())

    @pl.when(t + 1 < n_tiles)
    def _():
        fetch(t + 1, lambda cp: cp.start())

    fetch(t, lambda cp: cp.wait())

    pos = posc_ref[...]
    lane_r = lax.broadcasted_iota(jnp.int32, (t_tokens, n_rows), 1)
    onehot = jnp.where(lane_r == pos[:, 0:1], 1.0, jnp.where(lane_r == pos[:, 1:2], 1.0, 0.0)).astype(BF16)
    out = h_ref[...] + jnp.dot(onehot, ybuf[t % 2], preferred_element_type=F32)
    normed = _rms_norm_f32(out, gain_ref[...])
    if final:
        out_refs[0][...] = normed
    else:
        out_refs[0][...] = out
        out_refs[1][...] = normed.astype(out_refs[1].dtype)


def _combine(ys, posc, h, tables, ne, gain, final):
    n, d = h.shape
    tm = MOE_TOKEN_TILE
    tile = pl.BlockSpec((tm, d), lambda t, *_: (t, 0))
    grid_spec = pltpu.PrefetchScalarGridSpec(
        num_scalar_prefetch=3,
        grid=(n // tm,),
        in_specs=[pl.BlockSpec(memory_space=pl.ANY),
                  pl.BlockSpec((tm, posc.shape[1]), lambda t, *_: (t, 0)),
                  tile,
                  pl.BlockSpec((1, d), lambda t, *_: (0, 0))],
        out_specs=[tile] if final else [tile, tile],
        scratch_shapes=[pltpu.VMEM((2, _moe_compact_rows(ne), d), BF16), pltpu.SemaphoreType.DMA((2,))],
    )
    res = jax.ShapeDtypeStruct((n, d), F32)
    return pl.pallas_call(
        functools.partial(_combine_body, ne=ne, final=final),
        out_shape=[res] if final else [res, jax.ShapeDtypeStruct((n, d), BF16)],
        grid_spec=grid_spec,
        compiler_params=_cparams("arbitrary"),
        name="moe_combine",
    )(tables['seg_off'], tables['seg_len'], tables['dst'], ys, posc, h, gain.reshape(1, d))


def _moe(h, g_all, layer, wr_all, w_gu_all, w_down_all, widx, gain, final):
    n = h.shape[0]
    ne = wr_all.shape[-1]
    xn, posr, wrow, posc, cnt = _router(h, g_all, layer, wr_all, widx)
    n_sorted_rows = _moe_sorted_rows(n, ne)
    tables = _moe_tables(cnt, n_sorted_rows)
    xs, ws = _dispatch(xn, posr, wrow, tables, n_sorted_rows, ne)
    ys = _experts(xs, ws, tables, w_gu_all, w_down_all, widx)
    return _combine(ys, posc, h, tables, ne, gain, final)


def kernel(x, norm_mix_g, w_in, b_in, na_rpb, conv_w, conv_b, lru_wa, lru_ba, lru_wx, lru_bx, lru_lambda, w_branch, w_out, norm_ffn_g, ffn_w_gu, ffn_w_down, router_w, moe_w_gu, moe_w_down, final_g):
    batch, seq, d = x.shape
    depth = w_in.shape[0]
    n = batch * seq
    bw = w_branch.shape[2]
    h = x.reshape(n, d)

    tables = _na_bias_tables(na_rpb)
    twiddles = _fourier_twiddles(seq)
    w_gate = _lru_gate_weights(lru_wa, lru_wx)
    tn = 6 * bw

    xn = _rms_norm(h, norm_mix_g[0], BF16)
    for l in range(depth):
        last = l == depth - 1
        q, k, v, u_f, u_x, u_g = _proj(xn, w_in, b_in, l, 0, tn, 6)
        y_a = _neighbourhood_attention(q, k, v, tables, l, batch)
        y_b = _fourier_mix(u_f, batch, twiddles)
        h_c = _recurrent_branch(u_x, conv_w, conv_b, w_gate, lru_ba, lru_bx, lru_lambda, l, batch)
        h = _merge(y_a, y_b, h_c, u_g, xn, h, w_in, b_in, w_branch, w_out, l)
        next_gain = final_g if last else norm_mix_g[l + 1]
        if l % 2 == 0:
            h, xn = _ffn(h, norm_ffn_g[l], next_gain, ffn_w_gu, ffn_w_down, l // 2)
            if last:
                h = _rms_norm(h, final_g, F32)
        elif last:
            (h,) = _moe(h, norm_ffn_g, l, router_w, moe_w_gu, moe_w_down, l // 2, next_gain, final=True)
        else:
            h, xn = _moe(h, norm_ffn_g, l, router_w, moe_w_gu, moe_w_down, l // 2, next_gain, final=False)
    return h.reshape(batch, seq, d)
```
